```python
import jax, jax.numpy as jnp
from jax import lax
import numpy as np

D_MODEL = 1024
BATCH = 2
SEQ = 8192
DEPTH = 2
DEC_BATCH = 8
DEC_SEQ = 4096
PAST_LEN = 128

GRID_W = 64
NA_HEAD_DIM = 32
NA_HEADS = D_MODEL // NA_HEAD_DIM
NA_ROWS = 8
NA_COLS = 16
NA_Q_COLS = 16
NA_K_COLS = NA_Q_COLS + NA_COLS
RWKV_HEAD_DIM = 64
RWKV_HEADS = D_MODEL // RWKV_HEAD_DIM
DECAY_LORA = 64
ICLR_LORA = 64
GATE_LORA = 160
GN_EPS = 64e-5
D_FF = 2816
N_EXPERTS = 8
TOP_K = 2
D_EXPERT = 3584
LN_EPS = 1e-5
N_EVEN = (DEPTH + 1) // 2
N_ODD = DEPTH // 2
ALPHA = (2 * DEPTH) ** 0.25
BETA = (8 * DEPTH) ** -0.25

kernel_name = 'hybrid_na_rwkv7_deepnorm_encoder'


def layer_norm(x, g, b):
    xf = x.astype(jnp.float32)
    xc = xf - jnp.mean(xf, axis=-1, keepdims=True)
    var = jnp.mean(xc * xc, axis=-1, keepdims=True)
    return (xc * lax.rsqrt(var + LN_EPS) * g.astype(jnp.float32) + b.astype(jnp.float32)).astype(x.dtype)


def _na_column_layout():
    n_cb = GRID_W // NA_Q_COLS
    q_cols = np.arange(GRID_W).reshape(n_cb, NA_Q_COLS)
    q_start = np.clip(q_cols - NA_COLS // 2, 0, GRID_W - NA_COLS)
    blk_start = np.clip(np.arange(n_cb) * NA_Q_COLS - NA_COLS // 2, 0, GRID_W - NA_K_COLS)
    key_cols = blk_start[:, None] + np.arange(NA_K_COLS)
    kc = key_cols[:, None, :]
    valid = (kc >= q_start[..., None]) & (kc < q_start[..., None] + NA_COLS)
    dc_idx = np.clip(kc - q_cols[..., None] + NA_COLS - 1, 0, 2 * NA_COLS - 2)
    return key_cols, valid, dc_idx


def neighbourhood_attention(x, w_qkv, rpb, w_o):
    b, l, d = x.shape
    rows = l // GRID_W
    kr = min(NA_ROWS, rows)
    n_cb = GRID_W // NA_Q_COLS
    key_cols, valid, dc_idx = _na_column_layout()
    qkv = jnp.einsum('bld,de->ble', x, w_qkv).reshape(b, rows, GRID_W, 3, NA_HEADS, NA_HEAD_DIM)
    q = qkv[:, :, :, 0] * (NA_HEAD_DIM ** -0.5)
    k = qkv[:, :, :, 1]
    v = qkv[:, :, :, 2]
    mask = np.broadcast_to(valid[:, :, None, :], (n_cb, NA_Q_COLS, kr, NA_K_COLS)).reshape(n_cb, NA_Q_COLS, kr * NA_K_COLS)

    def row_block(r):
        rs = jnp.clip(r - kr // 2, 0, rows - kr)
        q_r = lax.dynamic_index_in_dim(q, r, axis=1, keepdims=False).reshape(b, n_cb, NA_Q_COLS, NA_HEADS, NA_HEAD_DIM)

        def gather(t):
            t_r = lax.dynamic_slice_in_dim(t, rs, kr, axis=1)[:, :, key_cols]
            return jnp.moveaxis(t_r, 2, 1).reshape(b, n_cb, kr * NA_K_COLS, NA_HEADS, NA_HEAD_DIM)

        k_blk = gather(k)
        v_blk = gather(v)
        dr_idx = rs + jnp.arange(kr) - r + NA_ROWS - 1
        bias = rpb[:, dr_idx][:, :, dc_idx]
        bias = jnp.transpose(bias, (0, 2, 3, 1, 4)).reshape(NA_HEADS, n_cb, NA_Q_COLS, kr * NA_K_COLS)
        s = jnp.einsum('bcqhd,bckhd->bhcqk', q_r, k_blk).astype(jnp.float32) + bias.astype(jnp.float32)
        s = jnp.where(mask, s, -jnp.inf)
        p = jax.nn.softmax(s, axis=-1).astype(v.dtype)
        o = jnp.einsum('bhcqk,bckhd->bcqhd', p, v_blk)
        return o.reshape(b, GRID_W, d)

    out = lax.map(row_block, jnp.arange(rows))
    out = jnp.moveaxis(out, 0, 1).reshape(b, l, d)
    return jnp.einsum('bld,de->ble', out, w_o)


def _delta_rule_scan(r, w, k, v, a, bb, reverse):
    def step(S, inp):
        r_t, w_t, k_t, v_t, a_t, b_t = inp
        sa = jnp.einsum('bhvk,bhk->bhv', S, a_t)
        S = S * w_t[:, :, None, :] + sa[..., None] * b_t[:, :, None, :] + v_t[..., None] * k_t[:, :, None, :]
        return S, jnp.einsum('bhvk,bhk->bhv', S, r_t)
    s0 = jnp.zeros(r.shape[1:3] + (RWKV_HEAD_DIM, RWKV_HEAD_DIM), jnp.float32)
    _, y = lax.scan(step, s0, (r, w, k, v, a, bb), reverse=reverse)
    return jnp.moveaxis(y, 0, 1)


def rwkv7_bidirectional(x, mu, w_rkv, w0, w1, w2, a0, a1, a2, g1, g2, k_k, k_a, r_k, lnx_g, lnx_b, w_o):
    b, l, d = x.shape
    f32 = jnp.float32
    xp = jnp.pad(x, ((0, 0), (1, 1), (0, 0)))
    xx = 0.5 * (xp[:, :-2] + xp[:, 2:]) - x
    mix = lambda i: x + xx * mu[i]
    r, k, v = jnp.einsum('nbld,nde->nble', jnp.stack([mix(0), mix(2), mix(3)]), w_rkv)
    wl = w0[:, None, None, :] + jnp.einsum('zblr,zrd->zbld', jnp.tanh(jnp.einsum('bld,zdr->zblr', mix(1), w1)), w2)
    decay = jnp.exp(-jnp.exp(-jax.nn.softplus(-wl.astype(f32)) - 0.5))
    a = jax.nn.sigmoid((a0[:, None, None, :] + jnp.einsum('zblr,zrd->zbld', jnp.einsum('bld,zdr->zblr', mix(4), a1), a2)).astype(f32))
    g = jnp.einsum('blr,rd->bld', jax.nn.sigmoid(jnp.einsum('bld,dr->blr', mix(5), g1)), g2)
    heads = lambda t: t.reshape(t.shape[:-1] + (RWKV_HEADS, RWKV_HEAD_DIM))
    rf, kf, vf = heads(r.astype(f32)), heads(k.astype(f32)), heads(v.astype(f32))
    kk = heads(k.astype(f32) * k_k.astype(f32))
    kk = kk / jnp.maximum(jnp.sqrt(jnp.sum(kk * kk, axis=-1, keepdims=True)), 1e-12)
    a_h = heads(a)
    k_dir = kf[None] * (1.0 + (a_h - 1.0) * heads(k_a.astype(f32)))
    bb = kk[None] * a_h
    decay = heads(decay)
    tm = lambda t: jnp.moveaxis(t, 1, 0)
    r_t, v_t, na_t = tm(rf), tm(vf), tm(-kk)
    y_f = _delta_rule_scan(r_t, tm(decay[0]), tm(k_dir[0]), v_t, na_t, tm(bb[0]), reverse=False)
    y_b = _delta_rule_scan(r_t, tm(decay[1]), tm(k_dir[1]), v_t, na_t, tm(bb[1]), reverse=True)
    y = y_f + y_b
    yc = y - jnp.mean(y, axis=-1, keepdims=True)
    yn = (yc * lax.rsqrt(jnp.mean(yc * yc, axis=-1, keepdims=True) + GN_EPS)).reshape(b, l, d)
    yn = yn * lnx_g.astype(f32) + lnx_b.astype(f32)
    bonus = (jnp.sum(rf * (k_dir[0] + k_dir[1]) * r_k.astype(f32), axis=-1, keepdims=True) * vf).reshape(b, l, d)
    out = ((yn + bonus) * g.astype(f32)).astype(x.dtype)
    return jnp.einsum('bld,de->ble', out, w_o)


def swiglu(x, w_gate, w_up, w_down):
    h = jax.nn.silu(jnp.einsum('bld,df->blf', x, w_gate)) * jnp.einsum('bld,df->blf', x, w_up)
    return jnp.einsum('blf,fd->bld', h, w_down)


def moe_swiglu(x, w_router, b_router, w_gate, w_up, w_down):
    logits = jnp.einsum('bld,de->ble', x, w_router).astype(jnp.float32) + b_router.astype(jnp.float32)
    top_v, top_i = lax.top_k(logits, TOP_K)
    gates = jax.nn.softmax(top_v, axis=-1)
    combine = jnp.sum(jax.nn.one_hot(top_i, N_EXPERTS, dtype=jnp.float32) * gates[..., None], axis=-2).astype(x.dtype)
    out = jnp.zeros_like(x)
    for e in range(N_EXPERTS):
        out = out + combine[..., e:e + 1] * swiglu(x, w_gate[e], w_up[e], w_down[e])
    return out


def encoder_trunk(x, na_w_qkv, na_rpb, na_w_o, ffn_w_gate, ffn_w_up, ffn_w_down,
                  rwkv_mu, rwkv_w_rkv, rwkv_w0, rwkv_w1, rwkv_w2, rwkv_a0, rwkv_a1, rwkv_a2,
                  rwkv_g1, rwkv_g2, rwkv_k_k, rwkv_k_a, rwkv_r_k, rwkv_lnx_g, rwkv_lnx_b, rwkv_w_o,
                  moe_w_router, moe_b_router, moe_w_gate, moe_w_up, moe_w_down,
                  ln_mix_g, ln_mix_b, ln_ffn_g, ln_ffn_b):
    for i in range(DEPTH):
        j = i // 2
        if i % 2 == 0:
            h = neighbourhood_attention(x, na_w_qkv[j], na_rpb[j], na_w_o[j])
        else:
            h = rwkv7_bidirectional(x, rwkv_mu[j], rwkv_w_rkv[j], rwkv_w0[j], rwkv_w1[j], rwkv_w2[j],
                                    rwkv_a0[j], rwkv_a1[j], rwkv_a2[j], rwkv_g1[j], rwkv_g2[j],
                                    rwkv_k_k[j], rwkv_k_a[j], rwkv_r_k[j], rwkv_lnx_g[j], rwkv_lnx_b[j], rwkv_w_o[j])
        x = layer_norm(ALPHA * x + h, ln_mix_g[i], ln_mix_b[i])
        if i % 2 == 0:
            f = swiglu(x, ffn_w_gate[j], ffn_w_up[j], ffn_w_down[j])
        else:
            f = moe_swiglu(x, moe_w_router[j], moe_b_router[j], moe_w_gate[j], moe_w_up[j], moe_w_down[j])
        x = layer_norm(ALPHA * x + f, ln_ffn_g[i], ln_ffn_b[i])
    return x


def setup_inputs(seed: int = 0) -> dict:
    key = jax.random.key(seed)
    ks = iter(jax.random.split(key, 64))
    nrm = lambda shape, scale: jax.random.normal(next(ks), shape, jnp.float32) * scale
    D = D_MODEL
    s = D ** -0.5
    inp = {}
    inp['x_prompt'] = nrm((BATCH, SEQ, D), 1.0)
    inp['x_sample'] = nrm((DEC_BATCH, DEC_SEQ, D), 1.0)
    inp['na_w_qkv'] = jnp.concatenate([nrm((N_EVEN, D, D), s), nrm((N_EVEN, D, D), s), nrm((N_EVEN, D, D), s * BETA)], axis=-1)
    inp['na_rpb'] = nrm((N_EVEN, NA_HEADS, 2 * NA_ROWS - 1, 2 * NA_COLS - 1), 0.02)
    inp['na_w_o'] = nrm((N_EVEN, D, D), s * BETA)
    inp['ffn_w_gate'] = nrm((N_EVEN, D, D_FF), s * BETA)
    inp['ffn_w_up'] = nrm((N_EVEN, D, D_FF), s * BETA)
    inp['ffn_w_down'] = nrm((N_EVEN, D_FF, D), D_FF ** -0.5 * BETA)
    inp['rwkv_mu'] = jax.random.uniform(next(ks), (N_ODD, 6, D), jnp.float32)
    inp['rwkv_w_rkv'] = jnp.stack([nrm((N_ODD, D, D), s), nrm((N_ODD, D, D), s), nrm((N_ODD, D, D), s * BETA)], axis=1)
    inp['rwkv_w0'] = jnp.linspace(-6.0, -1.0, D, dtype=jnp.float32) + 0.5 + nrm((N_ODD, 2, D), 0.1)
    inp['rwkv_w1'] = nrm((N_ODD, 2, D, DECAY_LORA), s)
    inp['rwkv_w2'] = nrm((N_ODD, 2, DECAY_LORA, D), 0.1 * DECAY_LORA ** -0.5)
    inp['rwkv_a0'] = nrm((N_ODD, 2, D), 0.1)
    inp['rwkv_a1'] = nrm((N_ODD, 2, D, ICLR_LORA), s)
    inp['rwkv_a2'] = nrm((N_ODD, 2, ICLR_LORA, D), 0.1 * ICLR_LORA ** -0.5)
    inp['rwkv_g1'] = nrm((N_ODD, D, GATE_LORA), s)
    inp['rwkv_g2'] = nrm((N_ODD, GATE_LORA, D), GATE_LORA ** -0.5)
    inp['rwkv_k_k'] = 0.85 + nrm((N_ODD, D), 0.02)
    inp['rwkv_k_a'] = 1.0 + nrm((N_ODD, D), 0.02)
    inp['rwkv_r_k'] = nrm((N_ODD, RWKV_HEADS, RWKV_HEAD_DIM), 0.1)
    inp['rwkv_lnx_g'] = 1.0 + nrm((N_ODD, D), 0.02)
    inp['rwkv_lnx_b'] = nrm((N_ODD, D), 0.02)
    inp['rwkv_w_o'] = nrm((N_ODD, D, D), s * BETA)
    inp['moe_w_router'] = nrm((N_ODD, D, N_EXPERTS), s)
    inp['moe_b_router'] = nrm((N_ODD, N_EXPERTS), 0.01)
    inp['moe_w_gate'] = nrm((N_ODD, N_EXPERTS, D, D_EXPERT), s * BETA)
    inp['moe_w_up'] = nrm((N_ODD, N_EXPERTS, D, D_EXPERT), s * BETA)
    inp['moe_w_down'] = nrm((N_ODD, N_EXPERTS, D_EXPERT, D), D_EXPERT ** -0.5 * BETA)
    inp['ln_mix_g'] = 1.0 + nrm((DEPTH, D), 0.02)
    inp['ln_mix_b'] = nrm((DEPTH, D), 0.02)
    inp['ln_ffn_g'] = 1.0 + nrm((DEPTH, D), 0.02)
    inp['ln_ffn_b'] = nrm((DEPTH, D), 0.02)
    return inp


def reference(x_prompt, x_sample, na_w_qkv, na_rpb, na_w_o, ffn_w_gate, ffn_w_up, ffn_w_down,
              rwkv_mu, rwkv_w_rkv, rwkv_w0, rwkv_w1, rwkv_w2, rwkv_a0, rwkv_a1, rwkv_a2,
              rwkv_g1, rwkv_g2, rwkv_k_k, rwkv_k_a, rwkv_r_k, rwkv_lnx_g, rwkv_lnx_b, rwkv_w_o,
              moe_w_router, moe_b_router, moe_w_gate, moe_w_up, moe_w_down,
              ln_mix_g, ln_mix_b, ln_ffn_g, ln_ffn_b):
    weights = (na_w_qkv, na_rpb, na_w_o, ffn_w_gate, ffn_w_up, ffn_w_down,
               rwkv_mu, rwkv_w_rkv, rwkv_w0, rwkv_w1, rwkv_w2, rwkv_a0, rwkv_a1, rwkv_a2,
               rwkv_g1, rwkv_g2, rwkv_k_k, rwkv_k_a, rwkv_r_k, rwkv_lnx_g, rwkv_lnx_b, rwkv_w_o,
               moe_w_router, moe_b_router, moe_w_gate, moe_w_up, moe_w_down,
               ln_mix_g, ln_mix_b, ln_ffn_g, ln_ffn_b)
    y_prompt = encoder_trunk(x_prompt, *weights)
    y_sample = encoder_trunk(x_sample, *weights)
    return (y_prompt, y_sample)
```

```python
import functools
import math

import numpy as np
import jax
import jax.numpy as jnp
from jax import lax
from jax.experimental import pallas as pl
from jax.experimental.pallas import tpu as pltpu

D_MODEL = 1024
DEPTH = 2
GRID_W = 64
NA_HEAD_DIM = 32
NA_HEADS = D_MODEL // NA_HEAD_DIM
NA_ROWS = 8
NA_COLS = 16
RWKV_HEAD_DIM = 64
DECAY_LORA = 64
ICLR_LORA = 64
GATE_LORA = 160
GN_EPS = 64e-5
D_FF = 2816
N_EXPERTS = 8
D_EXPERT = 3584
LN_EPS = 1e-5
ALPHA = (2 * DEPTH) ** 0.25

LANES = 128
SCAN_CHUNK = 64
SCAN_BLOCK = 256
VMEM_LIMIT = 56 * 1024 * 1024

F32 = jnp.float32
BF16 = jnp.bfloat16


def _params(*sem):
    return pltpu.CompilerParams(dimension_semantics=sem, vmem_limit_bytes=VMEM_LIMIT)


def _dot(a, b):
    return jnp.dot(a, b, preferred_element_type=F32)


def _dot_nt(a, b):
    return lax.dot_general(a, b, (((1,), (1,)), ((), ())), preferred_element_type=F32)


def _dot_tn(a, b):
    return lax.dot_general(a, b, (((0,), (0,)), ((), ())), preferred_element_type=F32)


def _split(a):
    hi = a.astype(BF16)
    lo = (a - hi.astype(F32)).astype(BF16)
    return hi, lo


def _dot_hilo(a, b_exact):
    hi, lo = _split(a)
    return _dot(hi, b_exact) + _dot(lo, b_exact)


def _dot_hilo_lhs_exact(a_exact, b):
    hi, lo = _split(b)
    return _dot(a_exact, hi) + _dot(a_exact, lo)


def _layer_norm(z, g, b):
    mu = jnp.mean(z, axis=-1, keepdims=True)
    zc = z - mu
    var = jnp.mean(zc * zc, axis=-1, keepdims=True)
    return zc * lax.rsqrt(var + LN_EPS) * g + b


def _sigmoid(z):
    return 1.0 / (1.0 + jnp.exp(-z))


def _head_sum_matrix(head_dim):
    lane = np.arange(LANES)
    return jnp.asarray((lane[:, None] // head_dim) == (lane[None, :] // head_dim), BF16)


def _qkv_kernel(x_ref, w_ref, s_ref, o_ref):
    acc = _dot(x_ref[...].astype(BF16), w_ref[...])
    o_ref[...] = (acc * s_ref[...]).astype(o_ref.dtype)


def _qkv_proj(x, w, colscale, tm=512):
    t, d = x.shape
    n = w.shape[1]
    return pl.pallas_call(
        _qkv_kernel,
        grid=(t // tm,),
        in_specs=[pl.BlockSpec((tm, d), lambda i: (i, 0)),
                  pl.BlockSpec((d, n), lambda i: (0, 0)),
                  pl.BlockSpec((1, n), lambda i: (0, 0))],
        out_specs=pl.BlockSpec((tm, n), lambda i: (i, 0)),
        out_shape=jax.ShapeDtypeStruct((t, n), BF16),
        compiler_params=_params("parallel"),
        name="qkv_proj",
    )(x, w, colscale)


NA_HEADS_PER_GROUP = LANES // NA_HEAD_DIM
NA_GROUPS = D_MODEL // LANES
NA_WIN = NA_ROWS * GRID_W


def _na_kernel(q_ref, k_ref, v_ref, b_ref, o_ref):
    lane_head = lax.broadcasted_iota(jnp.int32, (GRID_W, LANES), 1) // NA_HEAD_DIM
    for g in range(NA_GROUPS):
        cols = slice(g * LANES, (g + 1) * LANES)
        qg = q_ref[:, cols]
        zero = jnp.zeros_like(qg)
        qm = jnp.concatenate([jnp.where(lane_head == h, qg, zero) for h in range(NA_HEADS_PER_GROUP)], axis=0)
        s = _dot_nt(qm, k_ref[:, cols])
        s = s + b_ref[g * NA_HEADS_PER_GROUP:(g + 1) * NA_HEADS_PER_GROUP].reshape(NA_HEADS_PER_GROUP * GRID_W, NA_WIN)
        m = jnp.max(s, axis=-1, keepdims=True)
        p = jnp.exp(s - m)
        l = jnp.sum(p, axis=-1, keepdims=True)
        o4 = _dot(p.astype(BF16), v_ref[:, cols]) / l
        og = jnp.zeros((GRID_W, LANES), F32)
        for h in range(NA_HEADS_PER_GROUP):
            og = og + jnp.where(lane_head == h, o4[h * GRID_W:(h + 1) * GRID_W], 0.0)
        o_ref[:, cols] = og.astype(o_ref.dtype)


def _na_attention(qkv, bias, batch, rows):
    t = qkv.shape[0]
    half = NA_ROWS // 2

    def win_start(r):
        return jnp.clip(r - half, 0, rows - NA_ROWS)

    def q_map(b, r):
        return (b * rows + r, 0)

    def k_map(b, r):
        return ((b * rows + win_start(r)) * GRID_W, D_MODEL)

    def v_map(b, r):
        return ((b * rows + win_start(r)) * GRID_W, 2 * D_MODEL)

    def b_map(b, r):
        return (r - win_start(r), 0, 0, 0)

    return pl.pallas_call(
        _na_kernel,
        grid=(batch, rows),
        in_specs=[pl.BlockSpec((GRID_W, D_MODEL), q_map),
                  pl.BlockSpec((pl.Element(NA_WIN), pl.Element(D_MODEL)), k_map),
                  pl.BlockSpec((pl.Element(NA_WIN), pl.Element(D_MODEL)), v_map),
                  pl.BlockSpec((None, NA_HEADS, GRID_W, NA_WIN), b_map)],
        out_specs=pl.BlockSpec((GRID_W, D_MODEL), q_map),
        out_shape=jax.ShapeDtypeStruct((t, D_MODEL), BF16),
        compiler_params=_params("parallel", "arbitrary"),
        name="na_attention",
    )(qkv, qkv, qkv, bias)


def _na_bias_table(rpb):
    c = np.arange(GRID_W)
    q_start = np.clip(c - NA_COLS // 2, 0, GRID_W - NA_COLS)
    kc = np.arange(GRID_W)
    valid = (kc[None, :] >= q_start[:, None]) & (kc[None, :] < q_start[:, None] + NA_COLS)
    dc = np.clip(kc[None, :] - c[:, None] + NA_COLS - 1, 0, 2 * NA_COLS - 2)
    off = np.arange(NA_ROWS)
    j = np.arange(NA_ROWS)
    dr = j[None, :] - off[:, None] + NA_ROWS - 1
    tab = rpb[:, dr[:, :, None, None], dc[None, None, :, :]]
    tab = jnp.where(valid[None, None, None], tab, -jnp.inf)
    tab = jnp.transpose(tab, (1, 0, 3, 2, 4))
    return tab.reshape(NA_ROWS, NA_HEADS, GRID_W, NA_WIN).astype(F32)


def _proj_ln_kernel(a_ref, w_ref, x_ref, g_ref, b_ref, o_ref):
    h = _dot(a_ref[...], w_ref[...])
    o_ref[...] = _layer_norm(ALPHA * x_ref[...] + h, g_ref[...], b_ref[...])


def _proj_res_ln(a, w, x, g, b, tm=512):
    t, d = x.shape
    row = pl.BlockSpec((tm, d), lambda i: (i, 0))
    vec = pl.BlockSpec((1, d), lambda i: (0, 0))
    return pl.pallas_call(
        _proj_ln_kernel,
        grid=(t // tm,),
        in_specs=[row, pl.BlockSpec((d, d), lambda i: (0, 0)), row, vec, vec],
        out_specs=row,
        out_shape=jax.ShapeDtypeStruct((t, d), F32),
        compiler_params=_params("parallel"),
        name="proj_res_ln",
    )(a, w, x, g, b)


def _ffn_kernel(x_ref, wg_ref, wu_ref, wd_ref, g_ref, b_ref, o_ref, xb_ref, acc_ref):
    f = pl.program_id(1)

    @pl.when(f == 0)
    def _():
        xb_ref[...] = x_ref[...].astype(BF16)
        acc_ref[...] = jnp.zeros_like(acc_ref)

    xb = xb_ref[...]
    gate = _dot(xb, wg_ref[...])
    up = _dot(xb, wu_ref[...])
    h = (gate * _sigmoid(gate) * up).astype(BF16)
    acc_ref[...] += _dot(h, wd_ref[...])

    @pl.when(f == pl.num_programs(1) - 1)
    def _():
        o_ref[...] = _layer_norm(ALPHA * x_ref[...] + acc_ref[...], g_ref[...], b_ref[...])


def _ffn_res_ln(x, wg, wu, wd, g, b, tm=512, tf=1408):
    t, d = x.shape
    ff = wg.shape[1]
    row = pl.BlockSpec((tm, d), lambda i, f: (i, 0))
    vec = pl.BlockSpec((1, d), lambda i, f: (0, 0))
    return pl.pallas_call(
        _ffn_kernel,
        grid=(t // tm, ff // tf),
        in_specs=[row,
                  pl.BlockSpec((d, tf), lambda i, f: (0, f)),
                  pl.BlockSpec((d, tf), lambda i, f: (0, f)),
                  pl.BlockSpec((tf, d), lambda i, f: (f, 0)),
                  vec, vec],
        out_specs=row,
        out_shape=jax.ShapeDtypeStruct((t, d), F32),
        scratch_shapes=[pltpu.VMEM((tm, d), BF16), pltpu.VMEM((tm, d), F32)],
        compiler_params=_params("parallel", "arbitrary"),
        name="ffn_res_ln",
    )(x, wg, wu, wd, g, b)


HALO = 8


def _rwkv_proj_kernel(seq_len, x_ref, xp_ref, xn_ref, mu_ref, wrkv_ref, w1_ref, w2_ref, w0_ref,
                      a1_ref, a2_ref, a0_ref, g1_ref, g2_ref,
                      r_ref, k_ref, v_ref, g_ref, lw0_ref, lw1_ref, ia0_ref, ia1_ref):
    tm = x_ref.shape[0]
    i = pl.program_id(0)
    x = x_ref[...]
    row = lax.broadcasted_iota(jnp.int32, x.shape, 0)
    first_in_seq = (i * tm) % seq_len == 0
    last_in_seq = ((i + 1) * tm) % seq_len == 0
    prev_row = jnp.where(first_in_seq, 0.0, xp_ref[HALO - 1:HALO, :])
    next_row = jnp.where(last_in_seq, 0.0, xn_ref[0:1, :])
    x_prev = jnp.where(row == 0, prev_row, pltpu.roll(x, 1, axis=0))
    x_next = jnp.where(row == tm - 1, next_row, pltpu.roll(x, tm - 1, axis=0))
    xx = 0.5 * (x_prev + x_next) - x

    def mix(j):
        return (x + xx * mu_ref[j:j + 1, :]).astype(BF16)

    r_ref[...] = _dot(mix(0), wrkv_ref[0]).astype(r_ref.dtype)
    k_ref[...] = _dot(mix(2), wrkv_ref[1]).astype(k_ref.dtype)
    v_ref[...] = _dot(mix(3), wrkv_ref[2]).astype(v_ref.dtype)

    lane = lax.broadcasted_iota(jnp.int32, (tm, LANES), 1)
    dir0 = lane < DECAY_LORA

    lo = jnp.tanh(_dot(mix(1), w1_ref[...]))
    for z, out in enumerate((lw0_ref, lw1_ref)):
        lz = jnp.where(dir0 if z == 0 else ~dir0, lo, 0.0).astype(BF16)
        wl = w0_ref[z:z + 1, :] + _dot(lz, w2_ref[...])
        out[...] = (-math.exp(-0.5)) * _sigmoid(wl)

    al = _dot(mix(4), a1_ref[...])
    for z, out in enumerate((ia0_ref, ia1_ref)):
        az = jnp.where(dir0 if z == 0 else ~dir0, al, 0.0).astype(BF16)
        out[...] = _sigmoid(a0_ref[z:z + 1, :] + _dot(az, a2_ref[...])).astype(out.dtype)

    gl = _sigmoid(_dot(mix(5), g1_ref[...])).astype(BF16)
    g_ref[...] = _dot(gl, g2_ref[...]).astype(g_ref.dtype)


def _rwkv_proj(x, seq_len, mu, wrkv, w1, w2, w0, a1, a2, a0, g1, g2, tm=256):
    t, d = x.shape
    nb = t // HALO
    step = tm // HALO
    row = pl.BlockSpec((tm, d), lambda i: (i, 0))
    full = lambda shape: pl.BlockSpec(shape, lambda i: (0,) * len(shape))
    outs = [jax.ShapeDtypeStruct((t, d), BF16)] * 4 + [jax.ShapeDtypeStruct((t, d), F32)] * 2 + \
           [jax.ShapeDtypeStruct((t, d), BF16)] * 2
    return pl.pallas_call(
        functools.partial(_rwkv_proj_kernel, seq_len),
        grid=(t // tm,),
        in_specs=[row,
                  pl.BlockSpec((HALO, d), lambda i: (jnp.maximum(i * step - 1, 0), 0)),
                  pl.BlockSpec((HALO, d), lambda i: (jnp.minimum((i + 1) * step, nb - 1), 0)),
                  full(mu.shape), full(wrkv.shape), full(w1.shape), full(w2.shape), full(w0.shape),
                  full(a1.shape), full(a2.shape), full(a0.shape), full(g1.shape), full(g2.shape)],
        out_specs=[row] * 8,
        out_shape=outs,
        compiler_params=_params("parallel"),
        name="rwkv_proj",
    )(x, x, x, mu, wrkv, w1, w2, w0, a1, a2, a0, g1, g2)


def _pair_blockdiag(y, lane_a):
    zero = jnp.zeros_like(y)
    return jnp.concatenate([jnp.where(lane_a, y, zero), jnp.where(lane_a, zero, y)], axis=0)


def _pair_matmul(x, y, lane_a):
    return _dot(x.astype(BF16), _pair_blockdiag(y.astype(BF16), lane_a))


def _scan_chunk(st, r, k, v, lw, ia, kscale, ka, hsum, reverse):
    c = SCAN_CHUNK
    t_idx = lax.broadcasted_iota(jnp.int32, (c, LANES), 0)
    s_idx = lax.broadcasted_iota(jnp.int32, (c, LANES), 1) % c
    lane_a = lax.broadcasted_iota(jnp.int32, (c, LANES), 1) < RWKV_HEAD_DIM
    before = (s_idx > t_idx) if reverse else (s_idx < t_idx)
    upto = before | (s_idx == t_idx)
    tri = lax.broadcasted_iota(jnp.int32, (c, c), 0), lax.broadcasted_iota(jnp.int32, (c, c), 1)
    cum = ((tri[1] >= tri[0]) if reverse else (tri[1] <= tri[0])).astype(BF16)

    kf = k.astype(F32)
    iaf = ia.astype(F32)
    kkr = kf * kscale
    norm2 = _dot_hilo(kkr * kkr, hsum)
    kk = kkr / jnp.maximum(jnp.sqrt(norm2), 1e-12)
    kd = kf * (1.0 + (iaf - 1.0) * ka)
    bvec = kk * iaf

    lwc = _dot_hilo_lhs_exact(cum, lw)
    last = lwc[0:1, :] if reverse else lwc[c - 1:c, :]
    w_in = jnp.exp(lwc)
    w_ex = jnp.exp(lwc - lw)
    w_inv = jnp.exp(-lwc)
    w_out = jnp.exp(last - lwc)

    at = (-kk) * w_ex
    rt = r.astype(F32) * w_in
    bt = bvec * w_inv
    kt = kd * w_inv
    bh = bvec * w_out
    kh = kd * w_out

    lhs = jnp.concatenate([at, rt], axis=0).astype(BF16)
    rhs = jnp.concatenate([_pair_blockdiag(bt.astype(BF16), lane_a),
                           _pair_blockdiag(kt.astype(BF16), lane_a)], axis=0)
    gram = _dot_nt(lhs, rhs)
    a_ab = jnp.where(before, gram[0:c, 0:LANES], 0.0)
    a_ak = jnp.where(before, gram[0:c, LANES:2 * LANES], 0.0)
    a_rb = jnp.where(upto, gram[c:2 * c, 0:LANES], 0.0)
    a_rk = jnp.where(upto, gram[c:2 * c, LANES:2 * LANES], 0.0)

    eye = (s_idx == t_idx).astype(F32)
    tinv = eye + a_ab
    pw = a_ab
    for _ in range(int(math.log2(c)) - 1):
        pw = _pair_matmul(pw, pw, lane_a)
        tinv = tinv + _pair_matmul(tinv, pw, lane_a)

    vf = v.astype(F32)
    p_mat = _pair_matmul(tinv, at, lane_a)
    q_mat = _pair_matmul(tinv, _pair_matmul(a_ak, vf, lane_a), lane_a)
    y_loc = _pair_matmul(a_rk, vf, lane_a)

    x1 = _dot(jnp.concatenate([p_mat, rt], axis=0).astype(BF16), st.astype(BF16))
    u = x1[0:c] + q_mat
    y = x1[c:2 * c] + _pair_matmul(a_rb, u, lane_a) + y_loc

    bd = (lax.broadcasted_iota(jnp.int32, (LANES, LANES), 0) // RWKV_HEAD_DIM ==
          lax.broadcasted_iota(jnp.int32, (LANES, LANES), 1) // RWKV_HEAD_DIM)
    upd = _dot_tn(bh.astype(BF16), u.astype(BF16)) + _dot_tn(kh.astype(BF16), v)
    lw_hi, lw_lo = _split(lw)
    ones = jnp.ones((c, LANES), BF16)
    decay = jnp.exp(_dot_tn(lw_hi, ones) + _dot_tn(lw_lo, ones))
    st_new = st * decay + jnp.where(bd, upd, 0.0)
    return st_new, y


def _scan_kernel(rf_ref, kf_ref, vf_ref, lwf_ref, iaf_ref, rb_ref, kb_ref, vb_ref, lwb_ref, iab_ref,
                 ks_ref, ka_ref, hs_ref, yf_ref, yb_ref, st_ref):
    @pl.when(pl.program_id(2) == 0)
    def _():
        st_ref[...] = jnp.zeros_like(st_ref)

    n_chunks = SCAN_BLOCK // SCAN_CHUNK
    kscale = ks_ref[...]
    ka = ka_ref[...]
    hsum = hs_ref[...]
    st_f = st_ref[0]
    st_b = st_ref[1]
    for ci in range(n_chunks):
        fw = slice(ci * SCAN_CHUNK, (ci + 1) * SCAN_CHUNK)
        st_f, y = _scan_chunk(st_f, rf_ref[fw, :], kf_ref[fw, :], vf_ref[fw, :], lwf_ref[fw, :], iaf_ref[fw, :],
                              kscale, ka, hsum, False)
        yf_ref[fw, :] = y
        cb = n_chunks - 1 - ci
        bw = slice(cb * SCAN_CHUNK, (cb + 1) * SCAN_CHUNK)
        st_b, y = _scan_chunk(st_b, rb_ref[bw, :], kb_ref[bw, :], vb_ref[bw, :], lwb_ref[bw, :], iab_ref[bw, :],
                              kscale, ka, hsum, True)
        yb_ref[bw, :] = y
    st_ref[0] = st_f
    st_ref[1] = st_b


def _rwkv_scan(r, k, v, lw0, lw1, ia0, ia1, kscale, ka, batch, seq_len):
    t, d = r.shape
    ns = seq_len // SCAN_BLOCK
    fwd = pl.BlockSpec((SCAN_BLOCK, LANES), lambda b, h, s: (b * ns + s, h))
    bwd = pl.BlockSpec((SCAN_BLOCK, LANES), lambda b, h, s: (b * ns + ns - 1 - s, h))
    vec = pl.BlockSpec((1, LANES), lambda b, h, s: (0, h))
    hsum = _head_sum_matrix(RWKV_HEAD_DIM)
    return pl.pallas_call(
        _scan_kernel,
        grid=(batch, d // LANES, ns),
        in_specs=[fwd] * 5 + [bwd] * 5 + [vec, vec, pl.BlockSpec((LANES, LANES), lambda b, h, s: (0, 0))],
        out_specs=[fwd, bwd],
        out_shape=[jax.ShapeDtypeStruct((t, d), F32)] * 2,
        scratch_shapes=[pltpu.VMEM((2, LANES, LANES), F32)],
        compiler_params=_params("parallel", "parallel", "arbitrary"),
        name="rwkv_scan",
    )(r, k, v, lw0, ia0, r, k, v, lw1, ia1, kscale, ka, hsum)


def _rwkv_out_kernel(yf_ref, yb_ref, r_ref, k_ref, v_ref, g_ref, ia0_ref, ia1_ref, x_ref,
                     lg_ref, lb_ref, rk_ref, ka_ref, hs_ref, wo_ref, ng_ref, nb_ref, o_ref, z_ref):
    hsum = hs_ref[...]
    inv = 1.0 / RWKV_HEAD_DIM
    for gi in range(D_MODEL // LANES):
        cols = slice(gi * LANES, (gi + 1) * LANES)
        y = yf_ref[:, cols] + yb_ref[:, cols]
        yc = y - _dot_hilo(y, hsum) * inv
        var = _dot_hilo(yc * yc, hsum) * inv
        yn = yc * lax.rsqrt(var + GN_EPS) * lg_ref[:, cols] + lb_ref[:, cols]
        rr = r_ref[:, cols].astype(F32)
        kk = k_ref[:, cols].astype(F32)
        ia = ia0_ref[:, cols].astype(F32) + ia1_ref[:, cols].astype(F32)
        ksum = kk * (2.0 + (ia - 2.0) * ka_ref[:, cols])
        bonus = _dot_hilo(rr * ksum * rk_ref[:, cols], hsum) * v_ref[:, cols].astype(F32)
        z_ref[:, cols] = ((yn + bonus) * g_ref[:, cols].astype(F32)).astype(BF16)
    h = _dot(z_ref[...], wo_ref[...])
    o_ref[...] = _layer_norm(ALPHA * x_ref[...] + h, ng_ref[...], nb_ref[...])


def _rwkv_out(yf, yb, r, k, v, g, ia0, ia1, x, lnx_g, lnx_b, r_k, k_a, wo, ng, nb, tm=256):
    t, d = x.shape
    row = pl.BlockSpec((tm, d), lambda i: (i, 0))
    vec = pl.BlockSpec((1, d), lambda i: (0, 0))
    return pl.pallas_call(
        _rwkv_out_kernel,
        grid=(t // tm,),
        in_specs=[row] * 9 + [vec] * 4 + [pl.BlockSpec((LANES, LANES), lambda i: (0, 0)),
                                          pl.BlockSpec((d, d), lambda i: (0, 0)), vec, vec],
        out_specs=row,
        out_shape=jax.ShapeDtypeStruct((t, d), F32),
        scratch_shapes=[pltpu.VMEM((tm, d), BF16)],
        compiler_params=_params("parallel"),
        name="rwkv_out",
    )(yf, yb, r, k, v, g, ia0, ia1, x, lnx_g, lnx_b, r_k, k_a, _head_sum_matrix(RWKV_HEAD_DIM), wo, ng, nb)


def _router_kernel(x_ref, w_ref, b_ref, o_ref):
    xh, xl = _split(x_ref[...])
    wh = w_ref[0]
    wl = w_ref[1]
    logits = _dot(xh, wh) + (_dot(xh, wl) + _dot(xl, wh)) + b_ref[...]
    lane = lax.broadcasted_iota(jnp.int32, logits.shape, 1)
    logits = jnp.where(lane < N_EXPERTS, logits, -jnp.inf)
    m1 = jnp.max(logits, axis=-1, keepdims=True)
    i1 = jnp.min(jnp.where(logits == m1, lane, LANES), axis=-1, keepdims=True)
    rest = jnp.where(lane == i1, -jnp.inf, logits)
    m2 = jnp.max(rest, axis=-1, keepdims=True)
    i2 = jnp.min(jnp.where(rest == m2, lane, LANES), axis=-1, keepdims=True)
    e2 = jnp.exp(m2 - m1)
    g1 = 1.0 / (1.0 + e2)
    g2 = e2 / (1.0 + e2)
    o_ref[...] = jnp.where(lane == i1, g1, 0.0) + jnp.where(lane == i2, g2, 0.0)


def _router(x, w_hilo, bias, tm=512):
    t, d = x.shape
    return pl.pallas_call(
        _router_kernel,
        grid=(t // tm,),
        in_specs=[pl.BlockSpec((tm, d), lambda i: (i, 0)),
                  pl.BlockSpec((2, d, LANES), lambda i: (0, 0, 0)),
                  pl.BlockSpec((1, LANES), lambda i: (0, 0))],
        out_specs=pl.BlockSpec((tm, LANES), lambda i: (i, 0)),
        out_shape=jax.ShapeDtypeStruct((t, LANES), F32),
        compiler_params=_params("parallel"),
        name="moe_router",
    )(x, w_hilo, bias)


def _moe_kernel(x_ref, c_ref, wg_ref, wu_ref, wd_ref, g_ref, b_ref, o_ref, xb_ref, acc_ref, tot_ref):
    e = pl.program_id(1)
    f = pl.program_id(2)
    nf = pl.num_programs(2)

    @pl.when((e == 0) & (f == 0))
    def _():
        xb_ref[...] = x_ref[...].astype(BF16)
        tot_ref[...] = jnp.zeros_like(tot_ref)

    @pl.when(f == 0)
    def _():
        acc_ref[...] = jnp.zeros_like(acc_ref)

    xb = xb_ref[...]
    gate = _dot(xb, wg_ref[...])
    up = _dot(xb, wu_ref[...])
    h = (gate * _sigmoid(gate) * up).astype(BF16)
    acc_ref[...] += _dot(h, wd_ref[...])

    @pl.when(f == nf - 1)
    def _():
        comb = c_ref[...]
        lane = lax.broadcasted_iota(jnp.int32, comb.shape, 1)
        ce = jnp.sum(jnp.where(lane == e, comb, 0.0), axis=-1, keepdims=True)
        tot_ref[...] += ce * acc_ref[...]

    @pl.when((e == pl.num_programs(1) - 1) & (f == nf - 1))
    def _():
        o_ref[...] = _layer_norm(ALPHA * x_ref[...] + tot_ref[...], g_ref[...], b_ref[...])


def _moe_res_ln(x, comb, wg, wu, wd, g, b, tm=1024, tf=512):
    t, d = x.shape
    ne, _, fe = wg.shape
    row = pl.BlockSpec((tm, d), lambda i, e, f: (i, 0))
    vec = pl.BlockSpec((1, d), lambda i, e, f: (0, 0))
    return pl.pallas_call(
        _moe_kernel,
        grid=(t // tm, ne, fe // tf),
        in_specs=[row,
                  pl.BlockSpec((tm, LANES), lambda i, e, f: (i, 0)),
                  pl.BlockSpec((None, d, tf), lambda i, e, f: (e, 0, f)),
                  pl.BlockSpec((None, d, tf), lambda i, e, f: (e, 0, f)),
                  pl.BlockSpec((None, tf, d), lambda i, e, f: (e, f, 0)),
                  vec, vec],
        out_specs=row,
        out_shape=jax.ShapeDtypeStruct((t, d), F32),
        scratch_shapes=[pltpu.VMEM((tm, d), BF16), pltpu.VMEM((tm, d), F32), pltpu.VMEM((tm, d), F32)],
        compiler_params=_params("parallel", "arbitrary", "arbitrary"),
        name="moe_res_ln",
    )(x, comb, wg, wu, wd, g, b)


def _prepare_weights(na_w_qkv, na_rpb, na_w_o, ffn_w_gate, ffn_w_up, ffn_w_down,
                     rwkv_mu, rwkv_w_rkv, rwkv_w0, rwkv_w1, rwkv_w2, rwkv_a0, rwkv_a1, rwkv_a2,
                     rwkv_g1, rwkv_g2, rwkv_k_k, rwkv_k_a, rwkv_r_k, rwkv_lnx_g, rwkv_lnx_b, rwkv_w_o,
                     moe_w_router, moe_b_router, moe_w_gate, moe_w_up, moe_w_down,
                     ln_mix_g, ln_mix_b, ln_ffn_g, ln_ffn_b):
    d = D_MODEL
    vec = lambda a: a.reshape(1, d).astype(F32)
    gl_pad = 2 * LANES - GATE_LORA
    router_w = jnp.pad(moe_w_router[0], ((0, 0), (0, LANES - N_EXPERTS)))
    router_hi = router_w.astype(BF16)
    router_lo = (router_w - router_hi.astype(F32)).astype(BF16)
    return dict(
        qkv_w=na_w_qkv[0].astype(BF16),
        qkv_scale=jnp.concatenate([jnp.full((1, d), NA_HEAD_DIM ** -0.5, F32), jnp.ones((1, 2 * d), F32)], axis=1),
        na_bias=_na_bias_table(na_rpb[0]),
        na_wo=na_w_o[0].astype(BF16),
        ffn_wg=ffn_w_gate[0].astype(BF16), ffn_wu=ffn_w_up[0].astype(BF16), ffn_wd=ffn_w_down[0].astype(BF16),
        mu=rwkv_mu[0].astype(F32),
        wrkv=rwkv_w_rkv[0].astype(BF16),
        w1=jnp.concatenate([rwkv_w1[0, 0], rwkv_w1[0, 1]], axis=1).astype(BF16),
        w2=jnp.concatenate([rwkv_w2[0, 0], rwkv_w2[0, 1]], axis=0).astype(BF16),
        w0=rwkv_w0[0].astype(F32),
        a1=jnp.concatenate([rwkv_a1[0, 0], rwkv_a1[0, 1]], axis=1).astype(BF16),
        a2=jnp.concatenate([rwkv_a2[0, 0], rwkv_a2[0, 1]], axis=0).astype(BF16),
        a0=rwkv_a0[0].astype(F32),
        g1=jnp.pad(rwkv_g1[0], ((0, 0), (0, gl_pad))).astype(BF16),
        g2=jnp.pad(rwkv_g2[0], ((0, gl_pad), (0, 0))).astype(BF16),
        k_k=vec(rwkv_k_k[0]), k_a=vec(rwkv_k_a[0]), r_k=vec(rwkv_r_k[0]),
        lnx_g=vec(rwkv_lnx_g[0]), lnx_b=vec(rwkv_lnx_b[0]),
        rwkv_wo=rwkv_w_o[0].astype(BF16),
        router_w=jnp.stack([router_hi, router_lo]),
        router_b=jnp.pad(moe_b_router[0], (0, LANES - N_EXPERTS)).reshape(1, LANES).astype(F32),
        moe_wg=moe_w_gate[0].astype(BF16), moe_wu=moe_w_up[0].astype(BF16), moe_wd=moe_w_down[0].astype(BF16),
        ln_mix_g=ln_mix_g.astype(F32), ln_mix_b=ln_mix_b.astype(F32),
        ln_ffn_g=ln_ffn_g.astype(F32), ln_ffn_b=ln_ffn_b.astype(F32),
    )


def _trunk(x3, w):
    batch, seq_len, d = x3.shape
    rows = seq_len // GRID_W
    x = x3.reshape(batch * seq_len, d)
    ln = lambda a, i: a[i].reshape(1, d)

    qkv = _qkv_proj(x, w["qkv_w"], w["qkv_scale"])
    att = _na_attention(qkv, w["na_bias"], batch, rows)
    x = _proj_res_ln(att, w["na_wo"], x, ln(w["ln_mix_g"], 0), ln(w["ln_mix_b"], 0))
    x = _ffn_res_ln(x, w["ffn_wg"], w["ffn_wu"], w["ffn_wd"], ln(w["ln_ffn_g"], 0), ln(w["ln_ffn_b"], 0))

    r, k, v, g, lw0, lw1, ia0, ia1 = _rwkv_proj(x, seq_len, w["mu"], w["wrkv"], w["w1"], w["w2"], w["w0"],
                                                w["a1"], w["a2"], w["a0"], w["g1"], w["g2"])
    yf, yb = _rwkv_scan(r, k, v, lw0, lw1, ia0, ia1, w["k_k"], w["k_a"], batch, seq_len)
    x = _rwkv_out(yf, yb, r, k, v, g, ia0, ia1, x, w["lnx_g"], w["lnx_b"], w["r_k"], w["k_a"], w["rwkv_wo"],
                  ln(w["ln_mix_g"], 1), ln(w["ln_mix_b"], 1))
    comb = _router(x, w["router_w"], w["router_b"])
    x = _moe_res_ln(x, comb, w["moe_wg"], w["moe_wu"], w["moe_wd"], ln(w["ln_ffn_g"], 1), ln(w["ln_ffn_b"], 1))
    return x.reshape(batch, seq_len, d)


def kernel(x_prompt, x_sample, na_w_qkv, na_rpb, na_w_o, ffn_w_gate, ffn_w_up, ffn_w_down, rwkv_mu, rwkv_w_rkv, rwkv_w0, rwkv_w1, rwkv_w2, rwkv_a0, rwkv_a1, rwkv_a2, rwkv_g1, rwkv_g2, rwkv_k_k, rwkv_k_a, rwkv_r_k, rwkv_lnx_g, rwkv_lnx_b, rwkv_w_o, moe_w_router, moe_b_router, moe_w_gate, moe_w_up, moe_w_down, ln_mix_g, ln_mix_b, ln_ffn_g, ln_ffn_b):
    w = _prepare_weights(na_w_qkv, na_rpb, na_w_o, ffn_w_gate, ffn_w_up, ffn_w_down,
                         rwkv_mu, rwkv_w_rkv, rwkv_w0, rwkv_w1, rwkv_w2, rwkv_a0, rwkv_a1, rwkv_a2,
                         rwkv_g1, rwkv_g2, rwkv_k_k, rwkv_k_a, rwkv_r_k, rwkv_lnx_g, rwkv_lnx_b, rwkv_w_o,
                         moe_w_router, moe_b_router, moe_w_gate, moe_w_up, moe_w_down,
                         ln_mix_g, ln_mix_b, ln_ffn_g, ln_ffn_b)
    return (_trunk(x_prompt, w), _trunk(x_sample, w))
```

```python
import functools
import math

import numpy as np
import jax
import jax.numpy as jnp
from jax import lax
from jax.experimental import pallas as pl
from jax.experimental.pallas import tpu as pltpu

D_MODEL = 1024
DEPTH = 2
GRID_W = 64
NA_HEAD_DIM = 32
NA_HEADS = D_MODEL // NA_HEAD_DIM
NA_ROWS = 8
NA_COLS = 16
RWKV_HEAD_DIM = 64
DECAY_LORA = 64
ICLR_LORA = 64
GATE_LORA = 160
GN_EPS = 64e-5
D_FF = 2816
N_EXPERTS = 8
D_EXPERT = 3584
LN_EPS = 1e-5
ALPHA = (2 * DEPTH) ** 0.25

LANES = 128
SCAN_CHUNK = 64
SCAN_BLOCK = 256
VMEM_LIMIT = 56 * 1024 * 1024

F32 = jnp.float32
BF16 = jnp.bfloat16


def _params(*sem):
    return pltpu.CompilerParams(dimension_semantics=sem, vmem_limit_bytes=VMEM_LIMIT)


def _dot(a, b):
    return jnp.dot(a, b, preferred_element_type=F32)


def _dot_nt(a, b):
    return lax.dot_general(a, b, (((1,), (1,)), ((), ())), preferred_element_type=F32)


def _dot_tn(a, b):
    return lax.dot_general(a, b, (((0,), (0,)), ((), ())), preferred_element_type=F32)


def _split(a):
    hi = a.astype(BF16)
    lo = (a - hi.astype(F32)).astype(BF16)
    return hi, lo


def _dot_hilo(a, b_exact):
    hi, lo = _split(a)
    return _dot(hi, b_exact) + _dot(lo, b_exact)


def _dot_hilo_lhs_exact(a_exact, b):
    hi, lo = _split(b)
    return _dot(a_exact, hi) + _dot(a_exact, lo)


def _layer_norm(z, g, b):
    mu = jnp.mean(z, axis=-1, keepdims=True)
    zc = z - mu
    var = jnp.mean(zc * zc, axis=-1, keepdims=True)
    return zc * lax.rsqrt(var + LN_EPS) * g + b


def _sigmoid(z):
    return 1.0 / (1.0 + jnp.exp(-z))


def _head_sum_matrix(head_dim):
    lane = np.arange(LANES)
    return jnp.asarray((lane[:, None] // head_dim) == (lane[None, :] // head_dim), BF16)


def _qkv_kernel(x_ref, w_ref, s_ref, o_ref):
    acc = _dot(x_ref[...].astype(BF16), w_ref[...])
    o_ref[...] = (acc * s_ref[...]).astype(o_ref.dtype)


def _qkv_proj(x, w, colscale, tm=512):
    t, d = x.shape
    n = w.shape[1]
    return pl.pallas_call(
        _qkv_kernel,
        grid=(t // tm,),
        in_specs=[pl.BlockSpec((tm, d), lambda i: (i, 0)),
                  pl.BlockSpec((d, n), lambda i: (0, 0)),
                  pl.BlockSpec((1, n), lambda i: (0, 0))],
        out_specs=pl.BlockSpec((tm, n), lambda i: (i, 0)),
        out_shape=jax.ShapeDtypeStruct((t, n), BF16),
        compiler_params=_params("parallel"),
        name="qkv_proj",
    )(x, w, colscale)


NA_HEADS_PER_GROUP = LANES // NA_HEAD_DIM
NA_GROUPS = D_MODEL // LANES
NA_WIN = NA_ROWS * GRID_W


NA_GROUP_ROWS = NA_HEADS_PER_GROUP * GRID_W
NA_ROW_PAIRS = NA_ROWS // 2
NA_BIAS_PAIRS = 2 * NA_ROWS - 2


def _na_kernel(rows, q_ref, k_ref, v_ref, b_ref, o_ref):
    r = pl.program_id(1)
    off = r - jnp.clip(r - NA_ROWS // 2, 0, rows - NA_ROWS)
    lane_head = lax.broadcasted_iota(jnp.int32, (GRID_W, LANES), 1) // NA_HEAD_DIM
    groups = [slice(g * LANES, (g + 1) * LANES) for g in range(NA_GROUPS)]

    scores = []
    for cols in groups:
        qg = q_ref[:, cols]
        zero = jnp.zeros_like(qg)
        qm = jnp.concatenate([jnp.where(lane_head == h, qg, zero) for h in range(NA_HEADS_PER_GROUP)], axis=0)
        scores.append(_dot_nt(qm, k_ref[:, cols]))

    probs, denoms = [], []
    for g, s in enumerate(scores):
        hq = pl.ds(g * NA_GROUP_ROWS, NA_GROUP_ROWS)
        s = jnp.concatenate([s[:, m * LANES:(m + 1) * LANES] + b_ref[2 * m - off + NA_ROWS - 1, hq, :]
                             for m in range(NA_ROW_PAIRS)], axis=1)
        p = jnp.exp(s - jnp.max(s, axis=-1, keepdims=True))
        denoms.append(jnp.sum(p, axis=-1, keepdims=True))
        probs.append(p.astype(BF16))

    for g, cols in enumerate(groups):
        o4 = _dot(probs[g], v_ref[:, cols]) / denoms[g]
        og = jnp.zeros((GRID_W, LANES), F32)
        for h in range(NA_HEADS_PER_GROUP):
            og = og + jnp.where(lane_head == h, o4[h * GRID_W:(h + 1) * GRID_W], 0.0)
        o_ref[:, cols] = og.astype(o_ref.dtype)


def _na_attention(qkv, bias, batch, rows):
    t = qkv.shape[0]
    half = NA_ROWS // 2

    def win_start(r):
        return jnp.clip(r - half, 0, rows - NA_ROWS)

    def q_map(b, r):
        return (b * rows + r, 0)

    def k_map(b, r):
        return ((b * rows + win_start(r)) * GRID_W, D_MODEL)

    def v_map(b, r):
        return ((b * rows + win_start(r)) * GRID_W, 2 * D_MODEL)

    return pl.pallas_call(
        functools.partial(_na_kernel, rows),
        grid=(batch, rows),
        in_specs=[pl.BlockSpec((GRID_W, D_MODEL), q_map),
                  pl.BlockSpec((pl.Element(NA_WIN), pl.Element(D_MODEL)), k_map),
                  pl.BlockSpec((pl.Element(NA_WIN), pl.Element(D_MODEL)), v_map),
                  pl.BlockSpec(bias.shape, lambda b, r: (0, 0, 0), pipeline_mode=pl.Buffered(1))],
        out_specs=pl.BlockSpec((GRID_W, D_MODEL), q_map),
        out_shape=jax.ShapeDtypeStruct((t, D_MODEL), BF16),
        compiler_params=_params("parallel", "arbitrary"),
        name="na_attention",
    )(qkv, qkv, qkv, bias)


def _na_bias_kernel(rpb_ref, oh_ref, mask_ref, o_ref):
    x = rpb_ref[...]
    hi = x.astype(BF16)
    r1 = x - hi.astype(F32)
    mid = r1.astype(BF16)
    lo = (r1 - mid.astype(F32)).astype(BF16)
    oh = oh_ref[...]
    o_ref[...] = _dot(hi, oh) + _dot(mid, oh) + _dot(lo, oh) + mask_ref[...]


def _na_bias_table(rpb, tn=1024):
    n_dr, n_dc = 2 * NA_ROWS - 1, 2 * NA_COLS - 1
    c = np.arange(GRID_W)
    q_start = np.clip(c - NA_COLS // 2, 0, GRID_W - NA_COLS)
    kc = np.arange(GRID_W)
    valid = (kc[None, :] >= q_start[:, None]) & (kc[None, :] < q_start[:, None] + NA_COLS)
    dc = kc[None, :] - c[:, None] + NA_COLS - 1
    onehot = (np.arange(LANES)[:, None, None] == dc[None]) & valid[None]
    onehot = jnp.asarray(onehot.reshape(LANES, GRID_W * GRID_W), BF16)
    mask = jnp.asarray(np.where(valid, 0.0, -np.inf).reshape(1, GRID_W * GRID_W), F32)
    rpb2 = jnp.pad(rpb.reshape(NA_HEADS * n_dr, n_dc).astype(F32), ((0, 0), (0, LANES - n_dc)))
    nrow, ncol = rpb2.shape[0], GRID_W * GRID_W
    flat = pl.pallas_call(
        _na_bias_kernel,
        grid=(ncol // tn,),
        in_specs=[pl.BlockSpec((nrow, LANES), lambda j: (0, 0)),
                  pl.BlockSpec((LANES, tn), lambda j: (0, j)),
                  pl.BlockSpec((1, tn), lambda j: (0, j))],
        out_specs=pl.BlockSpec((nrow, tn), lambda j: (0, j)),
        out_shape=jax.ShapeDtypeStruct((nrow, ncol), F32),
        compiler_params=_params("parallel"),
        name="na_bias_table",
    )(rpb2, onehot, mask)
    toe = flat.reshape(NA_HEADS, n_dr, GRID_W, GRID_W)
    pairs = jnp.stack([toe[:, 0:NA_BIAS_PAIRS], toe[:, 1:NA_BIAS_PAIRS + 1]], axis=3)
    return jnp.transpose(pairs, (1, 0, 2, 3, 4)).reshape(NA_BIAS_PAIRS, NA_HEADS * GRID_W, LANES)


def _proj_ln_kernel(a_ref, w_ref, x_ref, g_ref, b_ref, o_ref):
    h = _dot(a_ref[...], w_ref[...])
    o_ref[...] = _layer_norm(ALPHA * x_ref[...] + h, g_ref[...], b_ref[...])


def _proj_res_ln(a, w, x, g, b, tm=512):
    t, d = x.shape
    row = pl.BlockSpec((tm, d), lambda i: (i, 0))
    vec = pl.BlockSpec((1, d), lambda i: (0, 0))
    return pl.pallas_call(
        _proj_ln_kernel,
        grid=(t // tm,),
        in_specs=[row, pl.BlockSpec((d, d), lambda i: (0, 0)), row, vec, vec],
        out_specs=row,
        out_shape=jax.ShapeDtypeStruct((t, d), F32),
        compiler_params=_params("parallel"),
        name="proj_res_ln",
    )(a, w, x, g, b)


def _ffn_kernel(x_ref, wg_ref, wu_ref, wd_ref, g_ref, b_ref, o_ref, xb_ref, acc_ref):
    f = pl.program_id(1)

    @pl.when(f == 0)
    def _():
        xb_ref[...] = x_ref[...].astype(BF16)
        acc_ref[...] = jnp.zeros_like(acc_ref)

    xb = xb_ref[...]
    gate = _dot(xb, wg_ref[...])
    up = _dot(xb, wu_ref[...])
    h = (gate * _sigmoid(gate) * up).astype(BF16)
    acc_ref[...] += _dot(h, wd_ref[...])

    @pl.when(f == pl.num_programs(1) - 1)
    def _():
        o_ref[...] = _layer_norm(ALPHA * x_ref[...] + acc_ref[...], g_ref[...], b_ref[...])


def _ffn_res_ln(x, wg, wu, wd, g, b, tm=512, tf=1408):
    t, d = x.shape
    ff = wg.shape[1]
    row = pl.BlockSpec((tm, d), lambda i, f: (i, 0))
    vec = pl.BlockSpec((1, d), lambda i, f: (0, 0))
    return pl.pallas_call(
        _ffn_kernel,
        grid=(t // tm, ff // tf),
        in_specs=[row,
                  pl.BlockSpec((d, tf), lambda i, f: (0, f)),
                  pl.BlockSpec((d, tf), lambda i, f: (0, f)),
                  pl.BlockSpec((tf, d), lambda i, f: (f, 0)),
                  vec, vec],
        out_specs=row,
        out_shape=jax.ShapeDtypeStruct((t, d), F32),
        scratch_shapes=[pltpu.VMEM((tm, d), BF16), pltpu.VMEM((tm, d), F32)],
        compiler_params=_params("parallel", "arbitrary"),
        name="ffn_res_ln",
    )(x, wg, wu, wd, g, b)


HALO = 8


def _rwkv_proj_kernel(seq_len, x_ref, xp_ref, xn_ref, mu_ref, wrkv_ref, w1_ref, w2_ref, w0_ref,
                      a1_ref, a2_ref, a0_ref, g1_ref, g2_ref, kscale_ref, hs_ref,
                      r_ref, k_ref, kk_ref, v_ref, g_ref, lw0_ref, lw1_ref, ia0_ref, ia1_ref):
    tm = x_ref.shape[0]
    i = pl.program_id(0)
    x = x_ref[...]
    row = lax.broadcasted_iota(jnp.int32, x.shape, 0)
    first_in_seq = (i * tm) % seq_len == 0
    last_in_seq = ((i + 1) * tm) % seq_len == 0
    prev_row = jnp.where(first_in_seq, 0.0, xp_ref[HALO - 1:HALO, :])
    next_row = jnp.where(last_in_seq, 0.0, xn_ref[0:1, :])
    x_prev = jnp.where(row == 0, prev_row, pltpu.roll(x, 1, axis=0))
    x_next = jnp.where(row == tm - 1, next_row, pltpu.roll(x, tm - 1, axis=0))
    xx = 0.5 * (x_prev + x_next) - x

    def mix(j):
        return (x + xx * mu_ref[j:j + 1, :]).astype(BF16)

    r_ref[...] = _dot(mix(0), wrkv_ref[0]).astype(r_ref.dtype)
    kf = _dot(mix(2), wrkv_ref[1])
    k_ref[...] = kf.astype(k_ref.dtype)
    v_ref[...] = _dot(mix(3), wrkv_ref[2]).astype(v_ref.dtype)
    hsum = hs_ref[...]
    for gi in range(D_MODEL // LANES):
        cols = slice(gi * LANES, (gi + 1) * LANES)
        kkr = kf[:, cols] * kscale_ref[:, cols]
        norm = jnp.sqrt(_dot_hilo(kkr * kkr, hsum))
        kk_ref[:, cols] = (kkr / jnp.maximum(norm, 1e-12)).astype(kk_ref.dtype)

    lane = lax.broadcasted_iota(jnp.int32, (tm, LANES), 1)
    dir0 = lane < DECAY_LORA

    lo = jnp.tanh(_dot(mix(1), w1_ref[...]))
    for z, out in enumerate((lw0_ref, lw1_ref)):
        lz = jnp.where(dir0 if z == 0 else ~dir0, lo, 0.0).astype(BF16)
        wl = w0_ref[z:z + 1, :] + _dot(lz, w2_ref[...])
        out[...] = (-math.exp(-0.5)) * _sigmoid(wl)

    al = _dot(mix(4), a1_ref[...])
    for z, out in enumerate((ia0_ref, ia1_ref)):
        az = jnp.where(dir0 if z == 0 else ~dir0, al, 0.0).astype(BF16)
        out[...] = _sigmoid(a0_ref[z:z + 1, :] + _dot(az, a2_ref[...])).astype(out.dtype)

    gl = _sigmoid(_dot(mix(5), g1_ref[...])).astype(BF16)
    g_ref[...] = _dot(gl, g2_ref[...]).astype(g_ref.dtype)


def _rwkv_proj(x, seq_len, mu, wrkv, w1, w2, w0, a1, a2, a0, g1, g2, kscale, tm=256):
    t, d = x.shape
    nb = t // HALO
    step = tm // HALO
    row = pl.BlockSpec((tm, d), lambda i: (i, 0))
    full = lambda shape: pl.BlockSpec(shape, lambda i: (0,) * len(shape))
    hsum = _head_sum_matrix(RWKV_HEAD_DIM)
    outs = [jax.ShapeDtypeStruct((t, d), BF16)] * 5 + [jax.ShapeDtypeStruct((t, d), F32)] * 2 + \
           [jax.ShapeDtypeStruct((t, d), BF16)] * 2
    return pl.pallas_call(
        functools.partial(_rwkv_proj_kernel, seq_len),
        grid=(t // tm,),
        in_specs=[row,
                  pl.BlockSpec((HALO, d), lambda i: (jnp.maximum(i * step - 1, 0), 0)),
                  pl.BlockSpec((HALO, d), lambda i: (jnp.minimum((i + 1) * step, nb - 1), 0)),
                  full(mu.shape), full(wrkv.shape), full(w1.shape), full(w2.shape), full(w0.shape),
                  full(a1.shape), full(a2.shape), full(a0.shape), full(g1.shape), full(g2.shape),
                  full(kscale.shape), full(hsum.shape)],
        out_specs=[row] * 9,
        out_shape=outs,
        compiler_params=_params("parallel"),
        name="rwkv_proj",
    )(x, x, x, mu, wrkv, w1, w2, w0, a1, a2, a0, g1, g2, kscale, hsum)


def _pair_blockdiag(y, lane_a):
    zero = jnp.zeros_like(y)
    return jnp.concatenate([jnp.where(lane_a, y, zero), jnp.where(lane_a, zero, y)], axis=0)


def _pair_matmul(x, y, lane_a):
    return _dot(x.astype(BF16), _pair_blockdiag(y.astype(BF16), lane_a))


def _scan_prepare(chains):
    c = SCAN_CHUNK
    t_idx = lax.broadcasted_iota(jnp.int32, (c, LANES), 0)
    s_idx = lax.broadcasted_iota(jnp.int32, (c, LANES), 1) % c
    lane_a = lax.broadcasted_iota(jnp.int32, (c, LANES), 1) < RWKV_HEAD_DIM
    eye = (s_idx == t_idx).astype(F32)
    tri_r = lax.broadcasted_iota(jnp.int32, (c, c), 0)
    tri_c = lax.broadcasted_iota(jnp.int32, (c, c), 1)
    bd = (lax.broadcasted_iota(jnp.int32, (LANES, LANES), 0) // RWKV_HEAD_DIM ==
          lax.broadcasted_iota(jnp.int32, (LANES, LANES), 1) // RWKV_HEAD_DIM)
    before = {rev: ((s_idx > t_idx) if rev else (s_idx < t_idx)) for rev in (False, True)}
    upto = {rev: before[rev] | (s_idx == t_idx) for rev in (False, True)}
    cum = {rev: ((tri_c >= tri_r) if rev else (tri_c <= tri_r)).astype(BF16) for rev in (False, True)}
    n = len(chains)

    lwc = [_dot_hilo_lhs_exact(cum[ch[0]], ch[6]) for ch in chains]
    vals = [ch[5] for ch in chains]

    at, rt, bh, kh, w_tot, lhs, rhs = [], [], [], [], [], [], []
    for (rev, ka, r, k, kk, v, lw, ia), lc in zip(chains, lwc):
        last = lc[0:1, :] if rev else lc[c - 1:c, :]
        w_in = jnp.exp(lc)
        w_ex = jnp.exp(lc - lw)
        w_inv = jnp.exp(-lc)
        w_end = jnp.exp(last)
        w_out = w_end * w_inv
        kf = k.astype(F32)
        kkf = kk.astype(F32)
        iaf = ia.astype(F32)
        kd = kf * (1.0 + (iaf - 1.0) * ka)
        bvec = kkf * iaf
        a_t = (-kkf) * w_ex
        r_t = r.astype(F32) * w_in
        at.append(a_t)
        rt.append(r_t)
        bh.append((bvec * w_out).astype(BF16))
        kh.append((kd * w_out).astype(BF16))
        w_tot.append(w_end)
        lhs.append(jnp.concatenate([a_t, r_t], axis=0).astype(BF16))
        rhs.append(jnp.concatenate([_pair_blockdiag((bvec * w_inv).astype(BF16), lane_a),
                                    _pair_blockdiag((kd * w_inv).astype(BF16), lane_a)], axis=0))

    gram = [_dot_nt(lhs[i], rhs[i]) for i in range(n)]
    a_ab = [jnp.where(before[chains[i][0]], gram[i][0:c, 0:LANES], 0.0) for i in range(n)]
    a_ak = [jnp.where(before[chains[i][0]], gram[i][0:c, LANES:2 * LANES], 0.0) for i in range(n)]
    a_rb = [jnp.where(upto[chains[i][0]], gram[i][c:2 * c, 0:LANES], 0.0) for i in range(n)]
    a_rk = [jnp.where(upto[chains[i][0]], gram[i][c:2 * c, LANES:2 * LANES], 0.0) for i in range(n)]

    tinv = [eye + a for a in a_ab]
    pw = a_ab
    for _ in range(int(math.log2(c)) - 1):
        pw = [_pair_matmul(p, p, lane_a) for p in pw]
        tinv = [tinv[i] + _pair_matmul(tinv[i], pw[i], lane_a) for i in range(n)]

    av = [_pair_matmul(a_ak[i], vals[i], lane_a) for i in range(n)]
    y_loc = [_pair_matmul(a_rk[i], vals[i], lane_a) for i in range(n)]
    pq = [_dot(tinv[i].astype(BF16),
               jnp.concatenate([_pair_blockdiag(at[i].astype(BF16), lane_a),
                                _pair_blockdiag(av[i].astype(BF16), lane_a)], axis=1)) for i in range(n)]
    pqb = [x.astype(BF16) for x in pq]
    mn = [_dot_tn(pqb[i], bh[i]) for i in range(n)]
    kv = [_dot_tn(vals[i], kh[i]) for i in range(n)]
    arb = [_dot(a_rb[i].astype(BF16),
                jnp.concatenate([_pair_blockdiag(pqb[i][:, 0:LANES], lane_a),
                                 _pair_blockdiag(pqb[i][:, LANES:2 * LANES], lane_a)], axis=1)) for i in range(n)]
    out = []
    for i in range(n):
        ry = (rt[i] + arb[i][:, 0:LANES]).astype(BF16)
        yc = arb[i][:, LANES:2 * LANES] + y_loc[i]
        m_state = jnp.where(bd, mn[i][0:LANES], 0.0).astype(BF16)
        n_state = jnp.where(bd, mn[i][LANES:2 * LANES] + kv[i], 0.0)
        out.append((ry, yc, m_state, n_state, w_tot[i]))
    return out


def _scan_kernel(n_pairs, rf_ref, kf_ref, kkf_ref, vf_ref, lwf_ref, iaf_ref,
                 rb_ref, kb_ref, kkb_ref, vb_ref, lwb_ref, iab_ref, ka_ref, yf_ref, yb_ref, s_ref):
    @pl.when(pl.program_id(2) == 0)
    def _():
        s_ref[...] = jnp.zeros_like(s_ref)

    c = SCAN_CHUNK
    n_chunks = SCAN_BLOCK // c
    dirs = ((False, (rf_ref, kf_ref, kkf_ref, vf_ref, lwf_ref, iaf_ref), yf_ref),
            (True, (rb_ref, kb_ref, kkb_ref, vb_ref, lwb_ref, iab_ref), yb_ref))
    chains, where = [], []
    for step in range(n_chunks):
        for p in range(n_pairs):
            cols = slice(p * LANES, (p + 1) * LANES)
            for di, (rev, refs, _) in enumerate(dirs):
                ci = n_chunks - 1 - step if rev else step
                rows = slice(ci * c, (ci + 1) * c)
                chains.append((rev, ka_ref[:, cols]) + tuple(ref[rows, cols] for ref in refs))
                where.append((2 * p + di, di, rows, cols))
    pre = _scan_prepare(chains)
    state = [s_ref[j] for j in range(2 * n_pairs)]
    for (j, di, rows, cols), (ry, yc, m_state, n_state, w_total) in zip(where, pre):
        s = state[j]
        sb = s.astype(BF16)
        dirs[di][2][rows, cols] = _dot_nt(ry, sb) + yc
        state[j] = s * w_total + _dot(sb, m_state) + n_state
    for j in range(2 * n_pairs):
        s_ref[j] = state[j]


def _rwkv_scan(r, k, kk, v, lw0, lw1, ia0, ia1, ka, batch, seq_len, n_pairs=2):
    t, d = r.shape
    ns = seq_len // SCAN_BLOCK
    width = n_pairs * LANES
    fwd = pl.BlockSpec((SCAN_BLOCK, width), lambda b, h, s: (b * ns + s, h))
    bwd = pl.BlockSpec((SCAN_BLOCK, width), lambda b, h, s: (b * ns + ns - 1 - s, h))
    vec = pl.BlockSpec((1, width), lambda b, h, s: (0, h))
    return pl.pallas_call(
        functools.partial(_scan_kernel, n_pairs),
        grid=(batch, d // width, ns),
        in_specs=[fwd] * 6 + [bwd] * 6 + [vec],
        out_specs=[fwd, bwd],
        out_shape=[jax.ShapeDtypeStruct((t, d), F32)] * 2,
        scratch_shapes=[pltpu.VMEM((2 * n_pairs, LANES, LANES), F32)],
        compiler_params=_params("parallel", "parallel", "arbitrary"),
        name="rwkv_scan",
    )(r, k, kk, v, lw0, ia0, r, k, kk, v, lw1, ia1, ka)


def _rwkv_out_kernel(yf_ref, yb_ref, r_ref, k_ref, v_ref, g_ref, ia0_ref, ia1_ref, x_ref,
                     lg_ref, lb_ref, rk_ref, ka_ref, hs_ref, wo_ref, ng_ref, nb_ref, o_ref, z_ref):
    hsum = hs_ref[...]
    inv = 1.0 / RWKV_HEAD_DIM
    for gi in range(D_MODEL // LANES):
        cols = slice(gi * LANES, (gi + 1) * LANES)
        y = yf_ref[:, cols] + yb_ref[:, cols]
        yc = y - _dot_hilo(y, hsum) * inv
        var = _dot_hilo(yc * yc, hsum) * inv
        yn = yc * lax.rsqrt(var + GN_EPS) * lg_ref[:, cols] + lb_ref[:, cols]
        rr = r_ref[:, cols].astype(F32)
        kk = k_ref[:, cols].astype(F32)
        ia = ia0_ref[:, cols].astype(F32) + ia1_ref[:, cols].astype(F32)
        ksum = kk * (2.0 + (ia - 2.0) * ka_ref[:, cols])
        bonus = _dot_hilo(rr * ksum * rk_ref[:, cols], hsum) * v_ref[:, cols].astype(F32)
        z_ref[:, cols] = ((yn + bonus) * g_ref[:, cols].astype(F32)).astype(BF16)
    h = _dot(z_ref[...], wo_ref[...])
    o_ref[...] = _layer_norm(ALPHA * x_ref[...] + h, ng_ref[...], nb_ref[...])


def _rwkv_out(yf, yb, r, k, v, g, ia0, ia1, x, lnx_g, lnx_b, r_k, k_a, wo, ng, nb, tm=256):
    t, d = x.shape
    row = pl.BlockSpec((tm, d), lambda i: (i, 0))
    vec = pl.BlockSpec((1, d), lambda i: (0, 0))
    return pl.pallas_call(
        _rwkv_out_kernel,
        grid=(t // tm,),
        in_specs=[row] * 9 + [vec] * 4 + [pl.BlockSpec((LANES, LANES), lambda i: (0, 0)),
                                          pl.BlockSpec((d, d), lambda i: (0, 0)), vec, vec],
        out_specs=row,
        out_shape=jax.ShapeDtypeStruct((t, d), F32),
        scratch_shapes=[pltpu.VMEM((tm, d), BF16)],
        compiler_params=_params("parallel"),
        name="rwkv_out",
    )(yf, yb, r, k, v, g, ia0, ia1, x, lnx_g, lnx_b, r_k, k_a, _head_sum_matrix(RWKV_HEAD_DIM), wo, ng, nb)


def _router_kernel(x_ref, w_ref, b_ref, o_ref):
    xh, xl = _split(x_ref[...])
    wh = w_ref[0]
    wl = w_ref[1]
    logits = _dot(xh, wh) + (_dot(xh, wl) + _dot(xl, wh)) + b_ref[...]
    lane = lax.broadcasted_iota(jnp.int32, logits.shape, 1)
    logits = jnp.where(lane < N_EXPERTS, logits, -jnp.inf)
    m1 = jnp.max(logits, axis=-1, keepdims=True)
    i1 = jnp.min(jnp.where(logits == m1, lane, LANES), axis=-1, keepdims=True)
    rest = jnp.where(lane == i1, -jnp.inf, logits)
    m2 = jnp.max(rest, axis=-1, keepdims=True)
    i2 = jnp.min(jnp.where(rest == m2, lane, LANES), axis=-1, keepdims=True)
    e2 = jnp.exp(m2 - m1)
    g1 = 1.0 / (1.0 + e2)
    g2 = e2 / (1.0 + e2)
    o_ref[...] = jnp.where(lane == i1, g1, 0.0) + jnp.where(lane == i2, g2, 0.0)


def _router(x, w_hilo, bias, tm=512):
    t, d = x.shape
    return pl.pallas_call(
        _router_kernel,
        grid=(t // tm,),
        in_specs=[pl.BlockSpec((tm, d), lambda i: (i, 0)),
                  pl.BlockSpec((2, d, LANES), lambda i: (0, 0, 0)),
                  pl.BlockSpec((1, LANES), lambda i: (0, 0))],
        out_specs=pl.BlockSpec((tm, LANES), lambda i: (i, 0)),
        out_shape=jax.ShapeDtypeStruct((t, LANES), F32),
        compiler_params=_params("parallel"),
        name="moe_router",
    )(x, w_hilo, bias)


def _moe_kernel(x_ref, c_ref, wg_ref, wu_ref, wd_ref, g_ref, b_ref, o_ref, xb_ref, acc_ref, tot_ref):
    e = pl.program_id(1)
    f = pl.program_id(2)
    nf = pl.num_programs(2)

    @pl.when((e == 0) & (f == 0))
    def _():
        xb_ref[...] = x_ref[...].astype(BF16)
        tot_ref[...] = jnp.zeros_like(tot_ref)

    @pl.when(f == 0)
    def _():
        acc_ref[...] = jnp.zeros_like(acc_ref)

    xb = xb_ref[...]
    gate = _dot(xb, wg_ref[...])
    up = _dot(xb, wu_ref[...])
    h = (gate * _sigmoid(gate) * up).astype(BF16)
    acc_ref[...] += _dot(h, wd_ref[...])

    @pl.when(f == nf - 1)
    def _():
        comb = c_ref[...]
        lane = lax.broadcasted_iota(jnp.int32, comb.shape, 1)
        ce = jnp.sum(jnp.where(lane == e, comb, 0.0), axis=-1, keepdims=True)
        tot_ref[...] += ce * acc_ref[...]

    @pl.when((e == pl.num_programs(1) - 1) & (f == nf - 1))
    def _():
        o_ref[...] = _layer_norm(ALPHA * x_ref[...] + tot_ref[...], g_ref[...], b_ref[...])


def _moe_res_ln(x, comb, wg, wu, wd, g, b, tm=1024, tf=512):
    t, d = x.shape
    ne, _, fe = wg.shape
    row = pl.BlockSpec((tm, d), lambda i, e, f: (i, 0))
    vec = pl.BlockSpec((1, d), lambda i, e, f: (0, 0))
    return pl.pallas_call(
        _moe_kernel,
        grid=(t // tm, ne, fe // tf),
        in_specs=[row,
                  pl.BlockSpec((tm, LANES), lambda i, e, f: (i, 0)),
                  pl.BlockSpec((None, d, tf), lambda i, e, f: (e, 0, f)),
                  pl.BlockSpec((None, d, tf), lambda i, e, f: (e, 0, f)),
                  pl.BlockSpec((None, tf, d), lambda i, e, f: (e, f, 0)),
                  vec, vec],
        out_specs=row,
        out_shape=jax.ShapeDtypeStruct((t, d), F32),
        scratch_shapes=[pltpu.VMEM((tm, d), BF16), pltpu.VMEM((tm, d), F32), pltpu.VMEM((tm, d), F32)],
        compiler_params=_params("parallel", "arbitrary", "arbitrary"),
        name="moe_res_ln",
    )(x, comb, wg, wu, wd, g, b)


def _prepare_weights(na_w_qkv, na_rpb, na_w_o, ffn_w_gate, ffn_w_up, ffn_w_down,
                     rwkv_mu, rwkv_w_rkv, rwkv_w0, rwkv_w1, rwkv_w2, rwkv_a0, rwkv_a1, rwkv_a2,
                     rwkv_g1, rwkv_g2, rwkv_k_k, rwkv_k_a, rwkv_r_k, rwkv_lnx_g, rwkv_lnx_b, rwkv_w_o,
                     moe_w_router, moe_b_router, moe_w_gate, moe_w_up, moe_w_down,
                     ln_mix_g, ln_mix_b, ln_ffn_g, ln_ffn_b):
    d = D_MODEL
    vec = lambda a: a.reshape(1, d).astype(F32)
    gl_pad = 2 * LANES - GATE_LORA
    router_w = jnp.pad(moe_w_router[0], ((0, 0), (0, LANES - N_EXPERTS)))
    router_hi = router_w.astype(BF16)
    router_lo = (router_w - router_hi.astype(F32)).astype(BF16)
    return dict(
        qkv_w=na_w_qkv[0].astype(BF16),
        qkv_scale=jnp.concatenate([jnp.full((1, d), NA_HEAD_DIM ** -0.5, F32), jnp.ones((1, 2 * d), F32)], axis=1),
        na_bias=_na_bias_table(na_rpb[0]),
        na_wo=na_w_o[0].astype(BF16),
        ffn_wg=ffn_w_gate[0].astype(BF16), ffn_wu=ffn_w_up[0].astype(BF16), ffn_wd=ffn_w_down[0].astype(BF16),
        mu=rwkv_mu[0].astype(F32),
        wrkv=rwkv_w_rkv[0].astype(BF16),
        w1=jnp.concatenate([rwkv_w1[0, 0], rwkv_w1[0, 1]], axis=1).astype(BF16),
        w2=jnp.concatenate([rwkv_w2[0, 0], rwkv_w2[0, 1]], axis=0).astype(BF16),
        w0=rwkv_w0[0].astype(F32),
        a1=jnp.concatenate([rwkv_a1[0, 0], rwkv_a1[0, 1]], axis=1).astype(BF16),
        a2=jnp.concatenate([rwkv_a2[0, 0], rwkv_a2[0, 1]], axis=0).astype(BF16),
        a0=rwkv_a0[0].astype(F32),
        g1=jnp.pad(rwkv_g1[0], ((0, 0), (0, gl_pad))).astype(BF16),
        g2=jnp.pad(rwkv_g2[0], ((0, gl_pad), (0, 0))).astype(BF16),
        k_k=vec(rwkv_k_k[0]), k_a=vec(rwkv_k_a[0]), r_k=vec(rwkv_r_k[0]),
        lnx_g=vec(rwkv_lnx_g[0]), lnx_b=vec(rwkv_lnx_b[0]),
        rwkv_wo=rwkv_w_o[0].astype(BF16),
        router_w=jnp.stack([router_hi, router_lo]),
        router_b=jnp.pad(moe_b_router[0], (0, LANES - N_EXPERTS)).reshape(1, LANES).astype(F32),
        moe_wg=moe_w_gate[0].astype(BF16), moe_wu=moe_w_up[0].astype(BF16), moe_wd=moe_w_down[0].astype(BF16),
        ln_mix_g=ln_mix_g.astype(F32), ln_mix_b=ln_mix_b.astype(F32),
        ln_ffn_g=ln_ffn_g.astype(F32), ln_ffn_b=ln_ffn_b.astype(F32),
    )


def _trunk(x3, w):
    batch, seq_len, d = x3.shape
    rows = seq_len // GRID_W
    x = x3.reshape(batch * seq_len, d)
    ln = lambda a, i: a[i].reshape(1, d)

    qkv = _qkv_proj(x, w["qkv_w"], w["qkv_scale"])
    att = _na_attention(qkv, w["na_bias"], batch, rows)
    x = _proj_res_ln(att, w["na_wo"], x, ln(w["ln_mix_g"], 0), ln(w["ln_mix_b"], 0))
    x = _ffn_res_ln(x, w["ffn_wg"], w["ffn_wu"], w["ffn_wd"], ln(w["ln_ffn_g"], 0), ln(w["ln_ffn_b"], 0))

    r, k, kk, v, g, lw0, lw1, ia0, ia1 = _rwkv_proj(x, seq_len, w["mu"], w["wrkv"], w["w1"], w["w2"], w["w0"],
                                                    w["a1"], w["a2"], w["a0"], w["g1"], w["g2"], w["k_k"])
    yf, yb = _rwkv_scan(r, k, kk, v, lw0, lw1, ia0, ia1, w["k_a"], batch, seq_len)
    x = _rwkv_out(yf, yb, r, k, v, g, ia0, ia1, x, w["lnx_g"], w["lnx_b"], w["r_k"], w["k_a"], w["rwkv_wo"],
                  ln(w["ln_mix_g"], 1), ln(w["ln_mix_b"], 1))
    comb = _router(x, w["router_w"], w["router_b"])
    x = _moe_res_ln(x, comb, w["moe_wg"], w["moe_wu"], w["moe_wd"], ln(w["ln_ffn_g"], 1), ln(w["ln_ffn_b"], 1))
    return x.reshape(batch, seq_len, d)


def kernel(x_prompt, x_sample, na_w_qkv, na_rpb, na_w_o, ffn_w_gate, ffn_w_up, ffn_w_down, rwkv_mu, rwkv_w_rkv, rwkv_w0, rwkv_w1, rwkv_w2, rwkv_a0, rwkv_a1, rwkv_a2, rwkv_g1, rwkv_g2, rwkv_k_k, rwkv_k_a, rwkv_r_k, rwkv_lnx_g, rwkv_lnx_b, rwkv_w_o, moe_w_router, moe_b_router, moe_w_gate, moe_w_up, moe_w_down, ln_mix_g, ln_mix_b, ln_ffn_g, ln_ffn_b):
    w = _prepare_weights(na_w_qkv, na_rpb, na_w_o, ffn_w_gate, ffn_w_up, ffn_w_down,
                         rwkv_mu, rwkv_w_rkv, rwkv_w0, rwkv_w1, rwkv_w2, rwkv_a0, rwkv_a1, rwkv_a2,
                         rwkv_g1, rwkv_g2, rwkv_k_k, rwkv_k_a, rwkv_r_k, rwkv_lnx_g, rwkv_lnx_b, rwkv_w_o,
                         moe_w_router, moe_b_router, moe_w_gate, moe_w_up, moe_w_down,
                         ln_mix_g, ln_mix_b, ln_ffn_g, ln_ffn_b)
    return (_trunk(x_prompt, w), _trunk(x_sample, w))
```

```python
import functools
import math

import numpy as np
import jax
import jax.numpy as jnp
from jax import lax
from jax.experimental import pallas as pl
from jax.experimental.pallas import tpu as pltpu

D_MODEL = 1024
DEPTH = 2
GRID_W = 64
NA_HEAD_DIM = 32
NA_HEADS = D_MODEL // NA_HEAD_DIM
NA_ROWS = 8
NA_COLS = 16
RWKV_HEAD_DIM = 64
DECAY_LORA = 64
ICLR_LORA = 64
GATE_LORA = 160
GN_EPS = 64e-5
D_FF = 2816
N_EXPERTS = 8
D_EXPERT = 3584
LN_EPS = 1e-5
ALPHA = (2 * DEPTH) ** 0.25

LANES = 128
SCAN_CHUNK = 64
SCAN_BLOCK = 256
VMEM_LIMIT = 56 * 1024 * 1024

F32 = jnp.float32
BF16 = jnp.bfloat16


def _params(*sem):
    return pltpu.CompilerParams(dimension_semantics=sem, vmem_limit_bytes=VMEM_LIMIT)


def _dot(a, b):
    return jnp.dot(a, b, preferred_element_type=F32)


def _dot_nt(a, b):
    return lax.dot_general(a, b, (((1,), (1,)), ((), ())), preferred_element_type=F32)


def _dot_tn(a, b):
    return lax.dot_general(a, b, (((0,), (0,)), ((), ())), preferred_element_type=F32)


def _split(a):
    hi = a.astype(BF16)
    lo = (a - hi.astype(F32)).astype(BF16)
    return hi, lo


def _dot_hilo(a, b_exact):
    hi, lo = _split(a)
    return _dot(hi, b_exact) + _dot(lo, b_exact)


def _dot_hilo_lhs_exact(a_exact, b):
    hi, lo = _split(b)
    return _dot(a_exact, hi) + _dot(a_exact, lo)


def _layer_norm(z, g, b):
    mu = jnp.mean(z, axis=-1, keepdims=True)
    zc = z - mu
    var = jnp.mean(zc * zc, axis=-1, keepdims=True)
    return zc * lax.rsqrt(var + LN_EPS) * g + b


def _sigmoid(z):
    return 1.0 / (1.0 + jnp.exp(-z))


def _head_sum_matrix(head_dim):
    lane = np.arange(LANES)
    return jnp.asarray((lane[:, None] // head_dim) == (lane[None, :] // head_dim), BF16)


def _qkv_kernel(x_ref, w_ref, s_ref, o_ref):
    acc = _dot(x_ref[...].astype(BF16), w_ref[...])
    o_ref[...] = (acc * s_ref[...]).astype(o_ref.dtype)


def _qkv_proj(x, w, colscale, tm=512):
    t, d = x.shape
    n = w.shape[1]
    return pl.pallas_call(
        _qkv_kernel,
        grid=(t // tm,),
        in_specs=[pl.BlockSpec((tm, d), lambda i: (i, 0)),
                  pl.BlockSpec((d, n), lambda i: (0, 0)),
                  pl.BlockSpec((1, n), lambda i: (0, 0))],
        out_specs=pl.BlockSpec((tm, n), lambda i: (i, 0)),
        out_shape=jax.ShapeDtypeStruct((t, n), BF16),
        compiler_params=_params("parallel"),
        name="qkv_proj",
    )(x, w, colscale)


NA_HEADS_PER_GROUP = LANES // NA_HEAD_DIM
NA_GROUPS = D_MODEL // LANES
NA_WIN = NA_ROWS * GRID_W


NA_GROUP_ROWS = NA_HEADS_PER_GROUP * GRID_W
NA_ROW_PAIRS = NA_ROWS // 2
NA_BIAS_PAIRS = 2 * NA_ROWS - 2


def _na_kernel(rows, q_ref, k_ref, v_ref, b_ref, o_ref):
    r = pl.program_id(1)
    off = r - jnp.clip(r - NA_ROWS // 2, 0, rows - NA_ROWS)
    lane_head = lax.broadcasted_iota(jnp.int32, (GRID_W, LANES), 1) // NA_HEAD_DIM
    groups = [slice(g * LANES, (g + 1) * LANES) for g in range(NA_GROUPS)]

    scores = []
    for cols in groups:
        qg = q_ref[:, cols]
        zero = jnp.zeros_like(qg)
        qm = jnp.concatenate([jnp.where(lane_head == h, qg, zero) for h in range(NA_HEADS_PER_GROUP)], axis=0)
        scores.append(_dot_nt(qm, k_ref[:, cols]))

    probs, denoms = [], []
    for g, s in enumerate(scores):
        hq = pl.ds(g * NA_GROUP_ROWS, NA_GROUP_ROWS)
        s = jnp.concatenate([s[:, m * LANES:(m + 1) * LANES] + b_ref[2 * m - off + NA_ROWS - 1, hq, :]
                             for m in range(NA_ROW_PAIRS)], axis=1)
        p = jnp.exp(s - jnp.max(s, axis=-1, keepdims=True))
        denoms.append(jnp.sum(p, axis=-1, keepdims=True))
        probs.append(p.astype(BF16))

    for g, cols in enumerate(groups):
        o4 = _dot(probs[g], v_ref[:, cols]) / denoms[g]
        og = jnp.zeros((GRID_W, LANES), F32)
        for h in range(NA_HEADS_PER_GROUP):
            og = og + jnp.where(lane_head == h, o4[h * GRID_W:(h + 1) * GRID_W], 0.0)
        o_ref[:, cols] = og.astype(o_ref.dtype)


def _na_attention(qkv, bias, batch, rows):
    t = qkv.shape[0]
    half = NA_ROWS // 2

    def win_start(r):
        return jnp.clip(r - half, 0, rows - NA_ROWS)

    def q_map(b, r):
        return (b * rows + r, 0)

    def k_map(b, r):
        return ((b * rows + win_start(r)) * GRID_W, D_MODEL)

    def v_map(b, r):
        return ((b * rows + win_start(r)) * GRID_W, 2 * D_MODEL)

    return pl.pallas_call(
        functools.partial(_na_kernel, rows),
        grid=(batch, rows),
        in_specs=[pl.BlockSpec((GRID_W, D_MODEL), q_map),
                  pl.BlockSpec((pl.Element(NA_WIN), pl.Element(D_MODEL)), k_map),
                  pl.BlockSpec((pl.Element(NA_WIN), pl.Element(D_MODEL)), v_map),
                  pl.BlockSpec(bias.shape, lambda b, r: (0, 0, 0), pipeline_mode=pl.Buffered(1))],
        out_specs=pl.BlockSpec((GRID_W, D_MODEL), q_map),
        out_shape=jax.ShapeDtypeStruct((t, D_MODEL), BF16),
        compiler_params=_params("parallel", "arbitrary"),
        name="na_attention",
    )(qkv, qkv, qkv, bias)


def _na_bias_kernel(rpb_ref, oh_ref, mask_ref, o_ref):
    x = rpb_ref[...]
    hi = x.astype(BF16)
    r1 = x - hi.astype(F32)
    mid = r1.astype(BF16)
    lo = (r1 - mid.astype(F32)).astype(BF16)
    oh = oh_ref[...]
    o_ref[...] = _dot(hi, oh) + _dot(mid, oh) + _dot(lo, oh) + mask_ref[...]


def _na_bias_table(rpb, tn=1024):
    n_dr, n_dc = 2 * NA_ROWS - 1, 2 * NA_COLS - 1
    c = np.arange(GRID_W)
    q_start = np.clip(c - NA_COLS // 2, 0, GRID_W - NA_COLS)
    kc = np.arange(GRID_W)
    valid = (kc[None, :] >= q_start[:, None]) & (kc[None, :] < q_start[:, None] + NA_COLS)
    dc = kc[None, :] - c[:, None] + NA_COLS - 1
    onehot = (np.arange(LANES)[:, None, None] == dc[None]) & valid[None]
    onehot = jnp.asarray(onehot.reshape(LANES, GRID_W * GRID_W), BF16)
    mask = jnp.asarray(np.where(valid, 0.0, -np.inf).reshape(1, GRID_W * GRID_W), F32)
    rpb2 = jnp.pad(rpb.reshape(NA_HEADS * n_dr, n_dc).astype(F32), ((0, 0), (0, LANES - n_dc)))
    nrow, ncol = rpb2.shape[0], GRID_W * GRID_W
    flat = pl.pallas_call(
        _na_bias_kernel,
        grid=(ncol // tn,),
        in_specs=[pl.BlockSpec((nrow, LANES), lambda j: (0, 0)),
                  pl.BlockSpec((LANES, tn), lambda j: (0, j)),
                  pl.BlockSpec((1, tn), lambda j: (0, j))],
        out_specs=pl.BlockSpec((nrow, tn), lambda j: (0, j)),
        out_shape=jax.ShapeDtypeStruct((nrow, ncol), F32),
        compiler_params=_params("parallel"),
        name="na_bias_table",
    )(rpb2, onehot, mask)
    toe = flat.reshape(NA_HEADS, n_dr, GRID_W, GRID_W)
    pairs = jnp.stack([toe[:, 0:NA_BIAS_PAIRS], toe[:, 1:NA_BIAS_PAIRS + 1]], axis=3)
    return jnp.transpose(pairs, (1, 0, 2, 3, 4)).reshape(NA_BIAS_PAIRS, NA_HEADS * GRID_W, LANES)


def _proj_ln_kernel(a_ref, w_ref, x_ref, g_ref, b_ref, o_ref):
    h = _dot(a_ref[...], w_ref[...])
    o_ref[...] = _layer_norm(ALPHA * x_ref[...] + h, g_ref[...], b_ref[...])


def _proj_res_ln(a, w, x, g, b, tm=512):
    t, d = x.shape
    row = pl.BlockSpec((tm, d), lambda i: (i, 0))
    vec = pl.BlockSpec((1, d), lambda i: (0, 0))
    return pl.pallas_call(
        _proj_ln_kernel,
        grid=(t // tm,),
        in_specs=[row, pl.BlockSpec((d, d), lambda i: (0, 0)), row, vec, vec],
        out_specs=row,
        out_shape=jax.ShapeDtypeStruct((t, d), F32),
        compiler_params=_params("parallel"),
        name="proj_res_ln",
    )(a, w, x, g, b)


def _ffn_kernel(x_ref, wg_ref, wu_ref, wd_ref, g_ref, b_ref, o_ref, xb_ref, acc_ref):
    f = pl.program_id(1)

    @pl.when(f == 0)
    def _():
        xb_ref[...] = x_ref[...].astype(BF16)
        acc_ref[...] = jnp.zeros_like(acc_ref)

    xb = xb_ref[...]
    gate = _dot(xb, wg_ref[...])
    up = _dot(xb, wu_ref[...])
    h = (gate * _sigmoid(gate) * up).astype(BF16)
    acc_ref[...] += _dot(h, wd_ref[...])

    @pl.when(f == pl.num_programs(1) - 1)
    def _():
        o_ref[...] = _layer_norm(ALPHA * x_ref[...] + acc_ref[...], g_ref[...], b_ref[...])


def _ffn_res_ln(x, wg, wu, wd, g, b, tm=512, tf=1408):
    t, d = x.shape
    ff = wg.shape[1]
    row = pl.BlockSpec((tm, d), lambda i, f: (i, 0))
    vec = pl.BlockSpec((1, d), lambda i, f: (0, 0))
    return pl.pallas_call(
        _ffn_kernel,
        grid=(t // tm, ff // tf),
        in_specs=[row,
                  pl.BlockSpec((d, tf), lambda i, f: (0, f)),
                  pl.BlockSpec((d, tf), lambda i, f: (0, f)),
                  pl.BlockSpec((tf, d), lambda i, f: (f, 0)),
                  vec, vec],
        out_specs=row,
        out_shape=jax.ShapeDtypeStruct((t, d), F32),
        scratch_shapes=[pltpu.VMEM((tm, d), BF16), pltpu.VMEM((tm, d), F32)],
        compiler_params=_params("parallel", "arbitrary"),
        name="ffn_res_ln",
    )(x, wg, wu, wd, g, b)


HALO = 8


def _rwkv_proj_kernel(seq_len, x_ref, xp_ref, xn_ref, mu_ref, wrkv_ref, w1_ref, w2_ref, w0_ref,
                      a1_ref, a2_ref, a0_ref, g1_ref, g2_ref, kscale_ref, hs_ref,
                      r_ref, k_ref, kk_ref, v_ref, g_ref, lw0_ref, lw1_ref, ia0_ref, ia1_ref):
    tm = x_ref.shape[0]
    i = pl.program_id(0)
    x = x_ref[...]
    row = lax.broadcasted_iota(jnp.int32, x.shape, 0)
    first_in_seq = (i * tm) % seq_len == 0
    last_in_seq = ((i + 1) * tm) % seq_len == 0
    prev_row = jnp.where(first_in_seq, 0.0, xp_ref[HALO - 1:HALO, :])
    next_row = jnp.where(last_in_seq, 0.0, xn_ref[0:1, :])
    x_prev = jnp.where(row == 0, prev_row, pltpu.roll(x, 1, axis=0))
    x_next = jnp.where(row == tm - 1, next_row, pltpu.roll(x, tm - 1, axis=0))
    xx = 0.5 * (x_prev + x_next) - x

    def mix(j):
        return (x + xx * mu_ref[j:j + 1, :]).astype(BF16)

    r_ref[...] = _dot(mix(0), wrkv_ref[0]).astype(r_ref.dtype)
    kf = _dot(mix(2), wrkv_ref[1])
    k_ref[...] = kf.astype(k_ref.dtype)
    v_ref[...] = _dot(mix(3), wrkv_ref[2]).astype(v_ref.dtype)
    hsum = hs_ref[...]
    for gi in range(D_MODEL // LANES):
        cols = slice(gi * LANES, (gi + 1) * LANES)
        kkr = kf[:, cols] * kscale_ref[:, cols]
        norm = jnp.sqrt(_dot_hilo(kkr * kkr, hsum))
        kk_ref[:, cols] = (kkr / jnp.maximum(norm, 1e-12)).astype(kk_ref.dtype)

    lane = lax.broadcasted_iota(jnp.int32, (tm, LANES), 1)
    dir0 = lane < DECAY_LORA

    lo = jnp.tanh(_dot(mix(1), w1_ref[...]))
    for z, out in enumerate((lw0_ref, lw1_ref)):
        lz = jnp.where(dir0 if z == 0 else ~dir0, lo, 0.0).astype(BF16)
        wl = w0_ref[z:z + 1, :] + _dot(lz, w2_ref[...])
        out[...] = (-math.exp(-0.5)) * _sigmoid(wl)

    al = _dot(mix(4), a1_ref[...])
    for z, out in enumerate((ia0_ref, ia1_ref)):
        az = jnp.where(dir0 if z == 0 else ~dir0, al, 0.0).astype(BF16)
        out[...] = _sigmoid(a0_ref[z:z + 1, :] + _dot(az, a2_ref[...])).astype(out.dtype)

    gl = _sigmoid(_dot(mix(5), g1_ref[...])).astype(BF16)
    g_ref[...] = _dot(gl, g2_ref[...]).astype(g_ref.dtype)


def _rwkv_proj(x, seq_len, mu, wrkv, w1, w2, w0, a1, a2, a0, g1, g2, kscale, tm=256):
    t, d = x.shape
    nb = t // HALO
    step = tm // HALO
    row = pl.BlockSpec((tm, d), lambda i: (i, 0))
    full = lambda shape: pl.BlockSpec(shape, lambda i: (0,) * len(shape))
    hsum = _head_sum_matrix(RWKV_HEAD_DIM)
    outs = [jax.ShapeDtypeStruct((t, d), BF16)] * 5 + [jax.ShapeDtypeStruct((t, d), F32)] * 2 + \
           [jax.ShapeDtypeStruct((t, d), BF16)] * 2
    return pl.pallas_call(
        functools.partial(_rwkv_proj_kernel, seq_len),
        grid=(t // tm,),
        in_specs=[row,
                  pl.BlockSpec((HALO, d), lambda i: (jnp.maximum(i * step - 1, 0), 0)),
                  pl.BlockSpec((HALO, d), lambda i: (jnp.minimum((i + 1) * step, nb - 1), 0)),
                  full(mu.shape), full(wrkv.shape), full(w1.shape), full(w2.shape), full(w0.shape),
                  full(a1.shape), full(a2.shape), full(a0.shape), full(g1.shape), full(g2.shape),
                  full(kscale.shape), full(hsum.shape)],
        out_specs=[row] * 9,
        out_shape=outs,
        compiler_params=_params("parallel"),
        name="rwkv_proj",
    )(x, x, x, mu, wrkv, w1, w2, w0, a1, a2, a0, g1, g2, kscale, hsum)


def _pair_blockdiag(y, lane_a):
    zero = jnp.zeros_like(y)
    return jnp.concatenate([jnp.where(lane_a, y, zero), jnp.where(lane_a, zero, y)], axis=0)


def _pair_matmul(x, y, lane_a):
    return _dot(x.astype(BF16), _pair_blockdiag(y.astype(BF16), lane_a))


def _scan_prepare(chains):
    c = SCAN_CHUNK
    t_idx = lax.broadcasted_iota(jnp.int32, (c, LANES), 0)
    s_idx = lax.broadcasted_iota(jnp.int32, (c, LANES), 1) % c
    lane_a = lax.broadcasted_iota(jnp.int32, (c, LANES), 1) < RWKV_HEAD_DIM
    eye = (s_idx == t_idx).astype(F32)
    tri_r = lax.broadcasted_iota(jnp.int32, (c, c), 0)
    tri_c = lax.broadcasted_iota(jnp.int32, (c, c), 1)
    bd = (lax.broadcasted_iota(jnp.int32, (LANES, LANES), 0) // RWKV_HEAD_DIM ==
          lax.broadcasted_iota(jnp.int32, (LANES, LANES), 1) // RWKV_HEAD_DIM)
    before = {rev: ((s_idx > t_idx) if rev else (s_idx < t_idx)) for rev in (False, True)}
    upto = {rev: before[rev] | (s_idx == t_idx) for rev in (False, True)}
    cum = {rev: ((tri_c >= tri_r) if rev else (tri_c <= tri_r)).astype(BF16) for rev in (False, True)}
    n = len(chains)

    lwc = [_dot_hilo_lhs_exact(cum[ch[0]], ch[6]) for ch in chains]
    vals = [ch[5] for ch in chains]

    at, rt, bh, kh, w_tot, lhs, rhs = [], [], [], [], [], [], []
    for (rev, ka, r, k, kk, v, lw, ia), lc in zip(chains, lwc):
        last = lc[0:1, :] if rev else lc[c - 1:c, :]
        w_in = jnp.exp(lc)
        w_ex = jnp.exp(lc - lw)
        w_inv = jnp.exp(-lc)
        w_end = jnp.exp(last)
        w_out = w_end * w_inv
        kf = k.astype(F32)
        kkf = kk.astype(F32)
        iaf = ia.astype(F32)
        kd = kf * (1.0 + (iaf - 1.0) * ka)
        bvec = kkf * iaf
        a_t = (-kkf) * w_ex
        r_t = r.astype(F32) * w_in
        at.append(a_t)
        rt.append(r_t)
        bh.append((bvec * w_out).astype(BF16))
        kh.append((kd * w_out).astype(BF16))
        w_tot.append(w_end)
        lhs.append(jnp.concatenate([a_t, r_t], axis=0).astype(BF16))
        rhs.append(jnp.concatenate([_pair_blockdiag((bvec * w_inv).astype(BF16), lane_a),
                                    _pair_blockdiag((kd * w_inv).astype(BF16), lane_a)], axis=0))

    gram = [_dot_nt(lhs[i], rhs[i]) for i in range(n)]
    a_ab = [jnp.where(before[chains[i][0]], gram[i][0:c, 0:LANES], 0.0) for i in range(n)]
    a_ak = [jnp.where(before[chains[i][0]], gram[i][0:c, LANES:2 * LANES], 0.0) for i in range(n)]
    a_rb = [jnp.where(upto[chains[i][0]], gram[i][c:2 * c, 0:LANES], 0.0) for i in range(n)]
    a_rk = [jnp.where(upto[chains[i][0]], gram[i][c:2 * c, LANES:2 * LANES], 0.0) for i in range(n)]

    tinv = [eye + a for a in a_ab]
    pw = a_ab
    for _ in range(int(math.log2(c)) - 1):
        pw = [_pair_matmul(p, p, lane_a) for p in pw]
        tinv = [tinv[i] + _pair_matmul(tinv[i], pw[i], lane_a) for i in range(n)]

    av = [_pair_matmul(a_ak[i], vals[i], lane_a) for i in range(n)]
    y_loc = [_pair_matmul(a_rk[i], vals[i], lane_a) for i in range(n)]
    pq = [_dot(tinv[i].astype(BF16),
               jnp.concatenate([_pair_blockdiag(at[i].astype(BF16), lane_a),
                                _pair_blockdiag(av[i].astype(BF16), lane_a)], axis=1)) for i in range(n)]
    pqb = [x.astype(BF16) for x in pq]
    mn = [_dot_tn(pqb[i], bh[i]) for i in range(n)]
    kv = [_dot_tn(vals[i], kh[i]) for i in range(n)]
    arb = [_dot(a_rb[i].astype(BF16),
                jnp.concatenate([_pair_blockdiag(pqb[i][:, 0:LANES], lane_a),
                                 _pair_blockdiag(pqb[i][:, LANES:2 * LANES], lane_a)], axis=1)) for i in range(n)]
    out = []
    for i in range(n):
        ry = (rt[i] + arb[i][:, 0:LANES]).astype(BF16)
        yc = arb[i][:, LANES:2 * LANES] + y_loc[i]
        m_state = jnp.where(bd, mn[i][0:LANES], 0.0).astype(BF16)
        n_state = jnp.where(bd, mn[i][LANES:2 * LANES] + kv[i], 0.0)
        out.append((ry, yc, m_state, n_state, w_tot[i]))
    return out


def _scan_kernel(n_pairs, rf_ref, kf_ref, kkf_ref, vf_ref, lwf_ref, iaf_ref,
                 rb_ref, kb_ref, kkb_ref, vb_ref, lwb_ref, iab_ref, ka_ref, yf_ref, yb_ref, s_ref):
    @pl.when(pl.program_id(2) == 0)
    def _():
        s_ref[...] = jnp.zeros_like(s_ref)

    c = SCAN_CHUNK
    n_chunks = SCAN_BLOCK // c
    dirs = ((False, (rf_ref, kf_ref, kkf_ref, vf_ref, lwf_ref, iaf_ref), yf_ref),
            (True, (rb_ref, kb_ref, kkb_ref, vb_ref, lwb_ref, iab_ref), yb_ref))
    chains, where = [], []
    for step in range(n_chunks):
        for p in range(n_pairs):
            cols = slice(p * LANES, (p + 1) * LANES)
            for di, (rev, refs, _) in enumerate(dirs):
                ci = n_chunks - 1 - step if rev else step
                rows = slice(ci * c, (ci + 1) * c)
                chains.append((rev, ka_ref[:, cols]) + tuple(ref[rows, cols] for ref in refs))
                where.append((2 * p + di, di, rows, cols))
    pre = _scan_prepare(chains)
    state = [s_ref[j] for j in range(2 * n_pairs)]
    for (j, di, rows, cols), (ry, yc, m_state, n_state, w_total) in zip(where, pre):
        s = state[j]
        sb = s.astype(BF16)
        dirs[di][2][rows, cols] = _dot_nt(ry, sb) + yc
        state[j] = s * w_total + _dot(sb, m_state) + n_state
    for j in range(2 * n_pairs):
        s_ref[j] = state[j]


def _rwkv_scan(r, k, kk, v, lw0, lw1, ia0, ia1, ka, batch, seq_len, n_pairs=2):
    t, d = r.shape
    ns = seq_len // SCAN_BLOCK
    width = n_pairs * LANES
    fwd = pl.BlockSpec((SCAN_BLOCK, width), lambda b, h, s: (b * ns + s, h))
    bwd = pl.BlockSpec((SCAN_BLOCK, width), lambda b, h, s: (b * ns + ns - 1 - s, h))
    vec = pl.BlockSpec((1, width), lambda b, h, s: (0, h))
    return pl.pallas_call(
        functools.partial(_scan_kernel, n_pairs),
        grid=(batch, d // width, ns),
        in_specs=[fwd] * 6 + [bwd] * 6 + [vec],
        out_specs=[fwd, bwd],
        out_shape=[jax.ShapeDtypeStruct((t, d), F32)] * 2,
        scratch_shapes=[pltpu.VMEM((2 * n_pairs, LANES, LANES), F32)],
        compiler_params=_params("parallel", "parallel", "arbitrary"),
        name="rwkv_scan",
    )(r, k, kk, v, lw0, ia0, r, k, kk, v, lw1, ia1, ka)


def _rwkv_out_kernel(yf_ref, yb_ref, r_ref, k_ref, v_ref, g_ref, ia0_ref, ia1_ref, x_ref,
                     lg_ref, lb_ref, rk_ref, ka_ref, hs_ref, wo_ref, ng_ref, nb_ref, o_ref, z_ref):
    hsum = hs_ref[...]
    inv = 1.0 / RWKV_HEAD_DIM
    for gi in range(D_MODEL // LANES):
        cols = slice(gi * LANES, (gi + 1) * LANES)
        y = yf_ref[:, cols] + yb_ref[:, cols]
        yc = y - _dot_hilo(y, hsum) * inv
        var = _dot_hilo(yc * yc, hsum) * inv
        yn = yc * lax.rsqrt(var + GN_EPS) * lg_ref[:, cols] + lb_ref[:, cols]
        rr = r_ref[:, cols].astype(F32)
        kk = k_ref[:, cols].astype(F32)
        ia = ia0_ref[:, cols].astype(F32) + ia1_ref[:, cols].astype(F32)
        ksum = kk * (2.0 + (ia - 2.0) * ka_ref[:, cols])
        bonus = _dot_hilo(rr * ksum * rk_ref[:, cols], hsum) * v_ref[:, cols].astype(F32)
        z_ref[:, cols] = ((yn + bonus) * g_ref[:, cols].astype(F32)).astype(BF16)
    h = _dot(z_ref[...], wo_ref[...])
    o_ref[...] = _layer_norm(ALPHA * x_ref[...] + h, ng_ref[...], nb_ref[...])


def _rwkv_out(yf, yb, r, k, v, g, ia0, ia1, x, lnx_g, lnx_b, r_k, k_a, wo, ng, nb, tm=256):
    t, d = x.shape
    row = pl.BlockSpec((tm, d), lambda i: (i, 0))
    vec = pl.BlockSpec((1, d), lambda i: (0, 0))
    return pl.pallas_call(
        _rwkv_out_kernel,
        grid=(t // tm,),
        in_specs=[row] * 9 + [vec] * 4 + [pl.BlockSpec((LANES, LANES), lambda i: (0, 0)),
                                          pl.BlockSpec((d, d), lambda i: (0, 0)), vec, vec],
        out_specs=row,
        out_shape=jax.ShapeDtypeStruct((t, d), F32),
        scratch_shapes=[pltpu.VMEM((tm, d), BF16)],
        compiler_params=_params("parallel"),
        name="rwkv_out",
    )(yf, yb, r, k, v, g, ia0, ia1, x, lnx_g, lnx_b, r_k, k_a, _head_sum_matrix(RWKV_HEAD_DIM), wo, ng, nb)


REC_E1, REC_E2, REC_R1, REC_R2, REC_G1, REC_G2 = range(6)


def _router_kernel(x_ref, w_ref, b_ref, tri_ref, rec_ref, cnt_ref, run_ref):
    @pl.when(pl.program_id(0) == 0)
    def _():
        run_ref[...] = jnp.zeros_like(run_ref)

    xh, xl = _split(x_ref[...])
    wh = w_ref[0]
    wl = w_ref[1]
    logits = _dot(xh, wh) + (_dot(xh, wl) + _dot(xl, wh)) + b_ref[...]
    lane = lax.broadcasted_iota(jnp.int32, logits.shape, 1)
    logits = jnp.where(lane < N_EXPERTS, logits, -jnp.inf)
    m1 = jnp.max(logits, axis=-1, keepdims=True)
    i1 = jnp.min(jnp.where(logits == m1, lane, LANES), axis=-1, keepdims=True)
    rest = jnp.where(lane == i1, -jnp.inf, logits)
    m2 = jnp.max(rest, axis=-1, keepdims=True)
    i2 = jnp.min(jnp.where(rest == m2, lane, LANES), axis=-1, keepdims=True)
    e2 = jnp.exp(m2 - m1)
    g1 = 1.0 / (1.0 + e2)
    g2 = e2 / (1.0 + e2)

    hot1 = lane == i1
    hot2 = lane == i2
    oh1 = hot1.astype(BF16)
    oh2 = hot2.astype(BF16)
    tri = tri_ref[...]
    run = run_ref[...]
    n1 = jnp.sum(oh1.astype(F32), axis=0, keepdims=True)
    n2 = jnp.sum(oh2.astype(F32), axis=0, keepdims=True)
    before1 = run + _dot(tri, oh1)
    before2 = run + n1 + _dot(tri, oh2)
    r1 = jnp.sum(jnp.where(hot1, before1, 0.0), axis=-1, keepdims=True)
    r2 = jnp.sum(jnp.where(hot2, before2, 0.0), axis=-1, keepdims=True)
    run = run + n1 + n2
    run_ref[...] = run
    cnt_ref[...] = run

    rec = jnp.zeros(logits.shape, F32)
    for idx, val in ((REC_E1, i1.astype(F32)), (REC_E2, i2.astype(F32)), (REC_R1, r1), (REC_R2, r2),
                     (REC_G1, g1), (REC_G2, g2)):
        rec = jnp.where(lane == idx, val, rec)
    rec_ref[...] = rec


def _router(x, w_hilo, bias, tm=512):
    t, d = x.shape
    tri = jnp.asarray(np.tril(np.ones((tm, tm), np.float32), -1), BF16)
    return pl.pallas_call(
        _router_kernel,
        grid=(t // tm,),
        in_specs=[pl.BlockSpec((tm, d), lambda i: (i, 0)),
                  pl.BlockSpec((2, d, LANES), lambda i: (0, 0, 0)),
                  pl.BlockSpec((1, LANES), lambda i: (0, 0)),
                  pl.BlockSpec((tm, tm), lambda i: (0, 0))],
        out_specs=[pl.BlockSpec((tm, LANES), lambda i: (i, 0)),
                   pl.BlockSpec((1, LANES), lambda i: (0, 0))],
        out_shape=[jax.ShapeDtypeStruct((t, LANES), F32), jax.ShapeDtypeStruct((1, LANES), F32)],
        scratch_shapes=[pltpu.VMEM((1, LANES), F32)],
        compiler_params=_params("arbitrary"),
        name="moe_router",
    )(x, w_hilo, bias, tri)


EXPERT_TILE = 1024
ROUTE_TILE = 512


def _row_copy(src, src_row, dst, dst_row, sem):
    return pltpu.make_async_copy(src.at[pl.ds(src_row, 1)], dst.at[pl.ds(dst_row, 1)], sem)


def _dispatch_kernel(pos_ref, x_hbm, xs_in, xs_hbm, sem):
    del xs_in
    base = pl.program_id(0) * ROUTE_TILE

    def copies(t):
        return (_row_copy(x_hbm, base + t, xs_hbm, pos_ref[0, 0, t], sem),
                _row_copy(x_hbm, base + t, xs_hbm, pos_ref[0, 0, ROUTE_TILE + t], sem))

    def start(t, carry):
        for cp in copies(t):
            cp.start()
        return carry

    def wait(t, carry):
        for cp in copies(t):
            cp.wait()
        return carry

    lax.fori_loop(0, ROUTE_TILE, start, 0)
    lax.fori_loop(0, ROUTE_TILE, wait, 0)


def _moe_dispatch(x, pos, n_rows):
    t, d = x.shape
    return pl.pallas_call(
        _dispatch_kernel,
        grid=(t // ROUTE_TILE,),
        in_specs=[pl.BlockSpec((1, 1, 2 * ROUTE_TILE), lambda i: (i, 0, 0), memory_space=pltpu.SMEM),
                  pl.BlockSpec(memory_space=pl.ANY),
                  pl.BlockSpec(memory_space=pl.ANY)],
        out_specs=pl.BlockSpec(memory_space=pl.ANY),
        out_shape=jax.ShapeDtypeStruct((n_rows, d), F32),
        scratch_shapes=[pltpu.SemaphoreType.DMA(())],
        input_output_aliases={2: 0},
        compiler_params=_params("arbitrary"),
        name="moe_dispatch",
    )(pos, x, jnp.zeros((n_rows, d), F32))


def _experts_kernel(te_ref, na_ref, x_ref, wg_ref, wu_ref, wd_ref, o_ref, xb_ref, acc_ref):
    i = pl.program_id(0)
    f = pl.program_id(1)
    active = i < na_ref[0]

    @pl.when(active & (f == 0))
    def _():
        xb_ref[...] = x_ref[...].astype(BF16)
        acc_ref[...] = jnp.zeros_like(acc_ref)

    @pl.when(active)
    def _():
        xb = xb_ref[...]
        gate = _dot(xb, wg_ref[...])
        up = _dot(xb, wu_ref[...])
        h = (gate * _sigmoid(gate) * up).astype(BF16)
        acc_ref[...] += _dot(h, wd_ref[...])

    @pl.when(f == pl.num_programs(1) - 1)
    def _():
        o_ref[...] = jnp.where(active, acc_ref[...], 0.0)


def _moe_experts(xs, tile_expert, n_active, wg, wu, wd, tf=512):
    n_rows, d = xs.shape
    fe = wg.shape[2]
    row = pl.BlockSpec((EXPERT_TILE, d), lambda i, f, te, na: (i, 0))
    return pl.pallas_call(
        _experts_kernel,
        grid_spec=pltpu.PrefetchScalarGridSpec(
            num_scalar_prefetch=2,
            grid=(n_rows // EXPERT_TILE, fe // tf),
            in_specs=[row,
                      pl.BlockSpec((None, d, tf), lambda i, f, te, na: (te[i], 0, f)),
                      pl.BlockSpec((None, d, tf), lambda i, f, te, na: (te[i], 0, f)),
                      pl.BlockSpec((None, tf, d), lambda i, f, te, na: (te[i], f, 0))],
            out_specs=row,
            scratch_shapes=[pltpu.VMEM((EXPERT_TILE, d), BF16), pltpu.VMEM((EXPERT_TILE, d), F32)]),
        out_shape=jax.ShapeDtypeStruct((n_rows, d), F32),
        compiler_params=_params("parallel", "arbitrary"),
        name="moe_experts",
    )(tile_expert, n_active, xs, wg, wu, wd)


def _combine_kernel(pos_ref, ys_hbm, x_ref, rec_ref, g_ref, b_ref, o_ref, y1_ref, y2_ref, sem):
    def copies(t):
        return (_row_copy(ys_hbm, pos_ref[0, 0, t], y1_ref, t, sem),
                _row_copy(ys_hbm, pos_ref[0, 0, ROUTE_TILE + t], y2_ref, t, sem))

    def start(t, carry):
        for cp in copies(t):
            cp.start()
        return carry

    def wait(t, carry):
        for cp in copies(t):
            cp.wait()
        return carry

    lax.fori_loop(0, ROUTE_TILE, start, 0)
    lax.fori_loop(0, ROUTE_TILE, wait, 0)

    rec = rec_ref[...]
    lane = lax.broadcasted_iota(jnp.int32, rec.shape, 1)
    g1 = jnp.sum(jnp.where(lane == REC_G1, rec, 0.0), axis=-1, keepdims=True)
    g2 = jnp.sum(jnp.where(lane == REC_G2, rec, 0.0), axis=-1, keepdims=True)
    f = g1 * y1_ref[...] + g2 * y2_ref[...]
    o_ref[...] = _layer_norm(ALPHA * x_ref[...] + f, g_ref[...], b_ref[...])


def _moe_combine_ln(ys, pos, x, rec, g, b):
    t, d = x.shape
    row = pl.BlockSpec((ROUTE_TILE, d), lambda i: (i, 0))
    vec = pl.BlockSpec((1, d), lambda i: (0, 0))
    return pl.pallas_call(
        _combine_kernel,
        grid=(t // ROUTE_TILE,),
        in_specs=[pl.BlockSpec((1, 1, 2 * ROUTE_TILE), lambda i: (i, 0, 0), memory_space=pltpu.SMEM),
                  pl.BlockSpec(memory_space=pl.ANY),
                  row,
                  pl.BlockSpec((ROUTE_TILE, LANES), lambda i: (i, 0)),
                  vec, vec],
        out_specs=row,
        out_shape=jax.ShapeDtypeStruct((t, d), F32),
        scratch_shapes=[pltpu.VMEM((ROUTE_TILE, d), F32), pltpu.VMEM((ROUTE_TILE, d), F32),
                        pltpu.SemaphoreType.DMA(())],
        compiler_params=_params("arbitrary"),
        name="moe_combine_ln",
    )(pos, ys, x, rec, g, b)


def _moe_res_ln(x, rec, counts, wg, wu, wd, g, b):
    t, d = x.shape
    n_tiles = 2 * t // EXPERT_TILE + N_EXPERTS
    cnt = counts[0, :N_EXPERTS].astype(jnp.int32)
    padded = (cnt + EXPERT_TILE - 1) // EXPERT_TILE * EXPERT_TILE
    ends = jnp.cumsum(padded)
    starts = ends - padded
    e1 = rec[:, REC_E1].astype(jnp.int32)
    e2 = rec[:, REC_E2].astype(jnp.int32)
    pos1 = starts[e1] + rec[:, REC_R1].astype(jnp.int32)
    pos2 = starts[e2] + rec[:, REC_R2].astype(jnp.int32)
    pos = jnp.concatenate([pos1.reshape(-1, 1, ROUTE_TILE), pos2.reshape(-1, 1, ROUTE_TILE)], axis=2)
    tile_start = jnp.arange(n_tiles, dtype=jnp.int32) * EXPERT_TILE
    tile_expert = jnp.minimum(jnp.sum(tile_start[:, None] >= ends[None, :], axis=1), N_EXPERTS - 1).astype(jnp.int32)
    n_active = (ends[-1:] // EXPERT_TILE).astype(jnp.int32)
    last_expert = tile_expert[jnp.maximum(n_active[0] - 1, 0)]
    tile_expert = jnp.where(jnp.arange(n_tiles) < n_active[0], tile_expert, last_expert)

    xs = _moe_dispatch(x, pos, n_tiles * EXPERT_TILE)
    ys = _moe_experts(xs, tile_expert, n_active, wg, wu, wd)
    return _moe_combine_ln(ys, pos, x, rec, g, b)


def _prepare_weights(na_w_qkv, na_rpb, na_w_o, ffn_w_gate, ffn_w_up, ffn_w_down,
                     rwkv_mu, rwkv_w_rkv, rwkv_w0, rwkv_w1, rwkv_w2, rwkv_a0, rwkv_a1, rwkv_a2,
                     rwkv_g1, rwkv_g2, rwkv_k_k, rwkv_k_a, rwkv_r_k, rwkv_lnx_g, rwkv_lnx_b, rwkv_w_o,
                     moe_w_router, moe_b_router, moe_w_gate, moe_w_up, moe_w_down,
                     ln_mix_g, ln_mix_b, ln_ffn_g, ln_ffn_b):
    d = D_MODEL
    vec = lambda a: a.reshape(1, d).astype(F32)
    gl_pad = 2 * LANES - GATE_LORA
    router_w = jnp.pad(moe_w_router[0], ((0, 0), (0, LANES - N_EXPERTS)))
    router_hi = router_w.astype(BF16)
    router_lo = (router_w - router_hi.astype(F32)).astype(BF16)
    return dict(
        qkv_w=na_w_qkv[0].astype(BF16),
        qkv_scale=jnp.concatenate([jnp.full((1, d), NA_HEAD_DIM ** -0.5, F32), jnp.ones((1, 2 * d), F32)], axis=1),
        na_bias=_na_bias_table(na_rpb[0]),
        na_wo=na_w_o[0].astype(BF16),
        ffn_wg=ffn_w_gate[0].astype(BF16), ffn_wu=ffn_w_up[0].astype(BF16), ffn_wd=ffn_w_down[0].astype(BF16),
        mu=rwkv_mu[0].astype(F32),
        wrkv=rwkv_w_rkv[0].astype(BF16),
        w1=jnp.concatenate([rwkv_w1[0, 0], rwkv_w1[0, 1]], axis=1).astype(BF16),
        w2=jnp.concatenate([rwkv_w2[0, 0], rwkv_w2[0, 1]], axis=0).astype(BF16),
        w0=rwkv_w0[0].astype(F32),
        a1=jnp.concatenate([rwkv_a1[0, 0], rwkv_a1[0, 1]], axis=1).astype(BF16),
        a2=jnp.concatenate([rwkv_a2[0, 0], rwkv_a2[0, 1]], axis=0).astype(BF16),
        a0=rwkv_a0[0].astype(F32),
        g1=jnp.pad(rwkv_g1[0], ((0, 0), (0, gl_pad))).astype(BF16),
        g2=jnp.pad(rwkv_g2[0], ((0, gl_pad), (0, 0))).astype(BF16),
        k_k=vec(rwkv_k_k[0]), k_a=vec(rwkv_k_a[0]), r_k=vec(rwkv_r_k[0]),
        lnx_g=vec(rwkv_lnx_g[0]), lnx_b=vec(rwkv_lnx_b[0]),
        rwkv_wo=rwkv_w_o[0].astype(BF16),
        router_w=jnp.stack([router_hi, router_lo]),
        router_b=jnp.pad(moe_b_router[0], (0, LANES - N_EXPERTS)).reshape(1, LANES).astype(F32),
        moe_wg=moe_w_gate[0].astype(BF16), moe_wu=moe_w_up[0].astype(BF16), moe_wd=moe_w_down[0].astype(BF16),
        ln_mix_g=ln_mix_g.astype(F32), ln_mix_b=ln_mix_b.astype(F32),
        ln_ffn_g=ln_ffn_g.astype(F32), ln_ffn_b=ln_ffn_b.astype(F32),
    )


def _trunk(x3, w):
    batch, seq_len, d = x3.shape
    rows = seq_len // GRID_W
    x = x3.reshape(batch * seq_len, d)
    ln = lambda a, i: a[i].reshape(1, d)

    qkv = _qkv_proj(x, w["qkv_w"], w["qkv_scale"])
    att = _na_attention(qkv, w["na_bias"], batch, rows)
    x = _proj_res_ln(att, w["na_wo"], x, ln(w["ln_mix_g"], 0), ln(w["ln_mix_b"], 0))
    x = _ffn_res_ln(x, w["ffn_wg"], w["ffn_wu"], w["ffn_wd"], ln(w["ln_ffn_g"], 0), ln(w["ln_ffn_b"], 0))

    r, k, kk, v, g, lw0, lw1, ia0, ia1 = _rwkv_proj(x, seq_len, w["mu"], w["wrkv"], w["w1"], w["w2"], w["w0"],
                                                    w["a1"], w["a2"], w["a0"], w["g1"], w["g2"], w["k_k"])
    yf, yb = _rwkv_scan(r, k, kk, v, lw0, lw1, ia0, ia1, w["k_a"], batch, seq_len)
    x = _rwkv_out(yf, yb, r, k, v, g, ia0, ia1, x, w["lnx_g"], w["lnx_b"], w["r_k"], w["k_a"], w["rwkv_wo"],
                  ln(w["ln_mix_g"], 1), ln(w["ln_mix_b"], 1))
    rec, counts = _router(x, w["router_w"], w["router_b"])
    x = _moe_res_ln(x, rec, counts, w["moe_wg"], w["moe_wu"], w["moe_wd"], ln(w["ln_ffn_g"], 1), ln(w["ln_ffn_b"], 1))
    return x.reshape(batch, seq_len, d)


def kernel(x_prompt, x_sample, na_w_qkv, na_rpb, na_w_o, ffn_w_gate, ffn_w_up, ffn_w_down, rwkv_mu, rwkv_w_rkv, rwkv_w0, rwkv_w1, rwkv_w2, rwkv_a0, rwkv_a1, rwkv_a2, rwkv_g1, rwkv_g2, rwkv_k_k, rwkv_k_a, rwkv_r_k, rwkv_lnx_g, rwkv_lnx_b, rwkv_w_o, moe_w_router, moe_b_router, moe_w_gate, moe_w_up, moe_w_down, ln_mix_g, ln_mix_b, ln_ffn_g, ln_ffn_b):
    w = _prepare_weights(na_w_qkv, na_rpb, na_w_o, ffn_w_gate, ffn_w_up, ffn_w_down,
                         rwkv_mu, rwkv_w_rkv, rwkv_w0, rwkv_w1, rwkv_w2, rwkv_a0, rwkv_a1, rwkv_a2,
                         rwkv_g1, rwkv_g2, rwkv_k_k, rwkv_k_a, rwkv_r_k, rwkv_lnx_g, rwkv_lnx_b, rwkv_w_o,
                         moe_w_router, moe_b_router, moe_w_gate, moe_w_up, moe_w_down,
                         ln_mix_g, ln_mix_b, ln_ffn_g, ln_ffn_b)
    return (_trunk(x_prompt, w), _trunk(x_sample, w))
```

```python
import functools
import math

import numpy as np
import jax
import jax.numpy as jnp
from jax import lax
from jax.experimental import pallas as pl
from jax.experimental.pallas import tpu as pltpu

D_MODEL = 1024
DEPTH = 2
GRID_W = 64
NA_HEAD_DIM = 32
NA_HEADS = D_MODEL // NA_HEAD_DIM
NA_ROWS = 8
NA_COLS = 16
RWKV_HEAD_DIM = 64
DECAY_LORA = 64
ICLR_LORA = 64
GATE_LORA = 160
GN_EPS = 64e-5
D_FF = 2816
N_EXPERTS = 8
D_EXPERT = 3584
LN_EPS = 1e-5
ALPHA = (2 * DEPTH) ** 0.25

LANES = 128
SCAN_CHUNK = 64
SCAN_BLOCK = 256
VMEM_LIMIT = 56 * 1024 * 1024

F32 = jnp.float32
BF16 = jnp.bfloat16


def _params(*sem):
    return pltpu.CompilerParams(dimension_semantics=sem, vmem_limit_bytes=VMEM_LIMIT)


def _dot(a, b):
    return jnp.dot(a, b, preferred_element_type=F32)


def _dot_nt(a, b):
    return lax.dot_general(a, b, (((1,), (1,)), ((), ())), preferred_element_type=F32)


def _dot_tn(a, b):
    return lax.dot_general(a, b, (((0,), (0,)), ((), ())), preferred_element_type=F32)


def _split(a):
    hi = a.astype(BF16)
    lo = (a - hi.astype(F32)).astype(BF16)
    return hi, lo


def _dot_hilo(a, b_exact):
    hi, lo = _split(a)
    return _dot(hi, b_exact) + _dot(lo, b_exact)


def _dot_hilo_lhs_exact(a_exact, b):
    hi, lo = _split(b)
    return _dot(a_exact, hi) + _dot(a_exact, lo)


def _layer_norm(z, g, b):
    mu = jnp.mean(z, axis=-1, keepdims=True)
    zc = z - mu
    var = jnp.mean(zc * zc, axis=-1, keepdims=True)
    return zc * lax.rsqrt(var + LN_EPS) * g + b


def _sigmoid(z):
    return 1.0 / (1.0 + jnp.exp(-z))


def _head_sum_matrix(head_dim):
    lane = np.arange(LANES)
    return jnp.asarray((lane[:, None] // head_dim) == (lane[None, :] // head_dim), BF16)


def _qkv_kernel(x_ref, w_ref, s_ref, o_ref):
    acc = _dot(x_ref[...].astype(BF16), w_ref[...])
    o_ref[...] = (acc * s_ref[...]).astype(o_ref.dtype)


def _qkv_proj(x, w, colscale, tm=512):
    t, d = x.shape
    n = w.shape[1]
    return pl.pallas_call(
        _qkv_kernel,
        grid=(t // tm,),
        in_specs=[pl.BlockSpec((tm, d), lambda i: (i, 0)),
                  pl.BlockSpec((d, n), lambda i: (0, 0)),
                  pl.BlockSpec((1, n), lambda i: (0, 0))],
        out_specs=pl.BlockSpec((tm, n), lambda i: (i, 0)),
        out_shape=jax.ShapeDtypeStruct((t, n), BF16),
        compiler_params=_params("parallel"),
        name="qkv_proj",
    )(x, w, colscale)


NA_HEADS_PER_GROUP = LANES // NA_HEAD_DIM
NA_GROUPS = D_MODEL // LANES
NA_WIN = NA_ROWS * GRID_W


NA_GROUP_ROWS = NA_HEADS_PER_GROUP * GRID_W
NA_ROW_PAIRS = NA_ROWS // 2
NA_BIAS_PAIRS = 2 * NA_ROWS - 2


def _na_kernel(rows, q_ref, k_ref, v_ref, b_ref, o_ref):
    r = pl.program_id(1)
    off = r - jnp.clip(r - NA_ROWS // 2, 0, rows - NA_ROWS)
    lane_head = lax.broadcasted_iota(jnp.int32, (GRID_W, LANES), 1) // NA_HEAD_DIM
    groups = [slice(g * LANES, (g + 1) * LANES) for g in range(NA_GROUPS)]

    scores = []
    for cols in groups:
        qg = q_ref[:, cols]
        zero = jnp.zeros_like(qg)
        qm = jnp.concatenate([jnp.where(lane_head == h, qg, zero) for h in range(NA_HEADS_PER_GROUP)], axis=0)
        scores.append(_dot_nt(qm, k_ref[:, cols]))

    probs, denoms = [], []
    for g, s in enumerate(scores):
        hq = pl.ds(g * NA_GROUP_ROWS, NA_GROUP_ROWS)
        s = jnp.concatenate([s[:, m * LANES:(m + 1) * LANES] + b_ref[2 * m - off + NA_ROWS - 1, hq, :]
                             for m in range(NA_ROW_PAIRS)], axis=1)
        p = jnp.exp(s - jnp.max(s, axis=-1, keepdims=True))
        denoms.append(jnp.sum(p, axis=-1, keepdims=True))
        probs.append(p.astype(BF16))

    for g, cols in enumerate(groups):
        o4 = _dot(probs[g], v_ref[:, cols]) / denoms[g]
        og = jnp.zeros((GRID_W, LANES), F32)
        for h in range(NA_HEADS_PER_GROUP):
            og = og + jnp.where(lane_head == h, o4[h * GRID_W:(h + 1) * GRID_W], 0.0)
        o_ref[:, cols] = og.astype(o_ref.dtype)


def _na_attention(qkv, bias, batch, rows):
    t = qkv.shape[0]
    half = NA_ROWS // 2

    def win_start(r):
        return jnp.clip(r - half, 0, rows - NA_ROWS)

    def q_map(b, r):
        return (b * rows + r, 0)

    def k_map(b, r):
        return ((b * rows + win_start(r)) * GRID_W, D_MODEL)

    def v_map(b, r):
        return ((b * rows + win_start(r)) * GRID_W, 2 * D_MODEL)

    return pl.pallas_call(
        functools.partial(_na_kernel, rows),
        grid=(batch, rows),
        in_specs=[pl.BlockSpec((GRID_W, D_MODEL), q_map),
                  pl.BlockSpec((pl.Element(NA_WIN), pl.Element(D_MODEL)), k_map),
                  pl.BlockSpec((pl.Element(NA_WIN), pl.Element(D_MODEL)), v_map),
                  pl.BlockSpec(bias.shape, lambda b, r: (0, 0, 0), pipeline_mode=pl.Buffered(1))],
        out_specs=pl.BlockSpec((GRID_W, D_MODEL), q_map),
        out_shape=jax.ShapeDtypeStruct((t, D_MODEL), BF16),
        compiler_params=_params("parallel", "arbitrary"),
        name="na_attention",
    )(qkv, qkv, qkv, bias)


def _na_bias_kernel(rpb_ref, oh_ref, mask_ref, o_ref):
    x = rpb_ref[...]
    hi = x.astype(BF16)
    r1 = x - hi.astype(F32)
    mid = r1.astype(BF16)
    lo = (r1 - mid.astype(F32)).astype(BF16)
    oh = oh_ref[...]
    o_ref[...] = _dot(hi, oh) + _dot(mid, oh) + _dot(lo, oh) + mask_ref[...]


def _na_bias_table(rpb, tn=1024):
    n_dr, n_dc = 2 * NA_ROWS - 1, 2 * NA_COLS - 1
    c = np.arange(GRID_W)
    q_start = np.clip(c - NA_COLS // 2, 0, GRID_W - NA_COLS)
    kc = np.arange(GRID_W)
    valid = (kc[None, :] >= q_start[:, None]) & (kc[None, :] < q_start[:, None] + NA_COLS)
    dc = kc[None, :] - c[:, None] + NA_COLS - 1
    onehot = (np.arange(LANES)[:, None, None] == dc[None]) & valid[None]
    onehot = jnp.asarray(onehot.reshape(LANES, GRID_W * GRID_W), BF16)
    mask = jnp.asarray(np.where(valid, 0.0, -np.inf).reshape(1, GRID_W * GRID_W), F32)
    rpb2 = jnp.pad(rpb.reshape(NA_HEADS * n_dr, n_dc).astype(F32), ((0, 0), (0, LANES - n_dc)))
    nrow, ncol = rpb2.shape[0], GRID_W * GRID_W
    flat = pl.pallas_call(
        _na_bias_kernel,
        grid=(ncol // tn,),
        in_specs=[pl.BlockSpec((nrow, LANES), lambda j: (0, 0)),
                  pl.BlockSpec((LANES, tn), lambda j: (0, j)),
                  pl.BlockSpec((1, tn), lambda j: (0, j))],
        out_specs=pl.BlockSpec((nrow, tn), lambda j: (0, j)),
        out_shape=jax.ShapeDtypeStruct((nrow, ncol), F32),
        compiler_params=_params("parallel"),
        name="na_bias_table",
    )(rpb2, onehot, mask)
    toe = flat.reshape(NA_HEADS, n_dr, GRID_W, GRID_W)
    pairs = jnp.stack([toe[:, 0:NA_BIAS_PAIRS], toe[:, 1:NA_BIAS_PAIRS + 1]], axis=3)
    return jnp.transpose(pairs, (1, 0, 2, 3, 4)).reshape(NA_BIAS_PAIRS, NA_HEADS * GRID_W, LANES)


def _proj_ln_kernel(a_ref, w_ref, x_ref, g_ref, b_ref, o_ref):
    h = _dot(a_ref[...], w_ref[...])
    o_ref[...] = _layer_norm(ALPHA * x_ref[...] + h, g_ref[...], b_ref[...])


def _proj_res_ln(a, w, x, g, b, tm=512):
    t, d = x.shape
    row = pl.BlockSpec((tm, d), lambda i: (i, 0))
    vec = pl.BlockSpec((1, d), lambda i: (0, 0))
    return pl.pallas_call(
        _proj_ln_kernel,
        grid=(t // tm,),
        in_specs=[row, pl.BlockSpec((d, d), lambda i: (0, 0)), row, vec, vec],
        out_specs=row,
        out_shape=jax.ShapeDtypeStruct((t, d), F32),
        compiler_params=_params("parallel"),
        name="proj_res_ln",
    )(a, w, x, g, b)


def _ffn_kernel(x_ref, wg_ref, wu_ref, wd_ref, g_ref, b_ref, o_ref, xb_ref, acc_ref):
    f = pl.program_id(1)

    @pl.when(f == 0)
    def _():
        xb_ref[...] = x_ref[...].astype(BF16)
        acc_ref[...] = jnp.zeros_like(acc_ref)

    xb = xb_ref[...]
    gate = _dot(xb, wg_ref[...])
    up = _dot(xb, wu_ref[...])
    h = (gate * _sigmoid(gate) * up).astype(BF16)
    acc_ref[...] += _dot(h, wd_ref[...])

    @pl.when(f == pl.num_programs(1) - 1)
    def _():
        o_ref[...] = _layer_norm(ALPHA * x_ref[...] + acc_ref[...], g_ref[...], b_ref[...])


def _ffn_res_ln(x, wg, wu, wd, g, b, tm=512, tf=1408):
    t, d = x.shape
    ff = wg.shape[1]
    row = pl.BlockSpec((tm, d), lambda i, f: (i, 0))
    vec = pl.BlockSpec((1, d), lambda i, f: (0, 0))
    return pl.pallas_call(
        _ffn_kernel,
        grid=(t // tm, ff // tf),
        in_specs=[row,
                  pl.BlockSpec((d, tf), lambda i, f: (0, f)),
                  pl.BlockSpec((d, tf), lambda i, f: (0, f)),
                  pl.BlockSpec((tf, d), lambda i, f: (f, 0)),
                  vec, vec],
        out_specs=row,
        out_shape=jax.ShapeDtypeStruct((t, d), F32),
        scratch_shapes=[pltpu.VMEM((tm, d), BF16), pltpu.VMEM((tm, d), F32)],
        compiler_params=_params("parallel", "arbitrary"),
        name="ffn_res_ln",
    )(x, wg, wu, wd, g, b)


HALO = 8


def _rwkv_proj_kernel(seq_len, x_ref, xp_ref, xn_ref, mu_ref, wrkv_ref, w1_ref, w2_ref, w0_ref,
                      a1_ref, a2_ref, a0_ref, g1_ref, g2_ref, kscale_ref, hs_ref,
                      r_ref, k_ref, kk_ref, v_ref, g_ref, lw0_ref, lw1_ref, ia0_ref, ia1_ref):
    tm = x_ref.shape[0]
    i = pl.program_id(0)
    x = x_ref[...]
    row = lax.broadcasted_iota(jnp.int32, x.shape, 0)
    first_in_seq = (i * tm) % seq_len == 0
    last_in_seq = ((i + 1) * tm) % seq_len == 0
    prev_row = jnp.where(first_in_seq, 0.0, xp_ref[HALO - 1:HALO, :])
    next_row = jnp.where(last_in_seq, 0.0, xn_ref[0:1, :])
    x_prev = jnp.where(row == 0, prev_row, pltpu.roll(x, 1, axis=0))
    x_next = jnp.where(row == tm - 1, next_row, pltpu.roll(x, tm - 1, axis=0))
    xx = 0.5 * (x_prev + x_next) - x

    def mix(j):
        return (x + xx * mu_ref[j:j + 1, :]).astype(BF16)

    r_ref[...] = _dot(mix(0), wrkv_ref[0]).astype(r_ref.dtype)
    kf = _dot(mix(2), wrkv_ref[1])
    k_ref[...] = kf.astype(k_ref.dtype)
    v_ref[...] = _dot(mix(3), wrkv_ref[2]).astype(v_ref.dtype)
    hsum = hs_ref[...]
    for gi in range(D_MODEL // LANES):
        cols = slice(gi * LANES, (gi + 1) * LANES)
        kkr = kf[:, cols] * kscale_ref[:, cols]
        norm = jnp.sqrt(_dot_hilo(kkr * kkr, hsum))
        kk_ref[:, cols] = (kkr / jnp.maximum(norm, 1e-12)).astype(kk_ref.dtype)

    lane = lax.broadcasted_iota(jnp.int32, (tm, LANES), 1)
    dir0 = lane < DECAY_LORA

    lo = jnp.tanh(_dot(mix(1), w1_ref[...]))
    for z, out in enumerate((lw0_ref, lw1_ref)):
        lz = jnp.where(dir0 if z == 0 else ~dir0, lo, 0.0).astype(BF16)
        wl = w0_ref[z:z + 1, :] + _dot(lz, w2_ref[...])
        out[...] = (-math.exp(-0.5)) * _sigmoid(wl)

    al = _dot(mix(4), a1_ref[...])
    for z, out in enumerate((ia0_ref, ia1_ref)):
        az = jnp.where(dir0 if z == 0 else ~dir0, al, 0.0).astype(BF16)
        out[...] = _sigmoid(a0_ref[z:z + 1, :] + _dot(az, a2_ref[...])).astype(out.dtype)

    gl = _sigmoid(_dot(mix(5), g1_ref[...])).astype(BF16)
    g_ref[...] = _dot(gl, g2_ref[...]).astype(g_ref.dtype)


def _rwkv_proj(x, seq_len, mu, wrkv, w1, w2, w0, a1, a2, a0, g1, g2, kscale, tm=256):
    t, d = x.shape
    nb = t // HALO
    step = tm // HALO
    row = pl.BlockSpec((tm, d), lambda i: (i, 0))
    full = lambda shape: pl.BlockSpec(shape, lambda i: (0,) * len(shape))
    hsum = _head_sum_matrix(RWKV_HEAD_DIM)
    outs = [jax.ShapeDtypeStruct((t, d), BF16)] * 5 + [jax.ShapeDtypeStruct((t, d), F32)] * 2 + \
           [jax.ShapeDtypeStruct((t, d), BF16)] * 2
    return pl.pallas_call(
        functools.partial(_rwkv_proj_kernel, seq_len),
        grid=(t // tm,),
        in_specs=[row,
                  pl.BlockSpec((HALO, d), lambda i: (jnp.maximum(i * step - 1, 0), 0)),
                  pl.BlockSpec((HALO, d), lambda i: (jnp.minimum((i + 1) * step, nb - 1), 0)),
                  full(mu.shape), full(wrkv.shape), full(w1.shape), full(w2.shape), full(w0.shape),
                  full(a1.shape), full(a2.shape), full(a0.shape), full(g1.shape), full(g2.shape),
                  full(kscale.shape), full(hsum.shape)],
        out_specs=[row] * 9,
        out_shape=outs,
        compiler_params=_params("parallel"),
        name="rwkv_proj",
    )(x, x, x, mu, wrkv, w1, w2, w0, a1, a2, a0, g1, g2, kscale, hsum)


def _pair_blockdiag(y, lane_a):
    zero = jnp.zeros_like(y)
    return jnp.concatenate([jnp.where(lane_a, y, zero), jnp.where(lane_a, zero, y)], axis=0)


def _pair_matmul(x, y, lane_a):
    return _dot(x.astype(BF16), _pair_blockdiag(y.astype(BF16), lane_a))


def _scan_prepare(chains):
    c = SCAN_CHUNK
    t_idx = lax.broadcasted_iota(jnp.int32, (c, LANES), 0)
    s_idx = lax.broadcasted_iota(jnp.int32, (c, LANES), 1) % c
    lane_a = lax.broadcasted_iota(jnp.int32, (c, LANES), 1) < RWKV_HEAD_DIM
    eye = (s_idx == t_idx).astype(F32)
    tri_r = lax.broadcasted_iota(jnp.int32, (c, c), 0)
    tri_c = lax.broadcasted_iota(jnp.int32, (c, c), 1)
    bd = (lax.broadcasted_iota(jnp.int32, (LANES, LANES), 0) // RWKV_HEAD_DIM ==
          lax.broadcasted_iota(jnp.int32, (LANES, LANES), 1) // RWKV_HEAD_DIM)
    before = {rev: ((s_idx > t_idx) if rev else (s_idx < t_idx)) for rev in (False, True)}
    upto = {rev: before[rev] | (s_idx == t_idx) for rev in (False, True)}
    cum = {rev: ((tri_c >= tri_r) if rev else (tri_c <= tri_r)).astype(BF16) for rev in (False, True)}
    n = len(chains)

    lwc = [_dot_hilo_lhs_exact(cum[ch[0]], ch[6]) for ch in chains]
    vals = [ch[5] for ch in chains]

    at, rt, bh, kh, w_tot, lhs, rhs = [], [], [], [], [], [], []
    for (rev, ka, r, k, kk, v, lw, ia), lc in zip(chains, lwc):
        last = lc[0:1, :] if rev else lc[c - 1:c, :]
        w_in = jnp.exp(lc)
        w_ex = jnp.exp(lc - lw)
        w_inv = jnp.exp(-lc)
        w_end = jnp.exp(last)
        w_out = w_end * w_inv
        kf = k.astype(F32)
        kkf = kk.astype(F32)
        iaf = ia.astype(F32)
        kd = kf * (1.0 + (iaf - 1.0) * ka)
        bvec = kkf * iaf
        a_t = (-kkf) * w_ex
        r_t = r.astype(F32) * w_in
        at.append(a_t)
        rt.append(r_t)
        bh.append((bvec * w_out).astype(BF16))
        kh.append((kd * w_out).astype(BF16))
        w_tot.append(w_end)
        lhs.append(jnp.concatenate([a_t, r_t], axis=0).astype(BF16))
        rhs.append(jnp.concatenate([_pair_blockdiag((bvec * w_inv).astype(BF16), lane_a),
                                    _pair_blockdiag((kd * w_inv).astype(BF16), lane_a)], axis=0))

    gram = [_dot_nt(lhs[i], rhs[i]) for i in range(n)]
    a_ab = [jnp.where(before[chains[i][0]], gram[i][0:c, 0:LANES], 0.0) for i in range(n)]
    a_ak = [jnp.where(before[chains[i][0]], gram[i][0:c, LANES:2 * LANES], 0.0) for i in range(n)]
    a_rb = [jnp.where(upto[chains[i][0]], gram[i][c:2 * c, 0:LANES], 0.0) for i in range(n)]
    a_rk = [jnp.where(upto[chains[i][0]], gram[i][c:2 * c, LANES:2 * LANES], 0.0) for i in range(n)]

    tinv = [eye + a for a in a_ab]
    pw = a_ab
    for _ in range(int(math.log2(c)) - 1):
        pw = [_pair_matmul(p, p, lane_a) for p in pw]
        tinv = [tinv[i] + _pair_matmul(tinv[i], pw[i], lane_a) for i in range(n)]

    av = [_pair_matmul(a_ak[i], vals[i], lane_a) for i in range(n)]
    y_loc = [_pair_matmul(a_rk[i], vals[i], lane_a) for i in range(n)]
    pq = [_dot(tinv[i].astype(BF16),
               jnp.concatenate([_pair_blockdiag(at[i].astype(BF16), lane_a),
                                _pair_blockdiag(av[i].astype(BF16), lane_a)], axis=1)) for i in range(n)]
    pqb = [x.astype(BF16) for x in pq]
    mn = [_dot_tn(pqb[i], bh[i]) for i in range(n)]
    kv = [_dot_tn(vals[i], kh[i]) for i in range(n)]
    arb = [_dot(a_rb[i].astype(BF16),
                jnp.concatenate([_pair_blockdiag(pqb[i][:, 0:LANES], lane_a),
                                 _pair_blockdiag(pqb[i][:, LANES:2 * LANES], lane_a)], axis=1)) for i in range(n)]
    out = []
    for i in range(n):
        ry = (rt[i] + arb[i][:, 0:LANES]).astype(BF16)
        yc = arb[i][:, LANES:2 * LANES] + y_loc[i]
        m_state = jnp.where(bd, mn[i][0:LANES], 0.0).astype(BF16)
        n_state = jnp.where(bd, mn[i][LANES:2 * LANES] + kv[i], 0.0)
        out.append((ry, yc, m_state, n_state, w_tot[i]))
    return out


def _scan_kernel(n_pairs, rf_ref, kf_ref, kkf_ref, vf_ref, lwf_ref, iaf_ref,
                 rb_ref, kb_ref, kkb_ref, vb_ref, lwb_ref, iab_ref, ka_ref, yf_ref, yb_ref, s_ref):
    @pl.when(pl.program_id(2) == 0)
    def _():
        s_ref[...] = jnp.zeros_like(s_ref)

    c = SCAN_CHUNK
    n_chunks = SCAN_BLOCK // c
    dirs = ((False, (rf_ref, kf_ref, kkf_ref, vf_ref, lwf_ref, iaf_ref), yf_ref),
            (True, (rb_ref, kb_ref, kkb_ref, vb_ref, lwb_ref, iab_ref), yb_ref))
    chains, where = [], []
    for step in range(n_chunks):
        for p in range(n_pairs):
            cols = slice(p * LANES, (p + 1) * LANES)
            for di, (rev, refs, _) in enumerate(dirs):
                ci = n_chunks - 1 - step if rev else step
                rows = slice(ci * c, (ci + 1) * c)
                chains.append((rev, ka_ref[:, cols]) + tuple(ref[rows, cols] for ref in refs))
                where.append((2 * p + di, di, rows, cols))
    pre = _scan_prepare(chains)
    state = [s_ref[j] for j in range(2 * n_pairs)]
    for (j, di, rows, cols), (ry, yc, m_state, n_state, w_total) in zip(where, pre):
        s = state[j]
        sb = s.astype(BF16)
        dirs[di][2][rows, cols] = _dot_nt(ry, sb) + yc
        state[j] = s * w_total + _dot(sb, m_state) + n_state
    for j in range(2 * n_pairs):
        s_ref[j] = state[j]


def _rwkv_scan(r, k, kk, v, lw0, lw1, ia0, ia1, ka, batch, seq_len, n_pairs=2):
    t, d = r.shape
    ns = seq_len // SCAN_BLOCK
    width = n_pairs * LANES
    fwd = pl.BlockSpec((SCAN_BLOCK, width), lambda b, h, s: (b * ns + s, h))
    bwd = pl.BlockSpec((SCAN_BLOCK, width), lambda b, h, s: (b * ns + ns - 1 - s, h))
    vec = pl.BlockSpec((1, width), lambda b, h, s: (0, h))
    return pl.pallas_call(
        functools.partial(_scan_kernel, n_pairs),
        grid=(batch, d // width, ns),
        in_specs=[fwd] * 6 + [bwd] * 6 + [vec],
        out_specs=[fwd, bwd],
        out_shape=[jax.ShapeDtypeStruct((t, d), F32)] * 2,
        scratch_shapes=[pltpu.VMEM((2 * n_pairs, LANES, LANES), F32)],
        compiler_params=_params("parallel", "parallel", "arbitrary"),
        name="rwkv_scan",
    )(r, k, kk, v, lw0, ia0, r, k, kk, v, lw1, ia1, ka)


def _rwkv_out_kernel(yf_ref, yb_ref, r_ref, k_ref, v_ref, g_ref, ia0_ref, ia1_ref, x_ref,
                     lg_ref, lb_ref, rk_ref, ka_ref, hs_ref, wo_ref, ng_ref, nb_ref, o_ref, z_ref):
    hsum = hs_ref[...]
    inv = 1.0 / RWKV_HEAD_DIM
    for gi in range(D_MODEL // LANES):
        cols = slice(gi * LANES, (gi + 1) * LANES)
        y = yf_ref[:, cols] + yb_ref[:, cols]
        yc = y - _dot_hilo(y, hsum) * inv
        var = _dot_hilo(yc * yc, hsum) * inv
        yn = yc * lax.rsqrt(var + GN_EPS) * lg_ref[:, cols] + lb_ref[:, cols]
        rr = r_ref[:, cols].astype(F32)
        kk = k_ref[:, cols].astype(F32)
        ia = ia0_ref[:, cols].astype(F32) + ia1_ref[:, cols].astype(F32)
        ksum = kk * (2.0 + (ia - 2.0) * ka_ref[:, cols])
        bonus = _dot_hilo(rr * ksum * rk_ref[:, cols], hsum) * v_ref[:, cols].astype(F32)
        z_ref[:, cols] = ((yn + bonus) * g_ref[:, cols].astype(F32)).astype(BF16)
    h = _dot(z_ref[...], wo_ref[...])
    o_ref[...] = _layer_norm(ALPHA * x_ref[...] + h, ng_ref[...], nb_ref[...])


def _rwkv_out(yf, yb, r, k, v, g, ia0, ia1, x, lnx_g, lnx_b, r_k, k_a, wo, ng, nb, tm=256):
    t, d = x.shape
    row = pl.BlockSpec((tm, d), lambda i: (i, 0))
    vec = pl.BlockSpec((1, d), lambda i: (0, 0))
    return pl.pallas_call(
        _rwkv_out_kernel,
        grid=(t // tm,),
        in_specs=[row] * 9 + [vec] * 4 + [pl.BlockSpec((LANES, LANES), lambda i: (0, 0)),
                                          pl.BlockSpec((d, d), lambda i: (0, 0)), vec, vec],
        out_specs=row,
        out_shape=jax.ShapeDtypeStruct((t, d), F32),
        scratch_shapes=[pltpu.VMEM((tm, d), BF16)],
        compiler_params=_params("parallel"),
        name="rwkv_out",
    )(yf, yb, r, k, v, g, ia0, ia1, x, lnx_g, lnx_b, r_k, k_a, _head_sum_matrix(RWKV_HEAD_DIM), wo, ng, nb)


REC_E1, REC_E2, REC_R1, REC_R2, REC_G1, REC_G2 = range(6)


def _router_kernel(x_ref, w_ref, b_ref, tri_ref, rec_ref, cnt_ref, run_ref):
    @pl.when(pl.program_id(0) == 0)
    def _():
        run_ref[...] = jnp.zeros_like(run_ref)

    xh, xl = _split(x_ref[...])
    wh = w_ref[0]
    wl = w_ref[1]
    logits = _dot(xh, wh) + (_dot(xh, wl) + _dot(xl, wh)) + b_ref[...]
    lane = lax.broadcasted_iota(jnp.int32, logits.shape, 1)
    logits = jnp.where(lane < N_EXPERTS, logits, -jnp.inf)
    m1 = jnp.max(logits, axis=-1, keepdims=True)
    i1 = jnp.min(jnp.where(logits == m1, lane, LANES), axis=-1, keepdims=True)
    rest = jnp.where(lane == i1, -jnp.inf, logits)
    m2 = jnp.max(rest, axis=-1, keepdims=True)
    i2 = jnp.min(jnp.where(rest == m2, lane, LANES), axis=-1, keepdims=True)
    e2 = jnp.exp(m2 - m1)
    g1 = 1.0 / (1.0 + e2)
    g2 = e2 / (1.0 + e2)

    hot1 = lane == i1
    hot2 = lane == i2
    oh1 = hot1.astype(BF16)
    oh2 = hot2.astype(BF16)
    tri = tri_ref[...]
    run = run_ref[...]
    n1 = jnp.sum(oh1.astype(F32), axis=0, keepdims=True)
    n2 = jnp.sum(oh2.astype(F32), axis=0, keepdims=True)
    before1 = run + _dot(tri, oh1)
    before2 = run + n1 + _dot(tri, oh2)
    r1 = jnp.sum(jnp.where(hot1, before1, 0.0), axis=-1, keepdims=True)
    r2 = jnp.sum(jnp.where(hot2, before2, 0.0), axis=-1, keepdims=True)
    run = run + n1 + n2
    run_ref[...] = run
    cnt_ref[...] = run

    rec = jnp.zeros(logits.shape, F32)
    for idx, val in ((REC_E1, i1.astype(F32)), (REC_E2, i2.astype(F32)), (REC_R1, r1), (REC_R2, r2),
                     (REC_G1, g1), (REC_G2, g2)):
        rec = jnp.where(lane == idx, val, rec)
    rec_ref[...] = rec


def _router(x, w_hilo, bias, tm=512):
    t, d = x.shape
    tri = jnp.asarray(np.tril(np.ones((tm, tm), np.float32), -1), BF16)
    return pl.pallas_call(
        _router_kernel,
        grid=(t // tm,),
        in_specs=[pl.BlockSpec((tm, d), lambda i: (i, 0)),
                  pl.BlockSpec((2, d, LANES), lambda i: (0, 0, 0)),
                  pl.BlockSpec((1, LANES), lambda i: (0, 0)),
                  pl.BlockSpec((tm, tm), lambda i: (0, 0))],
        out_specs=[pl.BlockSpec((tm, LANES), lambda i: (i, 0)),
                   pl.BlockSpec((1, LANES), lambda i: (0, 0))],
        out_shape=[jax.ShapeDtypeStruct((t, LANES), F32), jax.ShapeDtypeStruct((1, LANES), F32)],
        scratch_shapes=[pltpu.VMEM((1, LANES), F32)],
        compiler_params=_params("arbitrary"),
        name="moe_router",
    )(x, w_hilo, bias, tri)


EXPERT_TILE = 1024
ROUTE_TILE = 512


def _row_copy(src, src_row, dst, dst_row, sem):
    return pltpu.make_async_copy(src.at[pl.ds(src_row, 1)], dst.at[pl.ds(dst_row, 1)], sem)


def _dispatch_kernel(pos_ref, x_ref, xs_in, xs_hbm, sem):
    del xs_in

    def copies(t):
        return (_row_copy(x_ref, t, xs_hbm, pos_ref[0, 0, t], sem),
                _row_copy(x_ref, t, xs_hbm, pos_ref[0, 0, ROUTE_TILE + t], sem))

    def start(t, carry):
        for cp in copies(t):
            cp.start()
        return carry

    def wait(t, carry):
        for cp in copies(t):
            cp.wait()
        return carry

    lax.fori_loop(0, ROUTE_TILE, start, 0)
    lax.fori_loop(0, ROUTE_TILE, wait, 0)


def _moe_dispatch(x, pos, n_rows):
    t, d = x.shape
    return pl.pallas_call(
        _dispatch_kernel,
        grid=(t // ROUTE_TILE,),
        in_specs=[pl.BlockSpec((1, 1, 2 * ROUTE_TILE), lambda i: (i, 0, 0), memory_space=pltpu.SMEM),
                  pl.BlockSpec((ROUTE_TILE, d), lambda i: (i, 0)),
                  pl.BlockSpec(memory_space=pl.ANY)],
        out_specs=pl.BlockSpec(memory_space=pl.ANY),
        out_shape=jax.ShapeDtypeStruct((n_rows, d), F32),
        scratch_shapes=[pltpu.SemaphoreType.DMA(())],
        input_output_aliases={2: 0},
        compiler_params=_params("arbitrary"),
        name="moe_dispatch",
    )(pos, x, jnp.zeros((n_rows, d), F32))


def _experts_kernel(te_ref, na_ref, x_ref, wg_ref, wu_ref, wd_ref, o_ref, xb_ref, acc_ref):
    i = pl.program_id(0)
    f = pl.program_id(1)
    active = i < na_ref[0]

    @pl.when(active & (f == 0))
    def _():
        xb_ref[...] = x_ref[...].astype(BF16)
        acc_ref[...] = jnp.zeros_like(acc_ref)

    @pl.when(active)
    def _():
        xb = xb_ref[...]
        gate = _dot(xb, wg_ref[...])
        up = _dot(xb, wu_ref[...])
        h = (gate * _sigmoid(gate) * up).astype(BF16)
        acc_ref[...] += _dot(h, wd_ref[...])

    @pl.when(f == pl.num_programs(1) - 1)
    def _():
        o_ref[...] = jnp.where(active, acc_ref[...], 0.0)


def _moe_experts(xs, tile_expert, n_active, wg, wu, wd, tf=512):
    n_rows, d = xs.shape
    fe = wg.shape[2]
    row = pl.BlockSpec((EXPERT_TILE, d), lambda i, f, te, na: (i, 0))
    return pl.pallas_call(
        _experts_kernel,
        grid_spec=pltpu.PrefetchScalarGridSpec(
            num_scalar_prefetch=2,
            grid=(n_rows // EXPERT_TILE, fe // tf),
            in_specs=[row,
                      pl.BlockSpec((None, d, tf), lambda i, f, te, na: (te[i], 0, f)),
                      pl.BlockSpec((None, d, tf), lambda i, f, te, na: (te[i], 0, f)),
                      pl.BlockSpec((None, tf, d), lambda i, f, te, na: (te[i], f, 0))],
            out_specs=row,
            scratch_shapes=[pltpu.VMEM((EXPERT_TILE, d), BF16), pltpu.VMEM((EXPERT_TILE, d), F32)]),
        out_shape=jax.ShapeDtypeStruct((n_rows, d), F32),
        compiler_params=_params("parallel", "arbitrary"),
        name="moe_experts",
    )(tile_expert, n_active, xs, wg, wu, wd)


def _combine_kernel(pos_ref, nxt_ref, ys_hbm, x_ref, rec_ref, g_ref, b_ref, o_ref, y_ref, sem):
    i = pl.program_id(0)
    n = pl.num_programs(0)
    slot = i % 2

    def copies(p_ref, s, t):
        return (_row_copy(ys_hbm, p_ref[0, 0, t], y_ref.at[s, 0], t, sem.at[s]),
                _row_copy(ys_hbm, p_ref[0, 0, ROUTE_TILE + t], y_ref.at[s, 1], t, sem.at[s]))

    def start_all(p_ref, s):
        def body(t, carry):
            for cp in copies(p_ref, s, t):
                cp.start()
            return carry
        lax.fori_loop(0, ROUTE_TILE, body, 0)

    @pl.when(i == 0)
    def _():
        start_all(pos_ref, slot)

    @pl.when(i + 1 < n)
    def _():
        start_all(nxt_ref, 1 - slot)

    def wait_body(t, carry):
        for cp in copies(pos_ref, slot, t):
            cp.wait()
        return carry
    lax.fori_loop(0, ROUTE_TILE, wait_body, 0)

    rec = rec_ref[...]
    lane = lax.broadcasted_iota(jnp.int32, rec.shape, 1)
    g1 = jnp.sum(jnp.where(lane == REC_G1, rec, 0.0), axis=-1, keepdims=True)
    g2 = jnp.sum(jnp.where(lane == REC_G2, rec, 0.0), axis=-1, keepdims=True)
    f = g1 * y_ref[slot, 0] + g2 * y_ref[slot, 1]
    o_ref[...] = _layer_norm(ALPHA * x_ref[...] + f, g_ref[...], b_ref[...])


def _moe_combine_ln(ys, pos, x, rec, g, b):
    t, d = x.shape
    n = t // ROUTE_TILE
    row = pl.BlockSpec((ROUTE_TILE, d), lambda i: (i, 0))
    vec = pl.BlockSpec((1, d), lambda i: (0, 0))
    return pl.pallas_call(
        _combine_kernel,
        grid=(n,),
        in_specs=[pl.BlockSpec((1, 1, 2 * ROUTE_TILE), lambda i: (i, 0, 0), memory_space=pltpu.SMEM),
                  pl.BlockSpec((1, 1, 2 * ROUTE_TILE), lambda i: (jnp.minimum(i + 1, n - 1), 0, 0),
                               memory_space=pltpu.SMEM),
                  pl.BlockSpec(memory_space=pl.ANY),
                  row,
                  pl.BlockSpec((ROUTE_TILE, LANES), lambda i: (i, 0)),
                  vec, vec],
        out_specs=row,
        out_shape=jax.ShapeDtypeStruct((t, d), F32),
        scratch_shapes=[pltpu.VMEM((2, 2, ROUTE_TILE, d), F32), pltpu.SemaphoreType.DMA((2,))],
        compiler_params=_params("arbitrary"),
        name="moe_combine_ln",
    )(pos, pos, ys, x, rec, g, b)


def _moe_res_ln(x, rec, counts, wg, wu, wd, g, b):
    t, d = x.shape
    n_tiles = 2 * t // EXPERT_TILE + N_EXPERTS
    cnt = counts[0, :N_EXPERTS].astype(jnp.int32)
    padded = (cnt + EXPERT_TILE - 1) // EXPERT_TILE * EXPERT_TILE
    ends = jnp.cumsum(padded)
    starts = ends - padded
    e1 = rec[:, REC_E1].astype(jnp.int32)
    e2 = rec[:, REC_E2].astype(jnp.int32)
    pos1 = starts[e1] + rec[:, REC_R1].astype(jnp.int32)
    pos2 = starts[e2] + rec[:, REC_R2].astype(jnp.int32)
    pos = jnp.concatenate([pos1.reshape(-1, 1, ROUTE_TILE), pos2.reshape(-1, 1, ROUTE_TILE)], axis=2)
    tile_start = jnp.arange(n_tiles, dtype=jnp.int32) * EXPERT_TILE
    tile_expert = jnp.minimum(jnp.sum(tile_start[:, None] >= ends[None, :], axis=1), N_EXPERTS - 1).astype(jnp.int32)
    n_active = (ends[-1:] // EXPERT_TILE).astype(jnp.int32)
    last_expert = tile_expert[jnp.maximum(n_active[0] - 1, 0)]
    tile_expert = jnp.where(jnp.arange(n_tiles) < n_active[0], tile_expert, last_expert)

    xs = _moe_dispatch(x, pos, n_tiles * EXPERT_TILE)
    ys = _moe_experts(xs, tile_expert, n_active, wg, wu, wd)
    return _moe_combine_ln(ys, pos, x, rec, g, b)


def _prepare_weights(na_w_qkv, na_rpb, na_w_o, ffn_w_gate, ffn_w_up, ffn_w_down,
                     rwkv_mu, rwkv_w_rkv, rwkv_w0, rwkv_w1, rwkv_w2, rwkv_a0, rwkv_a1, rwkv_a2,
                     rwkv_g1, rwkv_g2, rwkv_k_k, rwkv_k_a, rwkv_r_k, rwkv_lnx_g, rwkv_lnx_b, rwkv_w_o,
                     moe_w_router, moe_b_router, moe_w_gate, moe_w_up, moe_w_down,
                     ln_mix_g, ln_mix_b, ln_ffn_g, ln_ffn_b):
    d = D_MODEL
    vec = lambda a: a.reshape(1, d).astype(F32)
    gl_pad = 2 * LANES - GATE_LORA
    router_w = jnp.pad(moe_w_router[0], ((0, 0), (0, LANES - N_EXPERTS)))
    router_hi = router_w.astype(BF16)
    router_lo = (router_w - router_hi.astype(F32)).astype(BF16)
    return dict(
        qkv_w=na_w_qkv[0].astype(BF16),
        qkv_scale=jnp.concatenate([jnp.full((1, d), NA_HEAD_DIM ** -0.5, F32), jnp.ones((1, 2 * d), F32)], axis=1),
        na_bias=_na_bias_table(na_rpb[0]),
        na_wo=na_w_o[0].astype(BF16),
        ffn_wg=ffn_w_gate[0].astype(BF16), ffn_wu=ffn_w_up[0].astype(BF16), ffn_wd=ffn_w_down[0].astype(BF16),
        mu=rwkv_mu[0].astype(F32),
        wrkv=rwkv_w_rkv[0].astype(BF16),
        w1=jnp.concatenate([rwkv_w1[0, 0], rwkv_w1[0, 1]], axis=1).astype(BF16),
        w2=jnp.concatenate([rwkv_w2[0, 0], rwkv_w2[0, 1]], axis=0).astype(BF16),
        w0=rwkv_w0[0].astype(F32),
        a1=jnp.concatenate([rwkv_a1[0, 0], rwkv_a1[0, 1]], axis=1).astype(BF16),
        a2=jnp.concatenate([rwkv_a2[0, 0], rwkv_a2[0, 1]], axis=0).astype(BF16),
        a0=rwkv_a0[0].astype(F32),
        g1=jnp.pad(rwkv_g1[0], ((0, 0), (0, gl_pad))).astype(BF16),
        g2=jnp.pad(rwkv_g2[0], ((0, gl_pad), (0, 0))).astype(BF16),
        k_k=vec(rwkv_k_k[0]), k_a=vec(rwkv_k_a[0]), r_k=vec(rwkv_r_k[0]),
        lnx_g=vec(rwkv_lnx_g[0]), lnx_b=vec(rwkv_lnx_b[0]),
        rwkv_wo=rwkv_w_o[0].astype(BF16),
        router_w=jnp.stack([router_hi, router_lo]),
        router_b=jnp.pad(moe_b_router[0], (0, LANES - N_EXPERTS)).reshape(1, LANES).astype(F32),
        moe_wg=moe_w_gate[0].astype(BF16), moe_wu=moe_w_up[0].astype(BF16), moe_wd=moe_w_down[0].astype(BF16),
        ln_mix_g=ln_mix_g.astype(F32), ln_mix_b=ln_mix_b.astype(F32),
        ln_ffn_g=ln_ffn_g.astype(F32), ln_ffn_b=ln_ffn_b.astype(F32),
    )


def _trunk(x3, w):
    batch, seq_len, d = x3.shape
    rows = seq_len // GRID_W
    x = x3.reshape(batch * seq_len, d)
    ln = lambda a, i: a[i].reshape(1, d)

    qkv = _qkv_proj(x, w["qkv_w"], w["qkv_scale"])
    att = _na_attention(qkv, w["na_bias"], batch, rows)
    x = _proj_res_ln(att, w["na_wo"], x, ln(w["ln_mix_g"], 0), ln(w["ln_mix_b"], 0))
    x = _ffn_res_ln(x, w["ffn_wg"], w["ffn_wu"], w["ffn_wd"], ln(w["ln_ffn_g"], 0), ln(w["ln_ffn_b"], 0))

    r, k, kk, v, g, lw0, lw1, ia0, ia1 = _rwkv_proj(x, seq_len, w["mu"], w["wrkv"], w["w1"], w["w2"], w["w0"],
                                                    w["a1"], w["a2"], w["a0"], w["g1"], w["g2"], w["k_k"])
    yf, yb = _rwkv_scan(r, k, kk, v, lw0, lw1, ia0, ia1, w["k_a"], batch, seq_len)
    x = _rwkv_out(yf, yb, r, k, v, g, ia0, ia1, x, w["lnx_g"], w["lnx_b"], w["r_k"], w["k_a"], w["rwkv_wo"],
                  ln(w["ln_mix_g"], 1), ln(w["ln_mix_b"], 1))
    rec, counts = _router(x, w["router_w"], w["router_b"])
    x = _moe_res_ln(x, rec, counts, w["moe_wg"], w["moe_wu"], w["moe_wd"], ln(w["ln_ffn_g"], 1), ln(w["ln_ffn_b"], 1))
    return x.reshape(batch, seq_len, d)


def kernel(x_prompt, x_sample, na_w_qkv, na_rpb, na_w_o, ffn_w_gate, ffn_w_up, ffn_w_down, rwkv_mu, rwkv_w_rkv, rwkv_w0, rwkv_w1, rwkv_w2, rwkv_a0, rwkv_a1, rwkv_a2, rwkv_g1, rwkv_g2, rwkv_k_k, rwkv_k_a, rwkv_r_k, rwkv_lnx_g, rwkv_lnx_b, rwkv_w_o, moe_w_router, moe_b_router, moe_w_gate, moe_w_up, moe_w_down, ln_mix_g, ln_mix_b, ln_ffn_g, ln_ffn_b):
    w = _prepare_weights(na_w_qkv, na_rpb, na_w_o, ffn_w_gate, ffn_w_up, ffn_w_down,
                         rwkv_mu, rwkv_w_rkv, rwkv_w0, rwkv_w1, rwkv_w2, rwkv_a0, rwkv_a1, rwkv_a2,
                         rwkv_g1, rwkv_g2, rwkv_k_k, rwkv_k_a, rwkv_r_k, rwkv_lnx_g, rwkv_lnx_b, rwkv_w_o,
                         moe_w_router, moe_b_router, moe_w_gate, moe_w_up, moe_w_down,
                         ln_mix_g, ln_mix_b, ln_ffn_g, ln_ffn_b)
    return (_trunk(x_prompt, w), _trunk(x_sample, w))
```

```python
import functools
import math

import numpy as np
import jax
import jax.numpy as jnp
from jax import lax
from jax.experimental import pallas as pl
from jax.experimental.pallas import tpu as pltpu

D_MODEL = 1024
DEPTH = 2
GRID_W = 64
NA_HEAD_DIM = 32
NA_HEADS = D_MODEL // NA_HEAD_DIM
NA_ROWS = 8
NA_COLS = 16
RWKV_HEAD_DIM = 64
DECAY_LORA = 64
ICLR_LORA = 64
GATE_LORA = 160
GN_EPS = 64e-5
D_FF = 2816
N_EXPERTS = 8
D_EXPERT = 3584
LN_EPS = 1e-5
ALPHA = (2 * DEPTH) ** 0.25
LOG2E = math.log2(math.e)

LANES = 128
SCAN_CHUNK = 64
SCAN_BLOCK = 256
VMEM_LIMIT = 56 * 1024 * 1024

F32 = jnp.float32
BF16 = jnp.bfloat16


def _params(*sem):
    return pltpu.CompilerParams(dimension_semantics=sem, vmem_limit_bytes=VMEM_LIMIT)


def _dot(a, b):
    return jnp.dot(a, b, preferred_element_type=F32)


def _dot_nt(a, b):
    return lax.dot_general(a, b, (((1,), (1,)), ((), ())), preferred_element_type=F32)


def _dot_tn(a, b):
    return lax.dot_general(a, b, (((0,), (0,)), ((), ())), preferred_element_type=F32)


def _split(a):
    hi = a.astype(BF16)
    lo = (a - hi.astype(F32)).astype(BF16)
    return hi, lo


def _dot_hilo(a, b_exact):
    hi, lo = _split(a)
    return _dot(hi, b_exact) + _dot(lo, b_exact)


def _layer_norm(z, g, b):
    mu = jnp.mean(z, axis=-1, keepdims=True)
    zc = z - mu
    var = jnp.mean(zc * zc, axis=-1, keepdims=True)
    return zc * lax.rsqrt(var + LN_EPS) * g + b


def _sigmoid(z):
    return 1.0 / (1.0 + jnp.exp(-z))


def _head_sum_matrix(head_dim):
    lane = np.arange(LANES)
    return jnp.asarray((lane[:, None] // head_dim) == (lane[None, :] // head_dim), BF16)


def _qkv_kernel(x_ref, w_ref, s_ref, o_ref):
    acc = _dot(x_ref[...].astype(BF16), w_ref[...])
    o_ref[...] = (acc * s_ref[...]).astype(o_ref.dtype)


def _qkv_proj(x, w, colscale, tm=512):
    t, d = x.shape
    n = w.shape[1]
    return pl.pallas_call(
        _qkv_kernel,
        grid=(t // tm,),
        in_specs=[pl.BlockSpec((tm, d), lambda i: (i, 0)),
                  pl.BlockSpec((d, n), lambda i: (0, 0)),
                  pl.BlockSpec((1, n), lambda i: (0, 0))],
        out_specs=pl.BlockSpec((tm, n), lambda i: (i, 0)),
        out_shape=jax.ShapeDtypeStruct((t, n), BF16),
        compiler_params=_params("parallel"),
        name="qkv_proj",
    )(x, w, colscale)


NA_HEADS_PER_GROUP = LANES // NA_HEAD_DIM
NA_GROUPS = D_MODEL // LANES
NA_WIN = NA_ROWS * GRID_W


NA_GROUP_ROWS = NA_HEADS_PER_GROUP * GRID_W
NA_ROW_PAIRS = NA_ROWS // 2
NA_BIAS_PAIRS = 2 * NA_ROWS - 2


def _na_kernel(rows, q_ref, k_ref, v_ref, b_ref, o_ref):
    r = pl.program_id(1)
    off = r - jnp.clip(r - NA_ROWS // 2, 0, rows - NA_ROWS)
    lane_head = lax.broadcasted_iota(jnp.int32, (GRID_W, LANES), 1) // NA_HEAD_DIM
    groups = [slice(g * LANES, (g + 1) * LANES) for g in range(NA_GROUPS)]

    scores = []
    for cols in groups:
        qg = q_ref[:, cols]
        zero = jnp.zeros_like(qg)
        qm = jnp.concatenate([jnp.where(lane_head == h, qg, zero) for h in range(NA_HEADS_PER_GROUP)], axis=0)
        scores.append(_dot_nt(qm, k_ref[:, cols]))

    probs, denoms = [], []
    for g, s in enumerate(scores):
        hq = pl.ds(g * NA_GROUP_ROWS, NA_GROUP_ROWS)
        s = jnp.concatenate([s[:, m * LANES:(m + 1) * LANES] + b_ref[2 * m - off + NA_ROWS - 1, hq, :]
                             for m in range(NA_ROW_PAIRS)], axis=1)
        p = jnp.exp2(s - jnp.max(s, axis=-1, keepdims=True))
        denoms.append(jnp.sum(p, axis=-1, keepdims=True))
        probs.append(p.astype(BF16))

    for g, cols in enumerate(groups):
        o4 = _dot(probs[g], v_ref[:, cols]) / denoms[g]
        og = jnp.zeros((GRID_W, LANES), F32)
        for h in range(NA_HEADS_PER_GROUP):
            og = og + jnp.where(lane_head == h, o4[h * GRID_W:(h + 1) * GRID_W], 0.0)
        o_ref[:, cols] = og.astype(o_ref.dtype)


def _na_attention(qkv, bias, batch, rows):
    t = qkv.shape[0]
    half = NA_ROWS // 2

    def win_start(r):
        return jnp.clip(r - half, 0, rows - NA_ROWS)

    def q_map(b, r):
        return (b * rows + r, 0)

    def k_map(b, r):
        return ((b * rows + win_start(r)) * GRID_W, D_MODEL)

    def v_map(b, r):
        return ((b * rows + win_start(r)) * GRID_W, 2 * D_MODEL)

    return pl.pallas_call(
        functools.partial(_na_kernel, rows),
        grid=(batch, rows),
        in_specs=[pl.BlockSpec((GRID_W, D_MODEL), q_map),
                  pl.BlockSpec((pl.Element(NA_WIN), pl.Element(D_MODEL)), k_map),
                  pl.BlockSpec((pl.Element(NA_WIN), pl.Element(D_MODEL)), v_map),
                  pl.BlockSpec(bias.shape, lambda b, r: (0, 0, 0), pipeline_mode=pl.Buffered(1))],
        out_specs=pl.BlockSpec((GRID_W, D_MODEL), q_map),
        out_shape=jax.ShapeDtypeStruct((t, D_MODEL), BF16),
        compiler_params=_params("parallel", "arbitrary"),
        name="na_attention",
    )(qkv, qkv, qkv, bias)


def _na_bias_kernel(rpb_ref, oh_ref, mask_ref, o_ref):
    x = rpb_ref[...]
    hi = x.astype(BF16)
    r1 = x - hi.astype(F32)
    mid = r1.astype(BF16)
    lo = (r1 - mid.astype(F32)).astype(BF16)
    oh = oh_ref[...]
    o_ref[...] = (_dot(hi, oh) + _dot(mid, oh) + _dot(lo, oh)) * LOG2E + mask_ref[...]


def _na_bias_table(rpb, tn=1024):
    n_dr, n_dc = 2 * NA_ROWS - 1, 2 * NA_COLS - 1
    c = np.arange(GRID_W)
    q_start = np.clip(c - NA_COLS // 2, 0, GRID_W - NA_COLS)
    kc = np.arange(GRID_W)
    valid = (kc[None, :] >= q_start[:, None]) & (kc[None, :] < q_start[:, None] + NA_COLS)
    dc = kc[None, :] - c[:, None] + NA_COLS - 1
    onehot = (np.arange(LANES)[:, None, None] == dc[None]) & valid[None]
    onehot = jnp.asarray(onehot.reshape(LANES, GRID_W * GRID_W), BF16)
    mask = jnp.asarray(np.where(valid, 0.0, -np.inf).reshape(1, GRID_W * GRID_W), F32)
    rpb2 = jnp.pad(rpb.reshape(NA_HEADS * n_dr, n_dc).astype(F32), ((0, 0), (0, LANES - n_dc)))
    nrow, ncol = rpb2.shape[0], GRID_W * GRID_W
    flat = pl.pallas_call(
        _na_bias_kernel,
        grid=(ncol // tn,),
        in_specs=[pl.BlockSpec((nrow, LANES), lambda j: (0, 0)),
                  pl.BlockSpec((LANES, tn), lambda j: (0, j)),
                  pl.BlockSpec((1, tn), lambda j: (0, j))],
        out_specs=pl.BlockSpec((nrow, tn), lambda j: (0, j)),
        out_shape=jax.ShapeDtypeStruct((nrow, ncol), F32),
        compiler_params=_params("parallel"),
        name="na_bias_table",
    )(rpb2, onehot, mask)
    toe = flat.reshape(NA_HEADS, n_dr, GRID_W, GRID_W)
    pairs = jnp.stack([toe[:, 0:NA_BIAS_PAIRS], toe[:, 1:NA_BIAS_PAIRS + 1]], axis=3)
    return jnp.transpose(pairs, (1, 0, 2, 3, 4)).reshape(NA_BIAS_PAIRS, NA_HEADS * GRID_W, LANES)


def _proj_ln_kernel(a_ref, w_ref, x_ref, g_ref, b_ref, o_ref):
    h = _dot(a_ref[...], w_ref[...])
    o_ref[...] = _layer_norm(ALPHA * x_ref[...] + h, g_ref[...], b_ref[...])


def _proj_res_ln(a, w, x, g, b, tm=512):
    t, d = x.shape
    row = pl.BlockSpec((tm, d), lambda i: (i, 0))
    vec = pl.BlockSpec((1, d), lambda i: (0, 0))
    return pl.pallas_call(
        _proj_ln_kernel,
        grid=(t // tm,),
        in_specs=[row, pl.BlockSpec((d, d), lambda i: (0, 0)), row, vec, vec],
        out_specs=row,
        out_shape=jax.ShapeDtypeStruct((t, d), F32),
        compiler_params=_params("parallel"),
        name="proj_res_ln",
    )(a, w, x, g, b)


def _ffn_kernel(x_ref, wg_ref, wu_ref, wd_ref, g_ref, b_ref, o_ref, xb_ref, acc_ref):
    f = pl.program_id(1)

    @pl.when(f == 0)
    def _():
        xb_ref[...] = x_ref[...].astype(BF16)
        acc_ref[...] = jnp.zeros_like(acc_ref)

    xb = xb_ref[...]
    gate = _dot(xb, wg_ref[...])
    up = _dot(xb, wu_ref[...])
    h = (gate * _sigmoid(gate) * up).astype(BF16)
    acc_ref[...] += _dot(h, wd_ref[...])

    @pl.when(f == pl.num_programs(1) - 1)
    def _():
        o_ref[...] = _layer_norm(ALPHA * x_ref[...] + acc_ref[...], g_ref[...], b_ref[...])


def _ffn_res_ln(x, wg, wu, wd, g, b, tm=512, tf=1408):
    t, d = x.shape
    ff = wg.shape[1]
    row = pl.BlockSpec((tm, d), lambda i, f: (i, 0))
    vec = pl.BlockSpec((1, d), lambda i, f: (0, 0))
    return pl.pallas_call(
        _ffn_kernel,
        grid=(t // tm, ff // tf),
        in_specs=[row,
                  pl.BlockSpec((d, tf), lambda i, f: (0, f)),
                  pl.BlockSpec((d, tf), lambda i, f: (0, f)),
                  pl.BlockSpec((tf, d), lambda i, f: (f, 0)),
                  vec, vec],
        out_specs=row,
        out_shape=jax.ShapeDtypeStruct((t, d), F32),
        scratch_shapes=[pltpu.VMEM((tm, d), BF16), pltpu.VMEM((tm, d), F32)],
        compiler_params=_params("parallel", "arbitrary"),
        name="ffn_res_ln",
    )(x, wg, wu, wd, g, b)


HALO = 8


def _rwkv_proj_kernel(seq_len, x_ref, xp_ref, xn_ref, mu_ref, wrkv_ref, w1_ref, w2_ref, w0_ref,
                      a1_ref, a2_ref, a0_ref, g1_ref, g2_ref, kscale_ref, hs_ref,
                      r_ref, k_ref, kk_ref, v_ref, g_ref, lw0_ref, lw1_ref, ia0_ref, ia1_ref):
    tm = x_ref.shape[0]
    i = pl.program_id(0)
    x = x_ref[...]
    row = lax.broadcasted_iota(jnp.int32, x.shape, 0)
    first_in_seq = (i * tm) % seq_len == 0
    last_in_seq = ((i + 1) * tm) % seq_len == 0
    prev_row = jnp.where(first_in_seq, 0.0, xp_ref[HALO - 1:HALO, :])
    next_row = jnp.where(last_in_seq, 0.0, xn_ref[0:1, :])
    x_prev = jnp.where(row == 0, prev_row, pltpu.roll(x, 1, axis=0))
    x_next = jnp.where(row == tm - 1, next_row, pltpu.roll(x, tm - 1, axis=0))
    xx = 0.5 * (x_prev + x_next) - x

    def mix(j):
        return (x + xx * mu_ref[j:j + 1, :]).astype(BF16)

    r_ref[...] = _dot(mix(0), wrkv_ref[0]).astype(r_ref.dtype)
    kf = _dot(mix(2), wrkv_ref[1])
    k_ref[...] = kf.astype(k_ref.dtype)
    v_ref[...] = _dot(mix(3), wrkv_ref[2]).astype(v_ref.dtype)
    hsum = hs_ref[...]
    for gi in range(D_MODEL // LANES):
        cols = slice(gi * LANES, (gi + 1) * LANES)
        kkr = kf[:, cols] * kscale_ref[:, cols]
        norm = jnp.sqrt(_dot_hilo(kkr * kkr, hsum))
        kk_ref[:, cols] = (kkr / jnp.maximum(norm, 1e-12)).astype(kk_ref.dtype)

    lane = lax.broadcasted_iota(jnp.int32, (tm, LANES), 1)
    dir0 = lane < DECAY_LORA

    lo = jnp.tanh(_dot(mix(1), w1_ref[...]))
    for z, out in enumerate((lw0_ref, lw1_ref)):
        lz = jnp.where(dir0 if z == 0 else ~dir0, lo, 0.0).astype(BF16)
        wl = w0_ref[z:z + 1, :] + _dot(lz, w2_ref[...])
        out[...] = (-math.exp(-0.5)) * _sigmoid(wl)

    al = _dot(mix(4), a1_ref[...])
    for z, out in enumerate((ia0_ref, ia1_ref)):
        az = jnp.where(dir0 if z == 0 else ~dir0, al, 0.0).astype(BF16)
        out[...] = _sigmoid(a0_ref[z:z + 1, :] + _dot(az, a2_ref[...])).astype(out.dtype)

    gl = _sigmoid(_dot(mix(5), g1_ref[...])).astype(BF16)
    g_ref[...] = _dot(gl, g2_ref[...]).astype(g_ref.dtype)


def _rwkv_proj(x, seq_len, mu, wrkv, w1, w2, w0, a1, a2, a0, g1, g2, kscale, tm=256):
    t, d = x.shape
    nb = t // HALO
    step = tm // HALO
    row = pl.BlockSpec((tm, d), lambda i: (i, 0))
    full = lambda shape: pl.BlockSpec(shape, lambda i: (0,) * len(shape))
    hsum = _head_sum_matrix(RWKV_HEAD_DIM)
    outs = [jax.ShapeDtypeStruct((t, d), BF16)] * 5 + [jax.ShapeDtypeStruct((t, d), F32)] * 2 + \
           [jax.ShapeDtypeStruct((t, d), BF16)] * 2
    return pl.pallas_call(
        functools.partial(_rwkv_proj_kernel, seq_len),
        grid=(t // tm,),
        in_specs=[row,
                  pl.BlockSpec((HALO, d), lambda i: (jnp.maximum(i * step - 1, 0), 0)),
                  pl.BlockSpec((HALO, d), lambda i: (jnp.minimum((i + 1) * step, nb - 1), 0)),
                  full(mu.shape), full(wrkv.shape), full(w1.shape), full(w2.shape), full(w0.shape),
                  full(a1.shape), full(a2.shape), full(a0.shape), full(g1.shape), full(g2.shape),
                  full(kscale.shape), full(hsum.shape)],
        out_specs=[row] * 9,
        out_shape=outs,
        compiler_params=_params("parallel"),
        name="rwkv_proj",
    )(x, x, x, mu, wrkv, w1, w2, w0, a1, a2, a0, g1, g2, kscale, hsum)


def _pair_blockdiag(y, lane_a):
    zero = jnp.zeros_like(y)
    return jnp.concatenate([jnp.where(lane_a, y, zero), jnp.where(lane_a, zero, y)], axis=0)


def _pair_matmul(x, y, lane_a):
    return _dot(x.astype(BF16), _pair_blockdiag(y.astype(BF16), lane_a))


def _scan_prepare(chains):
    c = SCAN_CHUNK
    t_idx = lax.broadcasted_iota(jnp.int32, (c, LANES), 0)
    s_idx = lax.broadcasted_iota(jnp.int32, (c, LANES), 1) % c
    lane_a = lax.broadcasted_iota(jnp.int32, (c, LANES), 1) < RWKV_HEAD_DIM
    eye = (s_idx == t_idx).astype(F32)
    tri_r = lax.broadcasted_iota(jnp.int32, (c, c), 0)
    tri_c = lax.broadcasted_iota(jnp.int32, (c, c), 1)
    bd = (lax.broadcasted_iota(jnp.int32, (LANES, LANES), 0) // RWKV_HEAD_DIM ==
          lax.broadcasted_iota(jnp.int32, (LANES, LANES), 1) // RWKV_HEAD_DIM)
    before = {rev: ((s_idx > t_idx) if rev else (s_idx < t_idx)) for rev in (False, True)}
    upto = {rev: before[rev] | (s_idx == t_idx) for rev in (False, True)}
    cum = {rev: ((tri_c >= tri_r) if rev else (tri_c <= tri_r)).astype(BF16) for rev in (False, True)}
    n = len(chains)

    lw_split = [jnp.concatenate(_split(ch[6]), axis=1) for ch in chains]
    lwc2 = [_dot(cum[chains[i][0]], lw_split[i]) for i in range(n)]
    lwc = [x[:, 0:LANES] + x[:, LANES:2 * LANES] for x in lwc2]
    vals = [ch[5] for ch in chains]

    at, rt, bh, kh, w_tot, lhs, rhs = [], [], [], [], [], [], []
    for (rev, ka, r, k, kk, v, lw, ia), lc in zip(chains, lwc):
        last = lc[0:1, :] if rev else lc[c - 1:c, :]
        w_in = jnp.exp(lc)
        w_ex = jnp.exp(lc - lw)
        w_inv = jnp.exp(-lc)
        w_end = jnp.exp(last)
        w_out = w_end * w_inv
        kf = k.astype(F32)
        kkf = kk.astype(F32)
        iaf = ia.astype(F32)
        kd = kf * (1.0 + (iaf - 1.0) * ka)
        bvec = kkf * iaf
        a_t = (-kkf) * w_ex
        r_t = r.astype(F32) * w_in
        at.append(a_t)
        rt.append(r_t)
        bh.append((bvec * w_out).astype(BF16))
        kh.append((kd * w_out).astype(BF16))
        w_tot.append(w_end)
        lhs.append(jnp.concatenate([a_t, r_t], axis=0).astype(BF16))
        rhs.append(jnp.concatenate([_pair_blockdiag((bvec * w_inv).astype(BF16), lane_a),
                                    _pair_blockdiag((kd * w_inv).astype(BF16), lane_a)], axis=0))

    gram = [_dot_nt(lhs[i], rhs[i]) for i in range(n)]
    a_ab = [jnp.where(before[chains[i][0]], gram[i][0:c, 0:LANES], 0.0) for i in range(n)]
    a_ak = [jnp.where(before[chains[i][0]], gram[i][0:c, LANES:2 * LANES], 0.0) for i in range(n)]
    a_rb = [jnp.where(upto[chains[i][0]], gram[i][c:2 * c, 0:LANES], 0.0) for i in range(n)]
    a_rk = [jnp.where(upto[chains[i][0]], gram[i][c:2 * c, LANES:2 * LANES], 0.0) for i in range(n)]

    tinv = [eye + a for a in a_ab]
    pw = [_pair_matmul(a, a, lane_a) for a in a_ab]
    n_sq = int(math.log2(c)) - 1
    for j in range(n_sq):
        final = j == n_sq - 1
        lhs_j = [(tinv[i] if final else jnp.concatenate([tinv[i], pw[i]], axis=0)).astype(BF16) for i in range(n)]
        res = [_dot(lhs_j[i], _pair_blockdiag(pw[i].astype(BF16), lane_a)) for i in range(n)]
        tinv = [tinv[i] + res[i][0:c] for i in range(n)]
        if not final:
            pw = [res[i][c:2 * c] for i in range(n)]

    av = [_pair_matmul(a_ak[i], vals[i], lane_a) for i in range(n)]
    y_loc = [_pair_matmul(a_rk[i], vals[i], lane_a) for i in range(n)]
    pq = [_dot(tinv[i].astype(BF16),
               jnp.concatenate([_pair_blockdiag(at[i].astype(BF16), lane_a),
                                _pair_blockdiag(av[i].astype(BF16), lane_a)], axis=1)) for i in range(n)]
    pqb = [x.astype(BF16) for x in pq]
    mn = [_dot_tn(pqb[i], bh[i]) for i in range(n)]
    kv = [_dot_tn(vals[i], kh[i]) for i in range(n)]
    arb = [_dot(a_rb[i].astype(BF16),
                jnp.concatenate([_pair_blockdiag(pqb[i][:, 0:LANES], lane_a),
                                 _pair_blockdiag(pqb[i][:, LANES:2 * LANES], lane_a)], axis=1)) for i in range(n)]
    out = []
    for i in range(n):
        ry = (rt[i] + arb[i][:, 0:LANES]).astype(BF16)
        yc = arb[i][:, LANES:2 * LANES] + y_loc[i]
        m_state = jnp.where(bd, mn[i][0:LANES], 0.0).astype(BF16)
        n_state = jnp.where(bd, mn[i][LANES:2 * LANES] + kv[i], 0.0)
        out.append((ry, yc, m_state, n_state, w_tot[i]))
    return out


def _scan_kernel(n_pairs, rf_ref, kf_ref, kkf_ref, vf_ref, lwf_ref, iaf_ref,
                 rb_ref, kb_ref, kkb_ref, vb_ref, lwb_ref, iab_ref, ka_ref, yf_ref, yb_ref, s_ref):
    @pl.when(pl.program_id(2) == 0)
    def _():
        s_ref[...] = jnp.zeros_like(s_ref)

    c = SCAN_CHUNK
    n_chunks = SCAN_BLOCK // c
    dirs = ((False, (rf_ref, kf_ref, kkf_ref, vf_ref, lwf_ref, iaf_ref), yf_ref),
            (True, (rb_ref, kb_ref, kkb_ref, vb_ref, lwb_ref, iab_ref), yb_ref))
    chains, where = [], []
    for step in range(n_chunks):
        for p in range(n_pairs):
            cols = slice(p * LANES, (p + 1) * LANES)
            for di, (rev, refs, _) in enumerate(dirs):
                ci = n_chunks - 1 - step if rev else step
                rows = slice(ci * c, (ci + 1) * c)
                chains.append((rev, ka_ref[:, cols]) + tuple(ref[rows, cols] for ref in refs))
                where.append((2 * p + di, di, rows, cols))
    pre = _scan_prepare(chains)
    state = [s_ref[j] for j in range(2 * n_pairs)]
    for (j, di, rows, cols), (ry, yc, m_state, n_state, w_total) in zip(where, pre):
        s = state[j]
        sb = s.astype(BF16)
        dirs[di][2][rows, cols] = _dot_nt(ry, sb) + yc
        state[j] = s * w_total + _dot(sb, m_state) + n_state
    for j in range(2 * n_pairs):
        s_ref[j] = state[j]


def _rwkv_scan(r, k, kk, v, lw0, lw1, ia0, ia1, ka, batch, seq_len, n_pairs=2):
    t, d = r.shape
    ns = seq_len // SCAN_BLOCK
    width = n_pairs * LANES
    fwd = pl.BlockSpec((SCAN_BLOCK, width), lambda b, h, s: (b * ns + s, h))
    bwd = pl.BlockSpec((SCAN_BLOCK, width), lambda b, h, s: (b * ns + ns - 1 - s, h))
    vec = pl.BlockSpec((1, width), lambda b, h, s: (0, h))
    return pl.pallas_call(
        functools.partial(_scan_kernel, n_pairs),
        grid=(batch, d // width, ns),
        in_specs=[fwd] * 6 + [bwd] * 6 + [vec],
        out_specs=[fwd, bwd],
        out_shape=[jax.ShapeDtypeStruct((t, d), F32)] * 2,
        scratch_shapes=[pltpu.VMEM((2 * n_pairs, LANES, LANES), F32)],
        compiler_params=_params("parallel", "parallel", "arbitrary"),
        name="rwkv_scan",
    )(r, k, kk, v, lw0, ia0, r, k, kk, v, lw1, ia1, ka)


def _rwkv_out_kernel(yf_ref, yb_ref, r_ref, k_ref, v_ref, g_ref, ia0_ref, ia1_ref, x_ref,
                     lg_ref, lb_ref, rk_ref, ka_ref, hs_ref, wo_ref, ng_ref, nb_ref, o_ref, z_ref):
    hsum = hs_ref[...]
    inv = 1.0 / RWKV_HEAD_DIM
    for gi in range(D_MODEL // LANES):
        cols = slice(gi * LANES, (gi + 1) * LANES)
        y = yf_ref[:, cols] + yb_ref[:, cols]
        yc = y - _dot_hilo(y, hsum) * inv
        var = _dot_hilo(yc * yc, hsum) * inv
        yn = yc * lax.rsqrt(var + GN_EPS) * lg_ref[:, cols] + lb_ref[:, cols]
        rr = r_ref[:, cols].astype(F32)
        kk = k_ref[:, cols].astype(F32)
        ia = ia0_ref[:, cols].astype(F32) + ia1_ref[:, cols].astype(F32)
        ksum = kk * (2.0 + (ia - 2.0) * ka_ref[:, cols])
        bonus = _dot_hilo(rr * ksum * rk_ref[:, cols], hsum) * v_ref[:, cols].astype(F32)
        z_ref[:, cols] = ((yn + bonus) * g_ref[:, cols].astype(F32)).astype(BF16)
    h = _dot(z_ref[...], wo_ref[...])
    o_ref[...] = _layer_norm(ALPHA * x_ref[...] + h, ng_ref[...], nb_ref[...])


def _rwkv_out(yf, yb, r, k, v, g, ia0, ia1, x, lnx_g, lnx_b, r_k, k_a, wo, ng, nb, tm=256):
    t, d = x.shape
    row = pl.BlockSpec((tm, d), lambda i: (i, 0))
    vec = pl.BlockSpec((1, d), lambda i: (0, 0))
    return pl.pallas_call(
        _rwkv_out_kernel,
        grid=(t // tm,),
        in_specs=[row] * 9 + [vec] * 4 + [pl.BlockSpec((LANES, LANES), lambda i: (0, 0)),
                                          pl.BlockSpec((d, d), lambda i: (0, 0)), vec, vec],
        out_specs=row,
        out_shape=jax.ShapeDtypeStruct((t, d), F32),
        scratch_shapes=[pltpu.VMEM((tm, d), BF16)],
        compiler_params=_params("parallel"),
        name="rwkv_out",
    )(yf, yb, r, k, v, g, ia0, ia1, x, lnx_g, lnx_b, r_k, k_a, _head_sum_matrix(RWKV_HEAD_DIM), wo, ng, nb)


REC_E1, REC_E2, REC_R1, REC_R2, REC_G1, REC_G2 = range(6)


def _router_kernel(x_ref, w_ref, b_ref, tri_ref, rec_ref, cnt_ref, run_ref):
    @pl.when(pl.program_id(0) == 0)
    def _():
        run_ref[...] = jnp.zeros_like(run_ref)

    xh, xl = _split(x_ref[...])
    wh = w_ref[0]
    wl = w_ref[1]
    logits = _dot(xh, wh) + (_dot(xh, wl) + _dot(xl, wh)) + b_ref[...]
    lane = lax.broadcasted_iota(jnp.int32, logits.shape, 1)
    logits = jnp.where(lane < N_EXPERTS, logits, -jnp.inf)
    m1 = jnp.max(logits, axis=-1, keepdims=True)
    i1 = jnp.min(jnp.where(logits == m1, lane, LANES), axis=-1, keepdims=True)
    rest = jnp.where(lane == i1, -jnp.inf, logits)
    m2 = jnp.max(rest, axis=-1, keepdims=True)
    i2 = jnp.min(jnp.where(rest == m2, lane, LANES), axis=-1, keepdims=True)
    e2 = jnp.exp(m2 - m1)
    g1 = 1.0 / (1.0 + e2)
    g2 = e2 / (1.0 + e2)

    hot1 = lane == i1
    hot2 = lane == i2
    oh1 = hot1.astype(BF16)
    oh2 = hot2.astype(BF16)
    tri = tri_ref[...]
    run = run_ref[...]
    n1 = jnp.sum(oh1.astype(F32), axis=0, keepdims=True)
    n2 = jnp.sum(oh2.astype(F32), axis=0, keepdims=True)
    before1 = run + _dot(tri, oh1)
    before2 = run + n1 + _dot(tri, oh2)
    r1 = jnp.sum(jnp.where(hot1, before1, 0.0), axis=-1, keepdims=True)
    r2 = jnp.sum(jnp.where(hot2, before2, 0.0), axis=-1, keepdims=True)
    run = run + n1 + n2
    run_ref[...] = run
    cnt_ref[...] = run

    rec = jnp.zeros(logits.shape, F32)
    for idx, val in ((REC_E1, i1.astype(F32)), (REC_E2, i2.astype(F32)), (REC_R1, r1), (REC_R2, r2),
                     (REC_G1, g1), (REC_G2, g2)):
        rec = jnp.where(lane == idx, val, rec)
    rec_ref[...] = rec


def _router(x, w_hilo, bias, tm=512):
    t, d = x.shape
    tri = jnp.asarray(np.tril(np.ones((tm, tm), np.float32), -1), BF16)
    return pl.pallas_call(
        _router_kernel,
        grid=(t // tm,),
        in_specs=[pl.BlockSpec((tm, d), lambda i: (i, 0)),
                  pl.BlockSpec((2, d, LANES), lambda i: (0, 0, 0)),
                  pl.BlockSpec((1, LANES), lambda i: (0, 0)),
                  pl.BlockSpec((tm, tm), lambda i: (0, 0))],
        out_specs=[pl.BlockSpec((tm, LANES), lambda i: (i, 0)),
                   pl.BlockSpec((1, LANES), lambda i: (0, 0))],
        out_shape=[jax.ShapeDtypeStruct((t, LANES), F32), jax.ShapeDtypeStruct((1, LANES), F32)],
        scratch_shapes=[pltpu.VMEM((1, LANES), F32)],
        compiler_params=_params("arbitrary"),
        name="moe_router",
    )(x, w_hilo, bias, tri)


EXPERT_TILE = 1024
ROUTE_TILE = 512


def _row_copy(src, src_row, dst, dst_row, sem):
    return pltpu.make_async_copy(src.at[pl.ds(src_row, 1)], dst.at[pl.ds(dst_row, 1)], sem)


def _dispatch_kernel(pos_ref, x_ref, xs_in, xs_hbm, sem):
    del xs_in

    def copies(t):
        return (_row_copy(x_ref, t, xs_hbm, pos_ref[0, 0, t], sem),
                _row_copy(x_ref, t, xs_hbm, pos_ref[0, 0, ROUTE_TILE + t], sem))

    def start(t, carry):
        for prio, cp in enumerate(copies(t)):
            cp.start(priority=prio)
        return carry

    def wait(t, carry):
        for cp in copies(t):
            cp.wait()
        return carry

    lax.fori_loop(0, ROUTE_TILE, start, 0)
    lax.fori_loop(0, ROUTE_TILE, wait, 0)


def _moe_dispatch(x, pos, n_rows):
    t, d = x.shape
    return pl.pallas_call(
        _dispatch_kernel,
        grid=(t // ROUTE_TILE,),
        in_specs=[pl.BlockSpec((1, 1, 2 * ROUTE_TILE), lambda i: (i, 0, 0), memory_space=pltpu.SMEM),
                  pl.BlockSpec((ROUTE_TILE, d), lambda i: (i, 0)),
                  pl.BlockSpec(memory_space=pl.ANY)],
        out_specs=pl.BlockSpec(memory_space=pl.ANY),
        out_shape=jax.ShapeDtypeStruct((n_rows, d), F32),
        scratch_shapes=[pltpu.SemaphoreType.DMA(())],
        input_output_aliases={2: 0},
        compiler_params=_params("arbitrary"),
        name="moe_dispatch",
    )(pos, x, jnp.zeros((n_rows, d), F32))


def _experts_kernel(te_ref, na_ref, x_ref, wg_ref, wu_ref, wd_ref, o_ref, xb_ref, acc_ref):
    i = pl.program_id(0)
    f = pl.program_id(1)
    active = i < na_ref[0]

    @pl.when(active & (f == 0))
    def _():
        xb_ref[...] = x_ref[...].astype(BF16)
        acc_ref[...] = jnp.zeros_like(acc_ref)

    @pl.when(active)
    def _():
        xb = xb_ref[...]
        gate = _dot(xb, wg_ref[...])
        up = _dot(xb, wu_ref[...])
        h = (gate * _sigmoid(gate) * up).astype(BF16)
        acc_ref[...] += _dot(h, wd_ref[...])

    @pl.when(f == pl.num_programs(1) - 1)
    def _():
        o_ref[...] = jnp.where(active, acc_ref[...], 0.0)


def _moe_experts(xs, tile_expert, n_active, wg, wu, wd, tf=512):
    n_rows, d = xs.shape
    fe = wg.shape[2]
    row = pl.BlockSpec((EXPERT_TILE, d), lambda i, f, te, na: (i, 0))
    return pl.pallas_call(
        _experts_kernel,
        grid_spec=pltpu.PrefetchScalarGridSpec(
            num_scalar_prefetch=2,
            grid=(n_rows // EXPERT_TILE, fe // tf),
            in_specs=[row,
                      pl.BlockSpec((None, d, tf), lambda i, f, te, na: (te[i], 0, f)),
                      pl.BlockSpec((None, d, tf), lambda i, f, te, na: (te[i], 0, f)),
                      pl.BlockSpec((None, tf, d), lambda i, f, te, na: (te[i], f, 0))],
            out_specs=row,
            scratch_shapes=[pltpu.VMEM((EXPERT_TILE, d), BF16), pltpu.VMEM((EXPERT_TILE, d), F32)]),
        out_shape=jax.ShapeDtypeStruct((n_rows, d), F32),
        compiler_params=_params("parallel", "arbitrary"),
        name="moe_experts",
    )(tile_expert, n_active, xs, wg, wu, wd)


def _combine_kernel(pos_ref, nxt_ref, ys_hbm, x_ref, rec_ref, g_ref, b_ref, o_ref, y_ref, sem):
    i = pl.program_id(0)
    n = pl.num_programs(0)
    slot = i % 2

    def copies(p_ref, s, t):
        return (_row_copy(ys_hbm, p_ref[0, 0, t], y_ref.at[s, 0], t, sem.at[s]),
                _row_copy(ys_hbm, p_ref[0, 0, ROUTE_TILE + t], y_ref.at[s, 1], t, sem.at[s]))

    def start_all(p_ref, s):
        def body(t, carry):
            for prio, cp in enumerate(copies(p_ref, s, t)):
                cp.start(priority=prio)
            return carry
        lax.fori_loop(0, ROUTE_TILE, body, 0)

    @pl.when(i == 0)
    def _():
        start_all(pos_ref, slot)

    @pl.when(i + 1 < n)
    def _():
        start_all(nxt_ref, 1 - slot)

    def wait_body(t, carry):
        for cp in copies(pos_ref, slot, t):
            cp.wait()
        return carry
    lax.fori_loop(0, ROUTE_TILE, wait_body, 0)

    rec = rec_ref[...]
    lane = lax.broadcasted_iota(jnp.int32, rec.shape, 1)
    g1 = jnp.sum(jnp.where(lane == REC_G1, rec, 0.0), axis=-1, keepdims=True)
    g2 = jnp.sum(jnp.where(lane == REC_G2, rec, 0.0), axis=-1, keepdims=True)
    f = g1 * y_ref[slot, 0] + g2 * y_ref[slot, 1]
    o_ref[...] = _layer_norm(ALPHA * x_ref[...] + f, g_ref[...], b_ref[...])


def _moe_combine_ln(ys, pos, x, rec, g, b):
    t, d = x.shape
    n = t // ROUTE_TILE
    row = pl.BlockSpec((ROUTE_TILE, d), lambda i: (i, 0))
    vec = pl.BlockSpec((1, d), lambda i: (0, 0))
    return pl.pallas_call(
        _combine_kernel,
        grid=(n,),
        in_specs=[pl.BlockSpec((1, 1, 2 * ROUTE_TILE), lambda i: (i, 0, 0), memory_space=pltpu.SMEM),
                  pl.BlockSpec((1, 1, 2 * ROUTE_TILE), lambda i: (jnp.minimum(i + 1, n - 1), 0, 0),
                               memory_space=pltpu.SMEM),
                  pl.BlockSpec(memory_space=pl.ANY),
                  row,
                  pl.BlockSpec((ROUTE_TILE, LANES), lambda i: (i, 0)),
                  vec, vec],
        out_specs=row,
        out_shape=jax.ShapeDtypeStruct((t, d), F32),
        scratch_shapes=[pltpu.VMEM((2, 2, ROUTE_TILE, d), F32), pltpu.SemaphoreType.DMA((2,))],
        compiler_params=_params("arbitrary"),
        name="moe_combine_ln",
    )(pos, pos, ys, x, rec, g, b)


def _moe_res_ln(x, rec, counts, wg, wu, wd, g, b):
    t, d = x.shape
    n_tiles = 2 * t // EXPERT_TILE + N_EXPERTS
    cnt = counts[0, :N_EXPERTS].astype(jnp.int32)
    padded = (cnt + EXPERT_TILE - 1) // EXPERT_TILE * EXPERT_TILE
    ends = jnp.cumsum(padded)
    starts = ends - padded
    e1 = rec[:, REC_E1].astype(jnp.int32)
    e2 = rec[:, REC_E2].astype(jnp.int32)
    pos1 = starts[e1] + rec[:, REC_R1].astype(jnp.int32)
    pos2 = starts[e2] + rec[:, REC_R2].astype(jnp.int32)
    pos = jnp.concatenate([pos1.reshape(-1, 1, ROUTE_TILE), pos2.reshape(-1, 1, ROUTE_TILE)], axis=2)
    tile_start = jnp.arange(n_tiles, dtype=jnp.int32) * EXPERT_TILE
    tile_expert = jnp.minimum(jnp.sum(tile_start[:, None] >= ends[None, :], axis=1), N_EXPERTS - 1).astype(jnp.int32)
    n_active = (ends[-1:] // EXPERT_TILE).astype(jnp.int32)
    last_expert = tile_expert[jnp.maximum(n_active[0] - 1, 0)]
    tile_expert = jnp.where(jnp.arange(n_tiles) < n_active[0], tile_expert, last_expert)

    xs = _moe_dispatch(x, pos, n_tiles * EXPERT_TILE)
    ys = _moe_experts(xs, tile_expert, n_active, wg, wu, wd)
    return _moe_combine_ln(ys, pos, x, rec, g, b)


def _prepare_weights(na_w_qkv, na_rpb, na_w_o, ffn_w_gate, ffn_w_up, ffn_w_down,
                     rwkv_mu, rwkv_w_rkv, rwkv_w0, rwkv_w1, rwkv_w2, rwkv_a0, rwkv_a1, rwkv_a2,
                     rwkv_g1, rwkv_g2, rwkv_k_k, rwkv_k_a, rwkv_r_k, rwkv_lnx_g, rwkv_lnx_b, rwkv_w_o,
                     moe_w_router, moe_b_router, moe_w_gate, moe_w_up, moe_w_down,
                     ln_mix_g, ln_mix_b, ln_ffn_g, ln_ffn_b):
    d = D_MODEL
    vec = lambda a: a.reshape(1, d).astype(F32)
    gl_pad = 2 * LANES - GATE_LORA
    router_w = jnp.pad(moe_w_router[0], ((0, 0), (0, LANES - N_EXPERTS)))
    router_hi = router_w.astype(BF16)
    router_lo = (router_w - router_hi.astype(F32)).astype(BF16)
    return dict(
        qkv_w=na_w_qkv[0].astype(BF16),
        qkv_scale=jnp.concatenate([jnp.full((1, d), NA_HEAD_DIM ** -0.5 * LOG2E, F32), jnp.ones((1, 2 * d), F32)], axis=1),
        na_bias=_na_bias_table(na_rpb[0]),
        na_wo=na_w_o[0].astype(BF16),
        ffn_wg=ffn_w_gate[0].astype(BF16), ffn_wu=ffn_w_up[0].astype(BF16), ffn_wd=ffn_w_down[0].astype(BF16),
        mu=rwkv_mu[0].astype(F32),
        wrkv=rwkv_w_rkv[0].astype(BF16),
        w1=jnp.concatenate([rwkv_w1[0, 0], rwkv_w1[0, 1]], axis=1).astype(BF16),
        w2=jnp.concatenate([rwkv_w2[0, 0], rwkv_w2[0, 1]], axis=0).astype(BF16),
        w0=rwkv_w0[0].astype(F32),
        a1=jnp.concatenate([rwkv_a1[0, 0], rwkv_a1[0, 1]], axis=1).astype(BF16),
        a2=jnp.concatenate([rwkv_a2[0, 0], rwkv_a2[0, 1]], axis=0).astype(BF16),
        a0=rwkv_a0[0].astype(F32),
        g1=jnp.pad(rwkv_g1[0], ((0, 0), (0, gl_pad))).astype(BF16),
        g2=jnp.pad(rwkv_g2[0], ((0, gl_pad), (0, 0))).astype(BF16),
        k_k=vec(rwkv_k_k[0]), k_a=vec(rwkv_k_a[0]), r_k=vec(rwkv_r_k[0]),
        lnx_g=vec(rwkv_lnx_g[0]), lnx_b=vec(rwkv_lnx_b[0]),
        rwkv_wo=rwkv_w_o[0].astype(BF16),
        router_w=jnp.stack([router_hi, router_lo]),
        router_b=jnp.pad(moe_b_router[0], (0, LANES - N_EXPERTS)).reshape(1, LANES).astype(F32),
        moe_wg=moe_w_gate[0].astype(BF16), moe_wu=moe_w_up[0].astype(BF16), moe_wd=moe_w_down[0].astype(BF16),
        ln_mix_g=ln_mix_g.astype(F32), ln_mix_b=ln_mix_b.astype(F32),
        ln_ffn_g=ln_ffn_g.astype(F32), ln_ffn_b=ln_ffn_b.astype(F32),
    )


def _trunk(x3, w):
    batch, seq_len, d = x3.shape
    rows = seq_len // GRID_W
    x = x3.reshape(batch * seq_len, d)
    ln = lambda a, i: a[i].reshape(1, d)

    qkv = _qkv_proj(x, w["qkv_w"], w["qkv_scale"])
    att = _na_attention(qkv, w["na_bias"], batch, rows)
    x = _proj_res_ln(att, w["na_wo"], x, ln(w["ln_mix_g"], 0), ln(w["ln_mix_b"], 0))
    x = _ffn_res_ln(x, w["ffn_wg"], w["ffn_wu"], w["ffn_wd"], ln(w["ln_ffn_g"], 0), ln(w["ln_ffn_b"], 0))

    r, k, kk, v, g, lw0, lw1, ia0, ia1 = _rwkv_proj(x, seq_len, w["mu"], w["wrkv"], w["w1"], w["w2"], w["w0"],
                                                    w["a1"], w["a2"], w["a0"], w["g1"], w["g2"], w["k_k"])
    yf, yb = _rwkv_scan(r, k, kk, v, lw0, lw1, ia0, ia1, w["k_a"], batch, seq_len)
    x = _rwkv_out(yf, yb, r, k, v, g, ia0, ia1, x, w["lnx_g"], w["lnx_b"], w["r_k"], w["k_a"], w["rwkv_wo"],
                  ln(w["ln_mix_g"], 1), ln(w["ln_mix_b"], 1))
    rec, counts = _router(x, w["router_w"], w["router_b"])
    x = _moe_res_ln(x, rec, counts, w["moe_wg"], w["moe_wu"], w["moe_wd"], ln(w["ln_ffn_g"], 1), ln(w["ln_ffn_b"], 1))
    return x.reshape(batch, seq_len, d)


def kernel(x_prompt, x_sample, na_w_qkv, na_rpb, na_w_o, ffn_w_gate, ffn_w_up, ffn_w_down, rwkv_mu, rwkv_w_rkv, rwkv_w0, rwkv_w1, rwkv_w2, rwkv_a0, rwkv_a1, rwkv_a2, rwkv_g1, rwkv_g2, rwkv_k_k, rwkv_k_a, rwkv_r_k, rwkv_lnx_g, rwkv_lnx_b, rwkv_w_o, moe_w_router, moe_b_router, moe_w_gate, moe_w_up, moe_w_down, ln_mix_g, ln_mix_b, ln_ffn_g, ln_ffn_b):
    w = _prepare_weights(na_w_qkv, na_rpb, na_w_o, ffn_w_gate, ffn_w_up, ffn_w_down,
                         rwkv_mu, rwkv_w_rkv, rwkv_w0, rwkv_w1, rwkv_w2, rwkv_a0, rwkv_a1, rwkv_a2,
                         rwkv_g1, rwkv_g2, rwkv_k_k, rwkv_k_a, rwkv_r_k, rwkv_lnx_g, rwkv_lnx_b, rwkv_w_o,
                         moe_w_router, moe_b_router, moe_w_gate, moe_w_up, moe_w_down,
                         ln_mix_g, ln_mix_b, ln_ffn_g, ln_ffn_b)
    return (_trunk(x_prompt, w), _trunk(x_sample, w))
```

```python
import functools
import math

import numpy as np
import jax
import jax.numpy as jnp
from jax import lax
from jax.experimental import pallas as pl
from jax.experimental.pallas import tpu as pltpu

D_MODEL = 1024
DEPTH = 2
GRID_W = 64
NA_HEAD_DIM = 32
NA_HEADS = D_MODEL // NA_HEAD_DIM
NA_ROWS = 8
NA_COLS = 16
RWKV_HEAD_DIM = 64
DECAY_LORA = 64
ICLR_LORA = 64
GATE_LORA = 160
GN_EPS = 64e-5
D_FF = 2816
N_EXPERTS = 8
D_EXPERT = 3584
LN_EPS = 1e-5
ALPHA = (2 * DEPTH) ** 0.25
LOG2E = math.log2(math.e)

LANES = 128
SCAN_CHUNK = 64
SCAN_BLOCK = 256
VMEM_LIMIT = 56 * 1024 * 1024

F32 = jnp.float32
BF16 = jnp.bfloat16


def _params(*sem):
    return pltpu.CompilerParams(dimension_semantics=sem, vmem_limit_bytes=VMEM_LIMIT)


def _dot(a, b):
    return jnp.dot(a, b, preferred_element_type=F32)


def _dot_nt(a, b):
    return lax.dot_general(a, b, (((1,), (1,)), ((), ())), preferred_element_type=F32)


def _dot_tn(a, b):
    return lax.dot_general(a, b, (((0,), (0,)), ((), ())), preferred_element_type=F32)


def _split(a):
    hi = a.astype(BF16)
    lo = (a - hi.astype(F32)).astype(BF16)
    return hi, lo


def _dot_hilo(a, b_exact):
    hi, lo = _split(a)
    return _dot(hi, b_exact) + _dot(lo, b_exact)


def _layer_norm(z, g, b):
    mu = jnp.mean(z, axis=-1, keepdims=True)
    zc = z - mu
    var = jnp.mean(zc * zc, axis=-1, keepdims=True)
    return zc * lax.rsqrt(var + LN_EPS) * g + b


def _sigmoid(z):
    return 1.0 / (1.0 + jnp.exp(-z))


def _head_sum_matrix(head_dim):
    lane = np.arange(LANES)
    return jnp.asarray((lane[:, None] // head_dim) == (lane[None, :] // head_dim), BF16)


def _qkv_kernel(x_ref, w_ref, s_ref, o_ref):
    acc = _dot(x_ref[...].astype(BF16), w_ref[...])
    o_ref[...] = (acc * s_ref[...]).astype(o_ref.dtype)


def _qkv_proj(x, w, colscale, tm=512):
    t, d = x.shape
    n = w.shape[1]
    return pl.pallas_call(
        _qkv_kernel,
        grid=(t // tm,),
        in_specs=[pl.BlockSpec((tm, d), lambda i: (i, 0)),
                  pl.BlockSpec((d, n), lambda i: (0, 0)),
                  pl.BlockSpec((1, n), lambda i: (0, 0))],
        out_specs=pl.BlockSpec((tm, n), lambda i: (i, 0)),
        out_shape=jax.ShapeDtypeStruct((t, n), BF16),
        compiler_params=_params("parallel"),
        name="qkv_proj",
    )(x, w, colscale)


NA_HEADS_PER_GROUP = LANES // NA_HEAD_DIM
NA_GROUPS = D_MODEL // LANES
NA_WIN = NA_ROWS * GRID_W


NA_GROUP_ROWS = NA_HEADS_PER_GROUP * GRID_W
NA_ROW_PAIRS = NA_ROWS // 2
NA_BIAS_PAIRS = 2 * NA_ROWS - 2


def _na_kernel(rows, q_ref, k_ref, v_ref, b_ref, o_ref):
    r = pl.program_id(1)
    off = r - jnp.clip(r - NA_ROWS // 2, 0, rows - NA_ROWS)
    lane_head = lax.broadcasted_iota(jnp.int32, (GRID_W, LANES), 1) // NA_HEAD_DIM
    groups = [slice(g * LANES, (g + 1) * LANES) for g in range(NA_GROUPS)]

    scores = []
    for cols in groups:
        qg = q_ref[:, cols]
        zero = jnp.zeros_like(qg)
        qm = jnp.concatenate([jnp.where(lane_head == h, qg, zero) for h in range(NA_HEADS_PER_GROUP)], axis=0)
        scores.append(_dot_nt(qm, k_ref[:, cols]))

    probs, denoms = [], []
    for g, s in enumerate(scores):
        hq = pl.ds(g * NA_GROUP_ROWS, NA_GROUP_ROWS)
        s = jnp.concatenate([s[:, m * LANES:(m + 1) * LANES] + b_ref[2 * m - off + NA_ROWS - 1, hq, :]
                             for m in range(NA_ROW_PAIRS)], axis=1)
        p = jnp.exp2(s - jnp.max(s, axis=-1, keepdims=True))
        denoms.append(jnp.sum(p, axis=-1, keepdims=True))
        probs.append(p.astype(BF16))

    for g, cols in enumerate(groups):
        o4 = _dot(probs[g], v_ref[:, cols]) / denoms[g]
        og = jnp.zeros((GRID_W, LANES), F32)
        for h in range(NA_HEADS_PER_GROUP):
            og = og + jnp.where(lane_head == h, o4[h * GRID_W:(h + 1) * GRID_W], 0.0)
        o_ref[:, cols] = og.astype(o_ref.dtype)


def _na_attention(qkv, bias, batch, rows):
    t = qkv.shape[0]
    half = NA_ROWS // 2

    def win_start(r):
        return jnp.clip(r - half, 0, rows - NA_ROWS)

    def q_map(b, r):
        return (b * rows + r, 0)

    def k_map(b, r):
        return ((b * rows + win_start(r)) * GRID_W, D_MODEL)

    def v_map(b, r):
        return ((b * rows + win_start(r)) * GRID_W, 2 * D_MODEL)

    return pl.pallas_call(
        functools.partial(_na_kernel, rows),
        grid=(batch, rows),
        in_specs=[pl.BlockSpec((GRID_W, D_MODEL), q_map),
                  pl.BlockSpec((pl.Element(NA_WIN), pl.Element(D_MODEL)), k_map),
                  pl.BlockSpec((pl.Element(NA_WIN), pl.Element(D_MODEL)), v_map),
                  pl.BlockSpec(bias.shape, lambda b, r: (0, 0, 0), pipeline_mode=pl.Buffered(1))],
        out_specs=pl.BlockSpec((GRID_W, D_MODEL), q_map),
        out_shape=jax.ShapeDtypeStruct((t, D_MODEL), BF16),
        compiler_params=_params("parallel", "arbitrary"),
        name="na_attention",
    )(qkv, qkv, qkv, bias)


def _na_bias_kernel(rpb_ref, oh_ref, mask_ref, o_ref):
    x = rpb_ref[...]
    hi = x.astype(BF16)
    r1 = x - hi.astype(F32)
    mid = r1.astype(BF16)
    lo = (r1 - mid.astype(F32)).astype(BF16)
    oh = oh_ref[...]
    o_ref[...] = (_dot(hi, oh) + _dot(mid, oh) + _dot(lo, oh)) * LOG2E + mask_ref[...]


def _na_bias_table(rpb, tn=1024):
    n_dr, n_dc = 2 * NA_ROWS - 1, 2 * NA_COLS - 1
    c = np.arange(GRID_W)
    q_start = np.clip(c - NA_COLS // 2, 0, GRID_W - NA_COLS)
    kc = np.arange(GRID_W)
    valid = (kc[None, :] >= q_start[:, None]) & (kc[None, :] < q_start[:, None] + NA_COLS)
    dc = kc[None, :] - c[:, None] + NA_COLS - 1
    onehot = (np.arange(LANES)[:, None, None] == dc[None]) & valid[None]
    onehot = jnp.asarray(onehot.reshape(LANES, GRID_W * GRID_W), BF16)
    mask = jnp.asarray(np.where(valid, 0.0, -np.inf).reshape(1, GRID_W * GRID_W), F32)
    rpb2 = jnp.pad(rpb.reshape(NA_HEADS * n_dr, n_dc).astype(F32), ((0, 0), (0, LANES - n_dc)))
    nrow, ncol = rpb2.shape[0], GRID_W * GRID_W
    flat = pl.pallas_call(
        _na_bias_kernel,
        grid=(ncol // tn,),
        in_specs=[pl.BlockSpec((nrow, LANES), lambda j: (0, 0)),
                  pl.BlockSpec((LANES, tn), lambda j: (0, j)),
                  pl.BlockSpec((1, tn), lambda j: (0, j))],
        out_specs=pl.BlockSpec((nrow, tn), lambda j: (0, j)),
        out_shape=jax.ShapeDtypeStruct((nrow, ncol), F32),
        compiler_params=_params("parallel"),
        name="na_bias_table",
    )(rpb2, onehot, mask)
    toe = flat.reshape(NA_HEADS, n_dr, GRID_W, GRID_W)
    pairs = jnp.stack([toe[:, 0:NA_BIAS_PAIRS], toe[:, 1:NA_BIAS_PAIRS + 1]], axis=3)
    return jnp.transpose(pairs, (1, 0, 2, 3, 4)).reshape(NA_BIAS_PAIRS, NA_HEADS * GRID_W, LANES)


def _proj_ln_kernel(a_ref, w_ref, x_ref, g_ref, b_ref, o_ref):
    h = _dot(a_ref[...], w_ref[...])
    o_ref[...] = _layer_norm(ALPHA * x_ref[...] + h, g_ref[...], b_ref[...])


def _proj_res_ln(a, w, x, g, b, tm=512):
    t, d = x.shape
    row = pl.BlockSpec((tm, d), lambda i: (i, 0))
    vec = pl.BlockSpec((1, d), lambda i: (0, 0))
    return pl.pallas_call(
        _proj_ln_kernel,
        grid=(t // tm,),
        in_specs=[row, pl.BlockSpec((d, d), lambda i: (0, 0)), row, vec, vec],
        out_specs=row,
        out_shape=jax.ShapeDtypeStruct((t, d), F32),
        compiler_params=_params("parallel"),
        name="proj_res_ln",
    )(a, w, x, g, b)


def _ffn_kernel(x_ref, wg_ref, wu_ref, wd_ref, g_ref, b_ref, o_ref, xb_ref, acc_ref):
    f = pl.program_id(1)

    @pl.when(f == 0)
    def _():
        xb_ref[...] = x_ref[...].astype(BF16)
        acc_ref[...] = jnp.zeros_like(acc_ref)

    xb = xb_ref[...]
    gate = _dot(xb, wg_ref[...])
    up = _dot(xb, wu_ref[...])
    h = (gate * _sigmoid(gate) * up).astype(BF16)
    acc_ref[...] += _dot(h, wd_ref[...])

    @pl.when(f == pl.num_programs(1) - 1)
    def _():
        o_ref[...] = _layer_norm(ALPHA * x_ref[...] + acc_ref[...], g_ref[...], b_ref[...])


def _ffn_res_ln(x, wg, wu, wd, g, b, tm=512, tf=1408):
    t, d = x.shape
    ff = wg.shape[1]
    row = pl.BlockSpec((tm, d), lambda i, f: (i, 0))
    vec = pl.BlockSpec((1, d), lambda i, f: (0, 0))
    return pl.pallas_call(
        _ffn_kernel,
        grid=(t // tm, ff // tf),
        in_specs=[row,
                  pl.BlockSpec((d, tf), lambda i, f: (0, f)),
                  pl.BlockSpec((d, tf), lambda i, f: (0, f)),
                  pl.BlockSpec((tf, d), lambda i, f: (f, 0)),
                  vec, vec],
        out_specs=row,
        out_shape=jax.ShapeDtypeStruct((t, d), F32),
        scratch_shapes=[pltpu.VMEM((tm, d), BF16), pltpu.VMEM((tm, d), F32)],
        compiler_params=_params("parallel", "arbitrary"),
        name="ffn_res_ln",
    )(x, wg, wu, wd, g, b)


HALO = 8


def _rwkv_proj_kernel(seq_len, x_ref, xp_ref, xn_ref, mu_ref, wrkv_ref, w1_ref, w2_ref, w0_ref,
                      a1_ref, a2_ref, a0_ref, g1_ref, g2_ref, kscale_ref, hs_ref,
                      r_ref, k_ref, kk_ref, v_ref, g_ref, lw0_ref, lw1_ref, ia0_ref, ia1_ref):
    tm = x_ref.shape[0]
    i = pl.program_id(0)
    x = x_ref[...]
    row = lax.broadcasted_iota(jnp.int32, x.shape, 0)
    first_in_seq = (i * tm) % seq_len == 0
    last_in_seq = ((i + 1) * tm) % seq_len == 0
    prev_row = jnp.where(first_in_seq, 0.0, xp_ref[HALO - 1:HALO, :])
    next_row = jnp.where(last_in_seq, 0.0, xn_ref[0:1, :])
    x_prev = jnp.where(row == 0, prev_row, pltpu.roll(x, 1, axis=0))
    x_next = jnp.where(row == tm - 1, next_row, pltpu.roll(x, tm - 1, axis=0))
    xx = 0.5 * (x_prev + x_next) - x

    def mix(j):
        return (x + xx * mu_ref[j:j + 1, :]).astype(BF16)

    r_ref[...] = _dot(mix(0), wrkv_ref[0]).astype(r_ref.dtype)
    kf = _dot(mix(2), wrkv_ref[1])
    k_ref[...] = kf.astype(k_ref.dtype)
    v_ref[...] = _dot(mix(3), wrkv_ref[2]).astype(v_ref.dtype)
    hsum = hs_ref[...]
    for gi in range(D_MODEL // LANES):
        cols = slice(gi * LANES, (gi + 1) * LANES)
        kkr = kf[:, cols] * kscale_ref[:, cols]
        norm = jnp.sqrt(_dot_hilo(kkr * kkr, hsum))
        kk_ref[:, cols] = (kkr / jnp.maximum(norm, 1e-12)).astype(kk_ref.dtype)

    lane = lax.broadcasted_iota(jnp.int32, (tm, LANES), 1)
    dir0 = lane < DECAY_LORA

    lo = jnp.tanh(_dot(mix(1), w1_ref[...]))
    for z, out in enumerate((lw0_ref, lw1_ref)):
        lz = jnp.where(dir0 if z == 0 else ~dir0, lo, 0.0).astype(BF16)
        wl = w0_ref[z:z + 1, :] + _dot(lz, w2_ref[...])
        out[...] = (-math.exp(-0.5)) * _sigmoid(wl)

    al = _dot(mix(4), a1_ref[...])
    for z, out in enumerate((ia0_ref, ia1_ref)):
        az = jnp.where(dir0 if z == 0 else ~dir0, al, 0.0).astype(BF16)
        out[...] = _sigmoid(a0_ref[z:z + 1, :] + _dot(az, a2_ref[...])).astype(out.dtype)

    gl = _sigmoid(_dot(mix(5), g1_ref[...])).astype(BF16)
    g_ref[...] = _dot(gl, g2_ref[...]).astype(g_ref.dtype)


def _rwkv_proj(x, seq_len, mu, wrkv, w1, w2, w0, a1, a2, a0, g1, g2, kscale, tm=256):
    t, d = x.shape
    nb = t // HALO
    step = tm // HALO
    row = pl.BlockSpec((tm, d), lambda i: (i, 0))
    full = lambda shape: pl.BlockSpec(shape, lambda i: (0,) * len(shape))
    hsum = _head_sum_matrix(RWKV_HEAD_DIM)
    outs = [jax.ShapeDtypeStruct((t, d), BF16)] * 5 + [jax.ShapeDtypeStruct((t, d), F32)] * 2 + \
           [jax.ShapeDtypeStruct((t, d), BF16)] * 2
    return pl.pallas_call(
        functools.partial(_rwkv_proj_kernel, seq_len),
        grid=(t // tm,),
        in_specs=[row,
                  pl.BlockSpec((HALO, d), lambda i: (jnp.maximum(i * step - 1, 0), 0)),
                  pl.BlockSpec((HALO, d), lambda i: (jnp.minimum((i + 1) * step, nb - 1), 0)),
                  full(mu.shape), full(wrkv.shape), full(w1.shape), full(w2.shape), full(w0.shape),
                  full(a1.shape), full(a2.shape), full(a0.shape), full(g1.shape), full(g2.shape),
                  full(kscale.shape), full(hsum.shape)],
        out_specs=[row] * 9,
        out_shape=outs,
        compiler_params=_params("parallel"),
        name="rwkv_proj",
    )(x, x, x, mu, wrkv, w1, w2, w0, a1, a2, a0, g1, g2, kscale, hsum)


def _pair_blockdiag(y, lane_a):
    zero = jnp.zeros_like(y)
    return jnp.concatenate([jnp.where(lane_a, y, zero), jnp.where(lane_a, zero, y)], axis=0)


def _pair_matmul(x, y, lane_a):
    return _dot(x.astype(BF16), _pair_blockdiag(y.astype(BF16), lane_a))


def _scan_prepare(chains):
    c = SCAN_CHUNK
    t_idx = lax.broadcasted_iota(jnp.int32, (c, LANES), 0)
    s_idx = lax.broadcasted_iota(jnp.int32, (c, LANES), 1) % c
    lane_a = lax.broadcasted_iota(jnp.int32, (c, LANES), 1) < RWKV_HEAD_DIM
    eye = (s_idx == t_idx).astype(F32)
    tri_r = lax.broadcasted_iota(jnp.int32, (c, c), 0)
    tri_c = lax.broadcasted_iota(jnp.int32, (c, c), 1)
    bd = (lax.broadcasted_iota(jnp.int32, (LANES, LANES), 0) // RWKV_HEAD_DIM ==
          lax.broadcasted_iota(jnp.int32, (LANES, LANES), 1) // RWKV_HEAD_DIM)
    before = {rev: ((s_idx > t_idx) if rev else (s_idx < t_idx)) for rev in (False, True)}
    upto = {rev: before[rev] | (s_idx == t_idx) for rev in (False, True)}
    cum = {rev: ((tri_c >= tri_r) if rev else (tri_c <= tri_r)).astype(BF16) for rev in (False, True)}
    n = len(chains)

    lw_split = [jnp.concatenate(_split(ch[6]), axis=1) for ch in chains]
    lwc2 = [_dot(cum[chains[i][0]], lw_split[i]) for i in range(n)]
    lwc = [x[:, 0:LANES] + x[:, LANES:2 * LANES] for x in lwc2]
    vals = [ch[5] for ch in chains]

    at, rt, bh, kh, w_tot, lhs, rhs = [], [], [], [], [], [], []
    for (rev, ka, r, k, kk, v, lw, ia), lc in zip(chains, lwc):
        last = lc[0:1, :] if rev else lc[c - 1:c, :]
        w_in = jnp.exp(lc)
        w_ex = jnp.exp(lc - lw)
        w_inv = jnp.exp(-lc)
        w_end = jnp.exp(last)
        w_out = w_end * w_inv
        kf = k.astype(F32)
        kkf = kk.astype(F32)
        iaf = ia.astype(F32)
        kd = kf * (1.0 + (iaf - 1.0) * ka)
        bvec = kkf * iaf
        a_t = (-kkf) * w_ex
        r_t = r.astype(F32) * w_in
        at.append(a_t)
        rt.append(r_t)
        bh.append((bvec * w_out).astype(BF16))
        kh.append((kd * w_out).astype(BF16))
        w_tot.append(w_end)
        lhs.append(jnp.concatenate([a_t, r_t], axis=0).astype(BF16))
        rhs.append(jnp.concatenate([_pair_blockdiag((bvec * w_inv).astype(BF16), lane_a),
                                    _pair_blockdiag((kd * w_inv).astype(BF16), lane_a)], axis=0))

    gram = [_dot_nt(lhs[i], rhs[i]) for i in range(n)]
    a_ab = [jnp.where(before[chains[i][0]], gram[i][0:c, 0:LANES], 0.0) for i in range(n)]
    a_ak = [jnp.where(before[chains[i][0]], gram[i][0:c, LANES:2 * LANES], 0.0) for i in range(n)]
    a_rb = [jnp.where(upto[chains[i][0]], gram[i][c:2 * c, 0:LANES], 0.0) for i in range(n)]
    a_rk = [jnp.where(upto[chains[i][0]], gram[i][c:2 * c, LANES:2 * LANES], 0.0) for i in range(n)]

    tinv = [eye + a for a in a_ab]
    pw = [_pair_matmul(a, a, lane_a) for a in a_ab]
    n_sq = int(math.log2(c)) - 1
    for j in range(n_sq):
        final = j == n_sq - 1
        lhs_j = [(tinv[i] if final else jnp.concatenate([tinv[i], pw[i]], axis=0)).astype(BF16) for i in range(n)]
        res = [_dot(lhs_j[i], _pair_blockdiag(pw[i].astype(BF16), lane_a)) for i in range(n)]
        tinv = [tinv[i] + res[i][0:c] for i in range(n)]
        if not final:
            pw = [res[i][c:2 * c] for i in range(n)]

    av = [_pair_matmul(a_ak[i], vals[i], lane_a) for i in range(n)]
    y_loc = [_pair_matmul(a_rk[i], vals[i], lane_a) for i in range(n)]
    pq = [_dot(tinv[i].astype(BF16),
               jnp.concatenate([_pair_blockdiag(at[i].astype(BF16), lane_a),
                                _pair_blockdiag(av[i].astype(BF16), lane_a)], axis=1)) for i in range(n)]
    pqb = [x.astype(BF16) for x in pq]
    mn = [_dot_tn(pqb[i], bh[i]) for i in range(n)]
    kv = [_dot_tn(vals[i], kh[i]) for i in range(n)]
    arb = [_dot(a_rb[i].astype(BF16),
                jnp.concatenate([_pair_blockdiag(pqb[i][:, 0:LANES], lane_a),
                                 _pair_blockdiag(pqb[i][:, LANES:2 * LANES], lane_a)], axis=1)) for i in range(n)]
    out = []
    for i in range(n):
        ry = (rt[i] + arb[i][:, 0:LANES]).astype(BF16)
        yc = arb[i][:, LANES:2 * LANES] + y_loc[i]
        m_state = jnp.where(bd, mn[i][0:LANES], 0.0).astype(BF16)
        n_state = jnp.where(bd, mn[i][LANES:2 * LANES] + kv[i], 0.0)
        out.append((ry, yc, m_state, n_state, w_tot[i]))
    return out


def _scan_kernel(n_pairs, rf_ref, kf_ref, kkf_ref, vf_ref, lwf_ref, iaf_ref,
                 rb_ref, kb_ref, kkb_ref, vb_ref, lwb_ref, iab_ref, ka_ref, yf_ref, yb_ref, s_ref):
    @pl.when(pl.program_id(2) == 0)
    def _():
        s_ref[...] = jnp.zeros_like(s_ref)

    c = SCAN_CHUNK
    n_chunks = SCAN_BLOCK // c
    dirs = ((False, (rf_ref, kf_ref, kkf_ref, vf_ref, lwf_ref, iaf_ref), yf_ref),
            (True, (rb_ref, kb_ref, kkb_ref, vb_ref, lwb_ref, iab_ref), yb_ref))
    chains, where = [], []
    for step in range(n_chunks):
        for p in range(n_pairs):
            cols = slice(p * LANES, (p + 1) * LANES)
            for di, (rev, refs, _) in enumerate(dirs):
                ci = n_chunks - 1 - step if rev else step
                rows = slice(ci * c, (ci + 1) * c)
                chains.append((rev, ka_ref[:, cols]) + tuple(ref[rows, cols] for ref in refs))
                where.append((2 * p + di, di, rows, cols))
    pre = _scan_prepare(chains)
    state = [s_ref[j] for j in range(2 * n_pairs)]
    for (j, di, rows, cols), (ry, yc, m_state, n_state, w_total) in zip(where, pre):
        s = state[j]
        sb = s.astype(BF16)
        dirs[di][2][rows, cols] = _dot_nt(ry, sb) + yc
        state[j] = s * w_total + _dot(sb, m_state) + n_state
    for j in range(2 * n_pairs):
        s_ref[j] = state[j]


def _rwkv_scan(r, k, kk, v, lw0, lw1, ia0, ia1, ka, batch, seq_len, n_pairs=4):
    t, d = r.shape
    ns = seq_len // SCAN_BLOCK
    width = n_pairs * LANES
    fwd = pl.BlockSpec((SCAN_BLOCK, width), lambda b, h, s: (b * ns + s, h))
    bwd = pl.BlockSpec((SCAN_BLOCK, width), lambda b, h, s: (b * ns + ns - 1 - s, h))
    vec = pl.BlockSpec((1, width), lambda b, h, s: (0, h))
    return pl.pallas_call(
        functools.partial(_scan_kernel, n_pairs),
        grid=(batch, d // width, ns),
        in_specs=[fwd] * 6 + [bwd] * 6 + [vec],
        out_specs=[fwd, bwd],
        out_shape=[jax.ShapeDtypeStruct((t, d), F32)] * 2,
        scratch_shapes=[pltpu.VMEM((2 * n_pairs, LANES, LANES), F32)],
        compiler_params=_params("parallel", "parallel", "arbitrary"),
        name="rwkv_scan",
    )(r, k, kk, v, lw0, ia0, r, k, kk, v, lw1, ia1, ka)


def _rwkv_out_kernel(yf_ref, yb_ref, r_ref, k_ref, v_ref, g_ref, ia0_ref, ia1_ref, x_ref,
                     lg_ref, lb_ref, rk_ref, ka_ref, hs_ref, wo_ref, ng_ref, nb_ref, o_ref, z_ref):
    hsum = hs_ref[...]
    inv = 1.0 / RWKV_HEAD_DIM
    for gi in range(D_MODEL // LANES):
        cols = slice(gi * LANES, (gi + 1) * LANES)
        y = yf_ref[:, cols] + yb_ref[:, cols]
        yc = y - _dot_hilo(y, hsum) * inv
        var = _dot_hilo(yc * yc, hsum) * inv
        yn = yc * lax.rsqrt(var + GN_EPS) * lg_ref[:, cols] + lb_ref[:, cols]
        rr = r_ref[:, cols].astype(F32)
        kk = k_ref[:, cols].astype(F32)
        ia = ia0_ref[:, cols].astype(F32) + ia1_ref[:, cols].astype(F32)
        ksum = kk * (2.0 + (ia - 2.0) * ka_ref[:, cols])
        bonus = _dot_hilo(rr * ksum * rk_ref[:, cols], hsum) * v_ref[:, cols].astype(F32)
        z_ref[:, cols] = ((yn + bonus) * g_ref[:, cols].astype(F32)).astype(BF16)
    h = _dot(z_ref[...], wo_ref[...])
    o_ref[...] = _layer_norm(ALPHA * x_ref[...] + h, ng_ref[...], nb_ref[...])


def _rwkv_out(yf, yb, r, k, v, g, ia0, ia1, x, lnx_g, lnx_b, r_k, k_a, wo, ng, nb, tm=256):
    t, d = x.shape
    row = pl.BlockSpec((tm, d), lambda i: (i, 0))
    vec = pl.BlockSpec((1, d), lambda i: (0, 0))
    return pl.pallas_call(
        _rwkv_out_kernel,
        grid=(t // tm,),
        in_specs=[row] * 9 + [vec] * 4 + [pl.BlockSpec((LANES, LANES), lambda i: (0, 0)),
                                          pl.BlockSpec((d, d), lambda i: (0, 0)), vec, vec],
        out_specs=row,
        out_shape=jax.ShapeDtypeStruct((t, d), F32),
        scratch_shapes=[pltpu.VMEM((tm, d), BF16)],
        compiler_params=_params("parallel"),
        name="rwkv_out",
    )(yf, yb, r, k, v, g, ia0, ia1, x, lnx_g, lnx_b, r_k, k_a, _head_sum_matrix(RWKV_HEAD_DIM), wo, ng, nb)


REC_E1, REC_E2, REC_R1, REC_R2, REC_G1, REC_G2 = range(6)


def _router_kernel(x_ref, w_ref, b_ref, tri_ref, rec_ref, cnt_ref, run_ref):
    @pl.when(pl.program_id(0) == 0)
    def _():
        run_ref[...] = jnp.zeros_like(run_ref)

    xh, xl = _split(x_ref[...])
    wh = w_ref[0]
    wl = w_ref[1]
    logits = _dot(xh, wh) + (_dot(xh, wl) + _dot(xl, wh)) + b_ref[...]
    lane = lax.broadcasted_iota(jnp.int32, logits.shape, 1)
    logits = jnp.where(lane < N_EXPERTS, logits, -jnp.inf)
    m1 = jnp.max(logits, axis=-1, keepdims=True)
    i1 = jnp.min(jnp.where(logits == m1, lane, LANES), axis=-1, keepdims=True)
    rest = jnp.where(lane == i1, -jnp.inf, logits)
    m2 = jnp.max(rest, axis=-1, keepdims=True)
    i2 = jnp.min(jnp.where(rest == m2, lane, LANES), axis=-1, keepdims=True)
    e2 = jnp.exp(m2 - m1)
    g1 = 1.0 / (1.0 + e2)
    g2 = e2 / (1.0 + e2)

    hot1 = lane == i1
    hot2 = lane == i2
    oh1 = hot1.astype(BF16)
    oh2 = hot2.astype(BF16)
    tri = tri_ref[...]
    run = run_ref[...]
    n1 = jnp.sum(oh1.astype(F32), axis=0, keepdims=True)
    n2 = jnp.sum(oh2.astype(F32), axis=0, keepdims=True)
    before1 = run + _dot(tri, oh1)
    before2 = run + n1 + _dot(tri, oh2)
    r1 = jnp.sum(jnp.where(hot1, before1, 0.0), axis=-1, keepdims=True)
    r2 = jnp.sum(jnp.where(hot2, before2, 0.0), axis=-1, keepdims=True)
    run = run + n1 + n2
    run_ref[...] = run
    cnt_ref[...] = run

    rec = jnp.zeros(logits.shape, F32)
    for idx, val in ((REC_E1, i1.astype(F32)), (REC_E2, i2.astype(F32)), (REC_R1, r1), (REC_R2, r2),
                     (REC_G1, g1), (REC_G2, g2)):
        rec = jnp.where(lane == idx, val, rec)
    rec_ref[...] = rec


def _router(x, w_hilo, bias, tm=512):
    t, d = x.shape
    tri = jnp.asarray(np.tril(np.ones((tm, tm), np.float32), -1), BF16)
    return pl.pallas_call(
        _router_kernel,
        grid=(t // tm,),
        in_specs=[pl.BlockSpec((tm, d), lambda i: (i, 0)),
                  pl.BlockSpec((2, d, LANES), lambda i: (0, 0, 0)),
                  pl.BlockSpec((1, LANES), lambda i: (0, 0)),
                  pl.BlockSpec((tm, tm), lambda i: (0, 0))],
        out_specs=[pl.BlockSpec((tm, LANES), lambda i: (i, 0)),
                   pl.BlockSpec((1, LANES), lambda i: (0, 0))],
        out_shape=[jax.ShapeDtypeStruct((t, LANES), F32), jax.ShapeDtypeStruct((1, LANES), F32)],
        scratch_shapes=[pltpu.VMEM((1, LANES), F32)],
        compiler_params=_params("arbitrary"),
        name="moe_router",
    )(x, w_hilo, bias, tri)


EXPERT_TILE = 1024
ROUTE_TILE = 512
TOKEN_ROWS = D_MODEL // LANES


def _token_copy(src, src_tok, dst, dst_tok, sem):
    def tile(tok):
        return pl.ds(pl.multiple_of(tok * TOKEN_ROWS, TOKEN_ROWS), TOKEN_ROWS)
    return pltpu.make_async_copy(src.at[tile(src_tok)], dst.at[tile(dst_tok)], sem)


def _to_token_tiles(ref, n_tok, value):
    for j in range(TOKEN_ROWS):
        ref[pl.ds(j, n_tok, stride=TOKEN_ROWS), :] = value[:, j * LANES:(j + 1) * LANES]


def _from_token_tiles(ref, n_tok, j):
    return ref[pl.ds(j, n_tok, stride=TOKEN_ROWS), :]


def _dispatch_kernel(pos_ref, x_ref, xs_in, xs_hbm, xt_ref, sem):
    del xs_in
    _to_token_tiles(xt_ref, ROUTE_TILE, x_ref[...])

    def copies(t):
        return (_token_copy(xt_ref, t, xs_hbm, pos_ref[0, 0, t], sem),
                _token_copy(xt_ref, t, xs_hbm, pos_ref[0, 0, ROUTE_TILE + t], sem))

    def start(t, carry):
        for cp in copies(t):
            cp.start()
        return carry

    def wait(t, carry):
        for cp in copies(t):
            cp.wait()
        return carry

    lax.fori_loop(0, ROUTE_TILE, start, 0)
    lax.fori_loop(0, ROUTE_TILE, wait, 0)


def _moe_dispatch(x, pos, n_rows):
    t, d = x.shape
    return pl.pallas_call(
        _dispatch_kernel,
        grid=(t // ROUTE_TILE,),
        in_specs=[pl.BlockSpec((1, 1, 2 * ROUTE_TILE), lambda i: (i, 0, 0), memory_space=pltpu.SMEM),
                  pl.BlockSpec((ROUTE_TILE, d), lambda i: (i, 0)),
                  pl.BlockSpec(memory_space=pl.ANY)],
        out_specs=pl.BlockSpec(memory_space=pl.ANY),
        out_shape=jax.ShapeDtypeStruct((n_rows * TOKEN_ROWS, LANES), F32),
        scratch_shapes=[pltpu.VMEM((ROUTE_TILE * TOKEN_ROWS, LANES), F32), pltpu.SemaphoreType.DMA(())],
        input_output_aliases={2: 0},
        compiler_params=_params("arbitrary"),
        name="moe_dispatch",
    )(pos, x, jnp.zeros((n_rows * TOKEN_ROWS, LANES), F32))


def _experts_kernel(te_ref, na_ref, x_ref, wg_ref, wu_ref, wd_ref, o_ref, xb_ref, acc_ref):
    i = pl.program_id(0)
    f = pl.program_id(1)
    active = i < na_ref[0]

    @pl.when(active & (f == 0))
    def _():
        for j in range(TOKEN_ROWS):
            xb_ref[:, j * LANES:(j + 1) * LANES] = _from_token_tiles(x_ref, EXPERT_TILE, j).astype(BF16)
        acc_ref[...] = jnp.zeros_like(acc_ref)

    @pl.when(active)
    def _():
        xb = xb_ref[...]
        gate = _dot(xb, wg_ref[...])
        up = _dot(xb, wu_ref[...])
        h = (gate * _sigmoid(gate) * up).astype(BF16)
        acc_ref[...] += _dot(h, wd_ref[...])

    @pl.when(f == pl.num_programs(1) - 1)
    def _():
        _to_token_tiles(o_ref, EXPERT_TILE, jnp.where(active, acc_ref[...], 0.0))


def _moe_experts(xs, tile_expert, n_active, wg, wu, wd, tf=512):
    d = D_MODEL
    n_tiles = xs.shape[0] // (EXPERT_TILE * TOKEN_ROWS)
    fe = wg.shape[2]
    row = pl.BlockSpec((EXPERT_TILE * TOKEN_ROWS, LANES), lambda i, f, te, na: (i, 0))
    return pl.pallas_call(
        _experts_kernel,
        grid_spec=pltpu.PrefetchScalarGridSpec(
            num_scalar_prefetch=2,
            grid=(n_tiles, fe // tf),
            in_specs=[row,
                      pl.BlockSpec((None, d, tf), lambda i, f, te, na: (te[i], 0, f)),
                      pl.BlockSpec((None, d, tf), lambda i, f, te, na: (te[i], 0, f)),
                      pl.BlockSpec((None, tf, d), lambda i, f, te, na: (te[i], f, 0))],
            out_specs=row,
            scratch_shapes=[pltpu.VMEM((EXPERT_TILE, d), BF16), pltpu.VMEM((EXPERT_TILE, d), F32)]),
        out_shape=jax.ShapeDtypeStruct(xs.shape, F32),
        compiler_params=_params("parallel", "arbitrary"),
        name="moe_experts",
    )(tile_expert, n_active, xs, wg, wu, wd)


def _combine_kernel(pos_ref, nxt_ref, ys_hbm, x_ref, rec_ref, g_ref, b_ref, o_ref, y_ref, sem):
    i = pl.program_id(0)
    n = pl.num_programs(0)
    slot = i % 2

    def copies(p_ref, s, t):
        return (_token_copy(ys_hbm, p_ref[0, 0, t], y_ref.at[s, 0], t, sem.at[s]),
                _token_copy(ys_hbm, p_ref[0, 0, ROUTE_TILE + t], y_ref.at[s, 1], t, sem.at[s]))

    def start_all(p_ref, s):
        def body(t, carry):
            for cp in copies(p_ref, s, t):
                cp.start()
            return carry
        lax.fori_loop(0, ROUTE_TILE, body, 0)

    @pl.when(i == 0)
    def _():
        start_all(pos_ref, slot)

    @pl.when(i + 1 < n)
    def _():
        start_all(nxt_ref, 1 - slot)

    def wait_body(t, carry):
        for cp in copies(pos_ref, slot, t):
            cp.wait()
        return carry
    lax.fori_loop(0, ROUTE_TILE, wait_body, 0)

    rec = rec_ref[...]
    lane = lax.broadcasted_iota(jnp.int32, rec.shape, 1)
    g1 = jnp.sum(jnp.where(lane == REC_G1, rec, 0.0), axis=-1, keepdims=True)
    g2 = jnp.sum(jnp.where(lane == REC_G2, rec, 0.0), axis=-1, keepdims=True)
    y1_ref = y_ref.at[slot, 0]
    y2_ref = y_ref.at[slot, 1]
    f = jnp.concatenate([g1 * _from_token_tiles(y1_ref, ROUTE_TILE, j) + g2 * _from_token_tiles(y2_ref, ROUTE_TILE, j)
                         for j in range(TOKEN_ROWS)], axis=1)
    o_ref[...] = _layer_norm(ALPHA * x_ref[...] + f, g_ref[...], b_ref[...])


def _moe_combine_ln(ys, pos, x, rec, g, b):
    t, d = x.shape
    n = t // ROUTE_TILE
    row = pl.BlockSpec((ROUTE_TILE, d), lambda i: (i, 0))
    vec = pl.BlockSpec((1, d), lambda i: (0, 0))
    return pl.pallas_call(
        _combine_kernel,
        grid=(n,),
        in_specs=[pl.BlockSpec((1, 1, 2 * ROUTE_TILE), lambda i: (i, 0, 0), memory_space=pltpu.SMEM),
                  pl.BlockSpec((1, 1, 2 * ROUTE_TILE), lambda i: (jnp.minimum(i + 1, n - 1), 0, 0),
                               memory_space=pltpu.SMEM),
                  pl.BlockSpec(memory_space=pl.ANY),
                  row,
                  pl.BlockSpec((ROUTE_TILE, LANES), lambda i: (i, 0)),
                  vec, vec],
        out_specs=row,
        out_shape=jax.ShapeDtypeStruct((t, d), F32),
        scratch_shapes=[pltpu.VMEM((2, 2, ROUTE_TILE * TOKEN_ROWS, LANES), F32), pltpu.SemaphoreType.DMA((2,))],
        compiler_params=_params("arbitrary"),
        name="moe_combine_ln",
    )(pos, pos, ys, x, rec, g, b)


def _moe_res_ln(x, rec, counts, wg, wu, wd, g, b):
    t, d = x.shape
    n_tiles = 2 * t // EXPERT_TILE + N_EXPERTS
    cnt = counts[0, :N_EXPERTS].astype(jnp.int32)
    padded = (cnt + EXPERT_TILE - 1) // EXPERT_TILE * EXPERT_TILE
    ends = jnp.cumsum(padded)
    starts = ends - padded
    e1 = rec[:, REC_E1].astype(jnp.int32)
    e2 = rec[:, REC_E2].astype(jnp.int32)
    pos1 = starts[e1] + rec[:, REC_R1].astype(jnp.int32)
    pos2 = starts[e2] + rec[:, REC_R2].astype(jnp.int32)
    pos = jnp.concatenate([pos1.reshape(-1, 1, ROUTE_TILE), pos2.reshape(-1, 1, ROUTE_TILE)], axis=2)
    tile_start = jnp.arange(n_tiles, dtype=jnp.int32) * EXPERT_TILE
    tile_expert = jnp.minimum(jnp.sum(tile_start[:, None] >= ends[None, :], axis=1), N_EXPERTS - 1).astype(jnp.int32)
    n_active = (ends[-1:] // EXPERT_TILE).astype(jnp.int32)
    last_expert = tile_expert[jnp.maximum(n_active[0] - 1, 0)]
    tile_expert = jnp.where(jnp.arange(n_tiles) < n_active[0], tile_expert, last_expert)

    xs = _moe_dispatch(x, pos, n_tiles * EXPERT_TILE)
    ys = _moe_experts(xs, tile_expert, n_active, wg, wu, wd)
    return _moe_combine_ln(ys, pos, x, rec, g, b)


def _prepare_weights(na_w_qkv, na_rpb, na_w_o, ffn_w_gate, ffn_w_up, ffn_w_down,
                     rwkv_mu, rwkv_w_rkv, rwkv_w0, rwkv_w1, rwkv_w2, rwkv_a0, rwkv_a1, rwkv_a2,
                     rwkv_g1, rwkv_g2, rwkv_k_k, rwkv_k_a, rwkv_r_k, rwkv_lnx_g, rwkv_lnx_b, rwkv_w_o,
                     moe_w_router, moe_b_router, moe_w_gate, moe_w_up, moe_w_down,
                     ln_mix_g, ln_mix_b, ln_ffn_g, ln_ffn_b):
    d = D_MODEL
    vec = lambda a: a.reshape(1, d).astype(F32)
    gl_pad = 2 * LANES - GATE_LORA
    router_w = jnp.pad(moe_w_router[0], ((0, 0), (0, LANES - N_EXPERTS)))
    router_hi = router_w.astype(BF16)
    router_lo = (router_w - router_hi.astype(F32)).astype(BF16)
    return dict(
        qkv_w=na_w_qkv[0].astype(BF16),
        qkv_scale=jnp.concatenate([jnp.full((1, d), NA_HEAD_DIM ** -0.5 * LOG2E, F32), jnp.ones((1, 2 * d), F32)], axis=1),
        na_bias=_na_bias_table(na_rpb[0]),
        na_wo=na_w_o[0].astype(BF16),
        ffn_wg=ffn_w_gate[0].astype(BF16), ffn_wu=ffn_w_up[0].astype(BF16), ffn_wd=ffn_w_down[0].astype(BF16),
        mu=rwkv_mu[0].astype(F32),
        wrkv=rwkv_w_rkv[0].astype(BF16),
        w1=jnp.concatenate([rwkv_w1[0, 0], rwkv_w1[0, 1]], axis=1).astype(BF16),
        w2=jnp.concatenate([rwkv_w2[0, 0], rwkv_w2[0, 1]], axis=0).astype(BF16),
        w0=rwkv_w0[0].astype(F32),
        a1=jnp.concatenate([rwkv_a1[0, 0], rwkv_a1[0, 1]], axis=1).astype(BF16),
        a2=jnp.concatenate([rwkv_a2[0, 0], rwkv_a2[0, 1]], axis=0).astype(BF16),
        a0=rwkv_a0[0].astype(F32),
        g1=jnp.pad(rwkv_g1[0], ((0, 0), (0, gl_pad))).astype(BF16),
        g2=jnp.pad(rwkv_g2[0], ((0, gl_pad), (0, 0))).astype(BF16),
        k_k=vec(rwkv_k_k[0]), k_a=vec(rwkv_k_a[0]), r_k=vec(rwkv_r_k[0]),
        lnx_g=vec(rwkv_lnx_g[0]), lnx_b=vec(rwkv_lnx_b[0]),
        rwkv_wo=rwkv_w_o[0].astype(BF16),
        router_w=jnp.stack([router_hi, router_lo]),
        router_b=jnp.pad(moe_b_router[0], (0, LANES - N_EXPERTS)).reshape(1, LANES).astype(F32),
        moe_wg=moe_w_gate[0].astype(BF16), moe_wu=moe_w_up[0].astype(BF16), moe_wd=moe_w_down[0].astype(BF16),
        ln_mix_g=ln_mix_g.astype(F32), ln_mix_b=ln_mix_b.astype(F32),
        ln_ffn_g=ln_ffn_g.astype(F32), ln_ffn_b=ln_ffn_b.astype(F32),
    )


def _trunk(x3, w):
    batch, seq_len, d = x3.shape
    rows = seq_len // GRID_W
    x = x3.reshape(batch * seq_len, d)
    ln = lambda a, i: a[i].reshape(1, d)

    qkv = _qkv_proj(x, w["qkv_w"], w["qkv_scale"])
    att = _na_attention(qkv, w["na_bias"], batch, rows)
    x = _proj_res_ln(att, w["na_wo"], x, ln(w["ln_mix_g"], 0), ln(w["ln_mix_b"], 0))
    x = _ffn_res_ln(x, w["ffn_wg"], w["ffn_wu"], w["ffn_wd"], ln(w["ln_ffn_g"], 0), ln(w["ln_ffn_b"], 0))

    r, k, kk, v, g, lw0, lw1, ia0, ia1 = _rwkv_proj(x, seq_len, w["mu"], w["wrkv"], w["w1"], w["w2"], w["w0"],
                                                    w["a1"], w["a2"], w["a0"], w["g1"], w["g2"], w["k_k"])
    yf, yb = _rwkv_scan(r, k, kk, v, lw0, lw1, ia0, ia1, w["k_a"], batch, seq_len)
    x = _rwkv_out(yf, yb, r, k, v, g, ia0, ia1, x, w["lnx_g"], w["lnx_b"], w["r_k"], w["k_a"], w["rwkv_wo"],
                  ln(w["ln_mix_g"], 1), ln(w["ln_mix_b"], 1))
    rec, counts = _router(x, w["router_w"], w["router_b"])
    x = _moe_res_ln(x, rec, counts, w["moe_wg"], w["moe_wu"], w["moe_wd"], ln(w["ln_ffn_g"], 1), ln(w["ln_ffn_b"], 1))
    return x.reshape(batch, seq_len, d)


def kernel(x_prompt, x_sample, na_w_qkv, na_rpb, na_w_o, ffn_w_gate, ffn_w_up, ffn_w_down, rwkv_mu, rwkv_w_rkv, rwkv_w0, rwkv_w1, rwkv_w2, rwkv_a0, rwkv_a1, rwkv_a2, rwkv_g1, rwkv_g2, rwkv_k_k, rwkv_k_a, rwkv_r_k, rwkv_lnx_g, rwkv_lnx_b, rwkv_w_o, moe_w_router, moe_b_router, moe_w_gate, moe_w_up, moe_w_down, ln_mix_g, ln_mix_b, ln_ffn_g, ln_ffn_b):
    w = _prepare_weights(na_w_qkv, na_rpb, na_w_o, ffn_w_gate, ffn_w_up, ffn_w_down,
                         rwkv_mu, rwkv_w_rkv, rwkv_w0, rwkv_w1, rwkv_w2, rwkv_a0, rwkv_a1, rwkv_a2,
                         rwkv_g1, rwkv_g2, rwkv_k_k, rwkv_k_a, rwkv_r_k, rwkv_lnx_g, rwkv_lnx_b, rwkv_w_o,
                         moe_w_router, moe_b_router, moe_w_gate, moe_w_up, moe_w_down,
                         ln_mix_g, ln_mix_b, ln_ffn_g, ln_ffn_b)
    return (_trunk(x_prompt, w), _trunk(x_sample, w))
```

```python
import functools
import math

import numpy as np
import jax
import jax.numpy as jnp
from jax import lax
from jax.experimental import pallas as pl
from jax.experimental.pallas import tpu as pltpu

D_MODEL = 1024
DEPTH = 2
GRID_W = 64
NA_HEAD_DIM = 32
NA_HEADS = D_MODEL // NA_HEAD_DIM
NA_ROWS = 8
NA_COLS = 16
RWKV_HEAD_DIM = 64
DECAY_LORA = 64
ICLR_LORA = 64
GATE_LORA = 160
GN_EPS = 64e-5
D_FF = 2816
N_EXPERTS = 8
D_EXPERT = 3584
LN_EPS = 1e-5
ALPHA = (2 * DEPTH) ** 0.25
LOG2E = math.log2(math.e)

LANES = 128
SCAN_CHUNK = 64
SCAN_BLOCK = 256
VMEM_LIMIT = 56 * 1024 * 1024

F32 = jnp.float32
BF16 = jnp.bfloat16


def _params(*sem):
    return pltpu.CompilerParams(dimension_semantics=sem, vmem_limit_bytes=VMEM_LIMIT)


def _dot(a, b):
    return jnp.dot(a, b, preferred_element_type=F32)


def _dot_nt(a, b):
    return lax.dot_general(a, b, (((1,), (1,)), ((), ())), preferred_element_type=F32)


def _dot_tn(a, b):
    return lax.dot_general(a, b, (((0,), (0,)), ((), ())), preferred_element_type=F32)


def _split(a):
    hi = a.astype(BF16)
    lo = (a - hi.astype(F32)).astype(BF16)
    return hi, lo


def _layer_norm(z, g, b):
    mu = jnp.mean(z, axis=-1, keepdims=True)
    zc = z - mu
    var = jnp.mean(zc * zc, axis=-1, keepdims=True)
    return zc * lax.rsqrt(var + LN_EPS) * g + b


def _sigmoid(z):
    return 1.0 / (1.0 + jnp.exp(-z))


def _head_sum_matrix(head_dim):
    lane = np.arange(LANES)
    return jnp.asarray((lane[:, None] // head_dim) == (lane[None, :] // head_dim), BF16)


def _qkv_kernel(x_ref, w_ref, s_ref, o_ref):
    acc = _dot(x_ref[...].astype(BF16), w_ref[...])
    o_ref[...] = (acc * s_ref[...]).astype(o_ref.dtype)


def _qkv_proj(x, w, colscale, tm=512):
    t, d = x.shape
    n = w.shape[1]
    return pl.pallas_call(
        _qkv_kernel,
        grid=(t // tm,),
        in_specs=[pl.BlockSpec((tm, d), lambda i: (i, 0)),
                  pl.BlockSpec((d, n), lambda i: (0, 0)),
                  pl.BlockSpec((1, n), lambda i: (0, 0))],
        out_specs=pl.BlockSpec((tm, n), lambda i: (i, 0)),
        out_shape=jax.ShapeDtypeStruct((t, n), BF16),
        compiler_params=_params("parallel"),
        name="qkv_proj",
    )(x, w, colscale)


NA_HEADS_PER_GROUP = LANES // NA_HEAD_DIM
NA_GROUPS = D_MODEL // LANES
NA_WIN = NA_ROWS * GRID_W


NA_GROUP_ROWS = NA_HEADS_PER_GROUP * GRID_W
NA_ROW_PAIRS = NA_ROWS // 2
NA_BIAS_PAIRS = 2 * NA_ROWS - 2


def _na_kernel(rows, q_ref, k_ref, v_ref, b_ref, o_ref):
    r = pl.program_id(1)
    off = r - jnp.clip(r - NA_ROWS // 2, 0, rows - NA_ROWS)
    lane_head = lax.broadcasted_iota(jnp.int32, (GRID_W, LANES), 1) // NA_HEAD_DIM
    groups = [slice(g * LANES, (g + 1) * LANES) for g in range(NA_GROUPS)]

    scores = []
    for cols in groups:
        qg = q_ref[:, cols]
        zero = jnp.zeros_like(qg)
        qm = jnp.concatenate([jnp.where(lane_head == h, qg, zero) for h in range(NA_HEADS_PER_GROUP)], axis=0)
        scores.append(_dot_nt(qm, k_ref[:, cols]))

    probs, denoms = [], []
    for g, s in enumerate(scores):
        hq = pl.ds(g * NA_GROUP_ROWS, NA_GROUP_ROWS)
        s = jnp.concatenate([s[:, m * LANES:(m + 1) * LANES] + b_ref[2 * m - off + NA_ROWS - 1, hq, :]
                             for m in range(NA_ROW_PAIRS)], axis=1)
        p = jnp.exp2(s - jnp.max(s, axis=-1, keepdims=True))
        denoms.append(jnp.sum(p, axis=-1, keepdims=True))
        probs.append(p.astype(BF16))

    for g, cols in enumerate(groups):
        o4 = _dot(probs[g], v_ref[:, cols]) / denoms[g]
        og = jnp.zeros((GRID_W, LANES), F32)
        for h in range(NA_HEADS_PER_GROUP):
            og = og + jnp.where(lane_head == h, o4[h * GRID_W:(h + 1) * GRID_W], 0.0)
        o_ref[:, cols] = og.astype(o_ref.dtype)


def _na_attention(qkv, bias, batch, rows):
    t = qkv.shape[0]
    half = NA_ROWS // 2

    def win_start(r):
        return jnp.clip(r - half, 0, rows - NA_ROWS)

    def q_map(b, r):
        return (b * rows + r, 0)

    def k_map(b, r):
        return ((b * rows + win_start(r)) * GRID_W, D_MODEL)

    def v_map(b, r):
        return ((b * rows + win_start(r)) * GRID_W, 2 * D_MODEL)

    return pl.pallas_call(
        functools.partial(_na_kernel, rows),
        grid=(batch, rows),
        in_specs=[pl.BlockSpec((GRID_W, D_MODEL), q_map),
                  pl.BlockSpec((pl.Element(NA_WIN), pl.Element(D_MODEL)), k_map),
                  pl.BlockSpec((pl.Element(NA_WIN), pl.Element(D_MODEL)), v_map),
                  pl.BlockSpec(bias.shape, lambda b, r: (0, 0, 0), pipeline_mode=pl.Buffered(1))],
        out_specs=pl.BlockSpec((GRID_W, D_MODEL), q_map),
        out_shape=jax.ShapeDtypeStruct((t, D_MODEL), BF16),
        compiler_params=_params("parallel", "arbitrary"),
        name="na_attention",
    )(qkv, qkv, qkv, bias)


def _na_bias_kernel(rpb_ref, oh_ref, mask_ref, o_ref):
    x = rpb_ref[...]
    hi = x.astype(BF16)
    r1 = x - hi.astype(F32)
    mid = r1.astype(BF16)
    lo = (r1 - mid.astype(F32)).astype(BF16)
    oh = oh_ref[...]
    o_ref[...] = (_dot(hi, oh) + _dot(mid, oh) + _dot(lo, oh)) * LOG2E + mask_ref[...]


def _na_bias_table(rpb, tn=1024):
    n_dr, n_dc = 2 * NA_ROWS - 1, 2 * NA_COLS - 1
    c = np.arange(GRID_W)
    q_start = np.clip(c - NA_COLS // 2, 0, GRID_W - NA_COLS)
    kc = np.arange(GRID_W)
    valid = (kc[None, :] >= q_start[:, None]) & (kc[None, :] < q_start[:, None] + NA_COLS)
    dc = kc[None, :] - c[:, None] + NA_COLS - 1
    onehot = (np.arange(LANES)[:, None, None] == dc[None]) & valid[None]
    onehot = jnp.asarray(onehot.reshape(LANES, GRID_W * GRID_W), BF16)
    mask = jnp.asarray(np.where(valid, 0.0, -np.inf).reshape(1, GRID_W * GRID_W), F32)
    rpb2 = jnp.pad(rpb.reshape(NA_HEADS * n_dr, n_dc).astype(F32), ((0, 0), (0, LANES - n_dc)))
    nrow, ncol = rpb2.shape[0], GRID_W * GRID_W
    flat = pl.pallas_call(
        _na_bias_kernel,
        grid=(ncol // tn,),
        in_specs=[pl.BlockSpec((nrow, LANES), lambda j: (0, 0)),
                  pl.BlockSpec((LANES, tn), lambda j: (0, j)),
                  pl.BlockSpec((1, tn), lambda j: (0, j))],
        out_specs=pl.BlockSpec((nrow, tn), lambda j: (0, j)),
        out_shape=jax.ShapeDtypeStruct((nrow, ncol), F32),
        compiler_params=_params("parallel"),
        name="na_bias_table",
    )(rpb2, onehot, mask)
    toe = flat.reshape(NA_HEADS, n_dr, GRID_W, GRID_W)
    pairs = jnp.stack([toe[:, 0:NA_BIAS_PAIRS], toe[:, 1:NA_BIAS_PAIRS + 1]], axis=3)
    return jnp.transpose(pairs, (1, 0, 2, 3, 4)).reshape(NA_BIAS_PAIRS, NA_HEADS * GRID_W, LANES)


def _proj_ln_kernel(a_ref, w_ref, x_ref, g_ref, b_ref, o_ref):
    h = _dot(a_ref[...], w_ref[...])
    o_ref[...] = _layer_norm(ALPHA * x_ref[...] + h, g_ref[...], b_ref[...])


def _proj_res_ln(a, w, x, g, b, tm=512):
    t, d = x.shape
    row = pl.BlockSpec((tm, d), lambda i: (i, 0))
    vec = pl.BlockSpec((1, d), lambda i: (0, 0))
    return pl.pallas_call(
        _proj_ln_kernel,
        grid=(t // tm,),
        in_specs=[row, pl.BlockSpec((d, d), lambda i: (0, 0)), row, vec, vec],
        out_specs=row,
        out_shape=jax.ShapeDtypeStruct((t, d), F32),
        compiler_params=_params("parallel"),
        name="proj_res_ln",
    )(a, w, x, g, b)


def _ffn_kernel(x_ref, wg_ref, wu_ref, wd_ref, g_ref, b_ref, o_ref, xb_ref, acc_ref):
    f = pl.program_id(1)

    @pl.when(f == 0)
    def _():
        xb_ref[...] = x_ref[...].astype(BF16)
        acc_ref[...] = jnp.zeros_like(acc_ref)

    xb = xb_ref[...]
    gate = _dot(xb, wg_ref[...])
    up = _dot(xb, wu_ref[...])
    h = (gate * _sigmoid(gate) * up).astype(BF16)
    acc_ref[...] += _dot(h, wd_ref[...])

    @pl.when(f == pl.num_programs(1) - 1)
    def _():
        o_ref[...] = _layer_norm(ALPHA * x_ref[...] + acc_ref[...], g_ref[...], b_ref[...])


def _ffn_res_ln(x, wg, wu, wd, g, b, tm=512, tf=1408):
    t, d = x.shape
    ff = wg.shape[1]
    row = pl.BlockSpec((tm, d), lambda i, f: (i, 0))
    vec = pl.BlockSpec((1, d), lambda i, f: (0, 0))
    return pl.pallas_call(
        _ffn_kernel,
        grid=(t // tm, ff // tf),
        in_specs=[row,
                  pl.BlockSpec((d, tf), lambda i, f: (0, f)),
                  pl.BlockSpec((d, tf), lambda i, f: (0, f)),
                  pl.BlockSpec((tf, d), lambda i, f: (f, 0)),
                  vec, vec],
        out_specs=row,
        out_shape=jax.ShapeDtypeStruct((t, d), F32),
        scratch_shapes=[pltpu.VMEM((tm, d), BF16), pltpu.VMEM((tm, d), F32)],
        compiler_params=_params("parallel", "arbitrary"),
        name="ffn_res_ln",
    )(x, wg, wu, wd, g, b)


HALO = 8


def _rwkv_proj_kernel(seq_len, x_ref, xp_ref, xn_ref, mu_ref, wrkv_ref, w1_ref, w2_ref, w0_ref,
                      a1_ref, a2_ref, a0_ref, g1_ref, g2_ref, kscale_ref, hs_ref,
                      r_ref, k_ref, kk_ref, v_ref, g_ref, lw0_ref, lw1_ref, ia0_ref, ia1_ref):
    tm = x_ref.shape[0]
    i = pl.program_id(0)
    x = x_ref[...]
    row = lax.broadcasted_iota(jnp.int32, x.shape, 0)
    first_in_seq = (i * tm) % seq_len == 0
    last_in_seq = ((i + 1) * tm) % seq_len == 0
    prev_row = jnp.where(first_in_seq, 0.0, xp_ref[HALO - 1:HALO, :])
    next_row = jnp.where(last_in_seq, 0.0, xn_ref[0:1, :])
    x_prev = jnp.where(row == 0, prev_row, pltpu.roll(x, 1, axis=0))
    x_next = jnp.where(row == tm - 1, next_row, pltpu.roll(x, tm - 1, axis=0))
    xx = 0.5 * (x_prev + x_next) - x

    mixes = [(x + xx * mu_ref[j:j + 1, :]).astype(BF16) for j in range(6)]
    rf = _dot(mixes[0], wrkv_ref[0])
    kf = _dot(mixes[2], wrkv_ref[1])
    vf = _dot(mixes[3], wrkv_ref[2])
    lo_pre = _dot(mixes[1], w1_ref[...])
    al = _dot(mixes[4], a1_ref[...])
    gl_pre = _dot(mixes[5], g1_ref[...])

    r_ref[...] = rf.astype(r_ref.dtype)
    k_ref[...] = kf.astype(k_ref.dtype)
    v_ref[...] = vf.astype(v_ref.dtype)
    lane = lax.broadcasted_iota(jnp.int32, (tm, LANES), 1)
    dir0 = lane < DECAY_LORA
    lo = jnp.tanh(lo_pre)
    lz = [jnp.where(dir0 if z == 0 else ~dir0, lo, 0.0).astype(BF16) for z in range(2)]
    az = [jnp.where(dir0 if z == 0 else ~dir0, al, 0.0).astype(BF16) for z in range(2)]
    gl = _sigmoid(gl_pre).astype(BF16)
    groups = [slice(gi * LANES, (gi + 1) * LANES) for gi in range(D_MODEL // LANES)]
    kkr = [kf[:, cols] * kscale_ref[:, cols] for cols in groups]
    sq = [_split(q * q) for q in kkr]

    wl = [_dot(lz[z], w2_ref[...]) for z in range(2)]
    ai = [_dot(az[z], a2_ref[...]) for z in range(2)]
    gf = _dot(gl, g2_ref[...])
    hsum = hs_ref[...]
    norm2 = [_dot(hi, hsum) + _dot(lo_, hsum) for hi, lo_ in sq]

    for z, out in enumerate((lw0_ref, lw1_ref)):
        out[...] = (-math.exp(-0.5)) * _sigmoid(w0_ref[z:z + 1, :] + wl[z])
    for z, out in enumerate((ia0_ref, ia1_ref)):
        out[...] = _sigmoid(a0_ref[z:z + 1, :] + ai[z]).astype(out.dtype)
    g_ref[...] = gf.astype(g_ref.dtype)
    for cols, q, n2 in zip(groups, kkr, norm2):
        kk_ref[:, cols] = (q / jnp.maximum(jnp.sqrt(n2), 1e-12)).astype(kk_ref.dtype)


def _rwkv_proj(x, seq_len, mu, wrkv, w1, w2, w0, a1, a2, a0, g1, g2, kscale, tm=256):
    t, d = x.shape
    nb = t // HALO
    step = tm // HALO
    row = pl.BlockSpec((tm, d), lambda i: (i, 0))
    full = lambda shape: pl.BlockSpec(shape, lambda i: (0,) * len(shape))
    hsum = _head_sum_matrix(RWKV_HEAD_DIM)
    outs = [jax.ShapeDtypeStruct((t, d), BF16)] * 5 + [jax.ShapeDtypeStruct((t, d), F32)] * 2 + \
           [jax.ShapeDtypeStruct((t, d), BF16)] * 2
    return pl.pallas_call(
        functools.partial(_rwkv_proj_kernel, seq_len),
        grid=(t // tm,),
        in_specs=[row,
                  pl.BlockSpec((HALO, d), lambda i: (jnp.maximum(i * step - 1, 0), 0)),
                  pl.BlockSpec((HALO, d), lambda i: (jnp.minimum((i + 1) * step, nb - 1), 0)),
                  full(mu.shape), full(wrkv.shape), full(w1.shape), full(w2.shape), full(w0.shape),
                  full(a1.shape), full(a2.shape), full(a0.shape), full(g1.shape), full(g2.shape),
                  full(kscale.shape), full(hsum.shape)],
        out_specs=[row] * 9,
        out_shape=outs,
        compiler_params=_params("parallel"),
        name="rwkv_proj",
    )(x, x, x, mu, wrkv, w1, w2, w0, a1, a2, a0, g1, g2, kscale, hsum)


def _pair_blockdiag(y, lane_a):
    zero = jnp.zeros_like(y)
    return jnp.concatenate([jnp.where(lane_a, y, zero), jnp.where(lane_a, zero, y)], axis=0)


def _pair_matmul(x, y, lane_a):
    return _dot(x.astype(BF16), _pair_blockdiag(y.astype(BF16), lane_a))


def _scan_prepare(chains):
    c = SCAN_CHUNK
    t_idx = lax.broadcasted_iota(jnp.int32, (c, LANES), 0)
    s_idx = lax.broadcasted_iota(jnp.int32, (c, LANES), 1) % c
    lane_a = lax.broadcasted_iota(jnp.int32, (c, LANES), 1) < RWKV_HEAD_DIM
    eye = (s_idx == t_idx).astype(F32)
    tri_r = lax.broadcasted_iota(jnp.int32, (c, c), 0)
    tri_c = lax.broadcasted_iota(jnp.int32, (c, c), 1)
    bd = (lax.broadcasted_iota(jnp.int32, (LANES, LANES), 0) // RWKV_HEAD_DIM ==
          lax.broadcasted_iota(jnp.int32, (LANES, LANES), 1) // RWKV_HEAD_DIM)
    before = {rev: ((s_idx > t_idx) if rev else (s_idx < t_idx)) for rev in (False, True)}
    upto = {rev: before[rev] | (s_idx == t_idx) for rev in (False, True)}
    cum = {rev: ((tri_c >= tri_r) if rev else (tri_c <= tri_r)).astype(BF16) for rev in (False, True)}
    n = len(chains)

    lw_split = [jnp.concatenate(_split(ch[6]), axis=1) for ch in chains]
    lwc2 = [_dot(cum[chains[i][0]], lw_split[i]) for i in range(n)]
    lwc = [x[:, 0:LANES] + x[:, LANES:2 * LANES] for x in lwc2]
    vals = [ch[5] for ch in chains]

    at, rt, bh, kh, w_tot, lhs, rhs = [], [], [], [], [], [], []
    for (rev, ka, r, k, kk, v, lw, ia), lc in zip(chains, lwc):
        last = lc[0:1, :] if rev else lc[c - 1:c, :]
        w_in = jnp.exp(lc)
        w_ex = jnp.exp(lc - lw)
        w_inv = jnp.exp(-lc)
        w_end = jnp.exp(last)
        w_out = w_end * w_inv
        kf = k.astype(F32)
        kkf = kk.astype(F32)
        iaf = ia.astype(F32)
        kd = kf * (1.0 + (iaf - 1.0) * ka)
        bvec = kkf * iaf
        a_t = (-kkf) * w_ex
        r_t = r.astype(F32) * w_in
        at.append(a_t)
        rt.append(r_t)
        bh.append((bvec * w_out).astype(BF16))
        kh.append((kd * w_out).astype(BF16))
        w_tot.append(w_end)
        lhs.append(jnp.concatenate([a_t, r_t], axis=0).astype(BF16))
        rhs.append(jnp.concatenate([_pair_blockdiag((bvec * w_inv).astype(BF16), lane_a),
                                    _pair_blockdiag((kd * w_inv).astype(BF16), lane_a)], axis=0))

    gram = [_dot_nt(lhs[i], rhs[i]) for i in range(n)]
    a_ab = [jnp.where(before[chains[i][0]], gram[i][0:c, 0:LANES], 0.0) for i in range(n)]
    a_ak = [jnp.where(before[chains[i][0]], gram[i][0:c, LANES:2 * LANES], 0.0) for i in range(n)]
    a_rb = [jnp.where(upto[chains[i][0]], gram[i][c:2 * c, 0:LANES], 0.0) for i in range(n)]
    a_rk = [jnp.where(upto[chains[i][0]], gram[i][c:2 * c, LANES:2 * LANES], 0.0) for i in range(n)]

    tinv = [eye + a for a in a_ab]
    pw = [_pair_matmul(a, a, lane_a) for a in a_ab]
    n_sq = int(math.log2(c)) - 1
    for j in range(n_sq):
        final = j == n_sq - 1
        lhs_j = [(tinv[i] if final else jnp.concatenate([tinv[i], pw[i]], axis=0)).astype(BF16) for i in range(n)]
        res = [_dot(lhs_j[i], _pair_blockdiag(pw[i].astype(BF16), lane_a)) for i in range(n)]
        tinv = [tinv[i] + res[i][0:c] for i in range(n)]
        if not final:
            pw = [res[i][c:2 * c] for i in range(n)]

    av = [_pair_matmul(a_ak[i], vals[i], lane_a) for i in range(n)]
    y_loc = [_pair_matmul(a_rk[i], vals[i], lane_a) for i in range(n)]
    pq = [_dot(tinv[i].astype(BF16),
               jnp.concatenate([_pair_blockdiag(at[i].astype(BF16), lane_a),
                                _pair_blockdiag(av[i].astype(BF16), lane_a)], axis=1)) for i in range(n)]
    pqb = [x.astype(BF16) for x in pq]
    mn = [_dot_tn(pqb[i], bh[i]) for i in range(n)]
    kv = [_dot_tn(vals[i], kh[i]) for i in range(n)]
    arb = [_dot(a_rb[i].astype(BF16),
                jnp.concatenate([_pair_blockdiag(pqb[i][:, 0:LANES], lane_a),
                                 _pair_blockdiag(pqb[i][:, LANES:2 * LANES], lane_a)], axis=1)) for i in range(n)]
    out = []
    for i in range(n):
        ry = (rt[i] + arb[i][:, 0:LANES]).astype(BF16)
        yc = arb[i][:, LANES:2 * LANES] + y_loc[i]
        m_state = jnp.where(bd, mn[i][0:LANES], 0.0).astype(BF16)
        n_state = jnp.where(bd, mn[i][LANES:2 * LANES] + kv[i], 0.0)
        out.append((ry, yc, m_state, n_state, w_tot[i]))
    return out


def _scan_kernel(n_pairs, rf_ref, kf_ref, kkf_ref, vf_ref, lwf_ref, iaf_ref,
                 rb_ref, kb_ref, kkb_ref, vb_ref, lwb_ref, iab_ref, ka_ref, yf_ref, yb_ref, s_ref):
    @pl.when(pl.program_id(2) == 0)
    def _():
        s_ref[...] = jnp.zeros_like(s_ref)

    c = SCAN_CHUNK
    n_chunks = SCAN_BLOCK // c
    dirs = ((False, (rf_ref, kf_ref, kkf_ref, vf_ref, lwf_ref, iaf_ref), yf_ref),
            (True, (rb_ref, kb_ref, kkb_ref, vb_ref, lwb_ref, iab_ref), yb_ref))
    chains, where = [], []
    for step in range(n_chunks):
        for p in range(n_pairs):
            cols = slice(p * LANES, (p + 1) * LANES)
            for di, (rev, refs, _) in enumerate(dirs):
                ci = n_chunks - 1 - step if rev else step
                rows = slice(ci * c, (ci + 1) * c)
                chains.append((rev, ka_ref[:, cols]) + tuple(ref[rows, cols] for ref in refs))
                where.append((2 * p + di, di, rows, cols))
    pre = _scan_prepare(chains)
    state = [s_ref[j] for j in range(2 * n_pairs)]
    for (j, di, rows, cols), (ry, yc, m_state, n_state, w_total) in zip(where, pre):
        s = state[j]
        sb = s.astype(BF16)
        dirs[di][2][rows, cols] = _dot_nt(ry, sb) + yc
        state[j] = s * w_total + _dot(sb, m_state) + n_state
    for j in range(2 * n_pairs):
        s_ref[j] = state[j]


def _rwkv_scan(r, k, kk, v, lw0, lw1, ia0, ia1, ka, batch, seq_len, n_pairs=4):
    t, d = r.shape
    ns = seq_len // SCAN_BLOCK
    width = n_pairs * LANES
    fwd = pl.BlockSpec((SCAN_BLOCK, width), lambda b, h, s: (b * ns + s, h))
    bwd = pl.BlockSpec((SCAN_BLOCK, width), lambda b, h, s: (b * ns + ns - 1 - s, h))
    vec = pl.BlockSpec((1, width), lambda b, h, s: (0, h))
    return pl.pallas_call(
        functools.partial(_scan_kernel, n_pairs),
        grid=(batch, d // width, ns),
        in_specs=[fwd] * 6 + [bwd] * 6 + [vec],
        out_specs=[fwd, bwd],
        out_shape=[jax.ShapeDtypeStruct((t, d), F32)] * 2,
        scratch_shapes=[pltpu.VMEM((2 * n_pairs, LANES, LANES), F32)],
        compiler_params=_params("parallel", "parallel", "arbitrary"),
        name="rwkv_scan",
    )(r, k, kk, v, lw0, ia0, r, k, kk, v, lw1, ia1, ka)


def _rwkv_out_kernel(yf_ref, yb_ref, r_ref, k_ref, v_ref, g_ref, ia0_ref, ia1_ref, x_ref,
                     lg_ref, lb_ref, rk_ref, ka_ref, hs_ref, wo_ref, ng_ref, nb_ref, o_ref, z_ref):
    hsum = hs_ref[...]
    inv = 1.0 / RWKV_HEAD_DIM
    groups = [slice(gi * LANES, (gi + 1) * LANES) for gi in range(D_MODEL // LANES)]

    def head_sums(vals):
        parts = [_split(v) for v in vals]
        return [_dot(hi, hsum) + _dot(lo, hsum) for hi, lo in parts]

    y = [yf_ref[:, cols] + yb_ref[:, cols] for cols in groups]
    rkk = []
    for cols in groups:
        ia = ia0_ref[:, cols].astype(F32) + ia1_ref[:, cols].astype(F32)
        ksum = k_ref[:, cols].astype(F32) * (2.0 + (ia - 2.0) * ka_ref[:, cols])
        rkk.append(r_ref[:, cols].astype(F32) * ksum * rk_ref[:, cols])
    sums = head_sums(y + rkk)
    n_g = len(groups)
    yc = [y[gi] - sums[gi] * inv for gi in range(n_g)]
    var = head_sums([c * c for c in yc])
    for gi, cols in enumerate(groups):
        yn = yc[gi] * lax.rsqrt(var[gi] * inv + GN_EPS) * lg_ref[:, cols] + lb_ref[:, cols]
        bonus = sums[n_g + gi] * v_ref[:, cols].astype(F32)
        z_ref[:, cols] = ((yn + bonus) * g_ref[:, cols].astype(F32)).astype(BF16)
    h = _dot(z_ref[...], wo_ref[...])
    o_ref[...] = _layer_norm(ALPHA * x_ref[...] + h, ng_ref[...], nb_ref[...])


def _rwkv_out(yf, yb, r, k, v, g, ia0, ia1, x, lnx_g, lnx_b, r_k, k_a, wo, ng, nb, tm=256):
    t, d = x.shape
    row = pl.BlockSpec((tm, d), lambda i: (i, 0))
    vec = pl.BlockSpec((1, d), lambda i: (0, 0))
    return pl.pallas_call(
        _rwkv_out_kernel,
        grid=(t // tm,),
        in_specs=[row] * 9 + [vec] * 4 + [pl.BlockSpec((LANES, LANES), lambda i: (0, 0)),
                                          pl.BlockSpec((d, d), lambda i: (0, 0)), vec, vec],
        out_specs=row,
        out_shape=jax.ShapeDtypeStruct((t, d), F32),
        scratch_shapes=[pltpu.VMEM((tm, d), BF16)],
        compiler_params=_params("parallel"),
        name="rwkv_out",
    )(yf, yb, r, k, v, g, ia0, ia1, x, lnx_g, lnx_b, r_k, k_a, _head_sum_matrix(RWKV_HEAD_DIM), wo, ng, nb)


REC_E1, REC_E2, REC_R1, REC_R2, REC_G1, REC_G2 = range(6)


def _router_kernel(x_ref, w_ref, b_ref, tri_ref, rec_ref, cnt_ref, run_ref):
    @pl.when(pl.program_id(0) == 0)
    def _():
        run_ref[...] = jnp.zeros_like(run_ref)

    xh, xl = _split(x_ref[...])
    wh = w_ref[0]
    wl = w_ref[1]
    logits = _dot(xh, wh) + (_dot(xh, wl) + _dot(xl, wh)) + b_ref[...]
    lane = lax.broadcasted_iota(jnp.int32, logits.shape, 1)
    logits = jnp.where(lane < N_EXPERTS, logits, -jnp.inf)
    m1 = jnp.max(logits, axis=-1, keepdims=True)
    i1 = jnp.min(jnp.where(logits == m1, lane, LANES), axis=-1, keepdims=True)
    rest = jnp.where(lane == i1, -jnp.inf, logits)
    m2 = jnp.max(rest, axis=-1, keepdims=True)
    i2 = jnp.min(jnp.where(rest == m2, lane, LANES), axis=-1, keepdims=True)
    e2 = jnp.exp(m2 - m1)
    g1 = 1.0 / (1.0 + e2)
    g2 = e2 / (1.0 + e2)

    hot1 = lane == i1
    hot2 = lane == i2
    oh1 = hot1.astype(BF16)
    oh2 = hot2.astype(BF16)
    tri = tri_ref[...]
    run = run_ref[...]
    n1 = jnp.sum(oh1.astype(F32), axis=0, keepdims=True)
    n2 = jnp.sum(oh2.astype(F32), axis=0, keepdims=True)
    before1 = run + _dot(tri, oh1)
    before2 = run + n1 + _dot(tri, oh2)
    r1 = jnp.sum(jnp.where(hot1, before1, 0.0), axis=-1, keepdims=True)
    r2 = jnp.sum(jnp.where(hot2, before2, 0.0), axis=-1, keepdims=True)
    run = run + n1 + n2
    run_ref[...] = run
    cnt_ref[...] = run

    rec = jnp.zeros(logits.shape, F32)
    for idx, val in ((REC_E1, i1.astype(F32)), (REC_E2, i2.astype(F32)), (REC_R1, r1), (REC_R2, r2),
                     (REC_G1, g1), (REC_G2, g2)):
        rec = jnp.where(lane == idx, val, rec)
    rec_ref[...] = rec


def _router(x, w_hilo, bias, tm=512):
    t, d = x.shape
    tri = jnp.asarray(np.tril(np.ones((tm, tm), np.float32), -1), BF16)
    return pl.pallas_call(
        _router_kernel,
        grid=(t // tm,),
        in_specs=[pl.BlockSpec((tm, d), lambda i: (i, 0)),
                  pl.BlockSpec((2, d, LANES), lambda i: (0, 0, 0)),
                  pl.BlockSpec((1, LANES), lambda i: (0, 0)),
                  pl.BlockSpec((tm, tm), lambda i: (0, 0))],
        out_specs=[pl.BlockSpec((tm, LANES), lambda i: (i, 0)),
                   pl.BlockSpec((1, LANES), lambda i: (0, 0))],
        out_shape=[jax.ShapeDtypeStruct((t, LANES), F32), jax.ShapeDtypeStruct((1, LANES), F32)],
        scratch_shapes=[pltpu.VMEM((1, LANES), F32)],
        compiler_params=_params("arbitrary"),
        name="moe_router",
    )(x, w_hilo, bias, tri)


EXPERT_TILE = 1024
ROUTE_TILE = 512
TOKEN_ROWS = D_MODEL // LANES
DMA_UNROLL = 8


def _token_copy(src, src_tok, dst, dst_tok, sem):
    def tile(tok):
        return pl.ds(pl.multiple_of(tok * TOKEN_ROWS, TOKEN_ROWS), TOKEN_ROWS)
    return pltpu.make_async_copy(src.at[tile(src_tok)], dst.at[tile(dst_tok)], sem)


def _to_token_tiles(ref, n_tok, value):
    for j in range(TOKEN_ROWS):
        ref[pl.ds(j, n_tok, stride=TOKEN_ROWS), :] = value[:, j * LANES:(j + 1) * LANES]


def _from_token_tiles(ref, n_tok, j):
    return ref[pl.ds(j, n_tok, stride=TOKEN_ROWS), :]


def _dispatch_kernel(pos_ref, x_ref, xs_in, xs_hbm, xt_ref, sem):
    del xs_in
    _to_token_tiles(xt_ref, ROUTE_TILE, x_ref[...])

    def copies(t):
        return (_token_copy(xt_ref, t, xs_hbm, pos_ref[0, 0, t], sem),
                _token_copy(xt_ref, t, xs_hbm, pos_ref[0, 0, ROUTE_TILE + t], sem))

    def start(t, carry):
        for cp in copies(t):
            cp.start()
        return carry

    def wait(t, carry):
        for cp in copies(t):
            cp.wait()
        return carry

    lax.fori_loop(0, ROUTE_TILE, start, 0, unroll=DMA_UNROLL)
    lax.fori_loop(0, ROUTE_TILE, wait, 0, unroll=DMA_UNROLL)


def _moe_dispatch(x, pos, n_rows):
    t, d = x.shape
    return pl.pallas_call(
        _dispatch_kernel,
        grid=(t // ROUTE_TILE,),
        in_specs=[pl.BlockSpec((1, 1, 2 * ROUTE_TILE), lambda i: (i, 0, 0), memory_space=pltpu.SMEM),
                  pl.BlockSpec((ROUTE_TILE, d), lambda i: (i, 0)),
                  pl.BlockSpec(memory_space=pl.ANY)],
        out_specs=pl.BlockSpec(memory_space=pl.ANY),
        out_shape=jax.ShapeDtypeStruct((n_rows * TOKEN_ROWS, LANES), F32),
        scratch_shapes=[pltpu.VMEM((ROUTE_TILE * TOKEN_ROWS, LANES), F32), pltpu.SemaphoreType.DMA(())],
        input_output_aliases={2: 0},
        compiler_params=_params("arbitrary"),
        name="moe_dispatch",
    )(pos, x, jnp.zeros((n_rows * TOKEN_ROWS, LANES), F32))


def _experts_kernel(te_ref, na_ref, x_ref, wg_ref, wu_ref, wd_ref, o_ref, xb_ref, acc_ref):
    i = pl.program_id(0)
    f = pl.program_id(1)
    active = i < na_ref[0]

    @pl.when(active & (f == 0))
    def _():
        for j in range(TOKEN_ROWS):
            xb_ref[:, j * LANES:(j + 1) * LANES] = _from_token_tiles(x_ref, EXPERT_TILE, j).astype(BF16)
        acc_ref[...] = jnp.zeros_like(acc_ref)

    @pl.when(active)
    def _():
        xb = xb_ref[...]
        gate = _dot(xb, wg_ref[...])
        up = _dot(xb, wu_ref[...])
        h = (gate * _sigmoid(gate) * up).astype(BF16)
        acc_ref[...] += _dot(h, wd_ref[...])

    @pl.when(f == pl.num_programs(1) - 1)
    def _():
        _to_token_tiles(o_ref, EXPERT_TILE, jnp.where(active, acc_ref[...], 0.0))


def _moe_experts(xs, tile_expert, n_active, wg, wu, wd, tf=512):
    d = D_MODEL
    n_tiles = xs.shape[0] // (EXPERT_TILE * TOKEN_ROWS)
    fe = wg.shape[2]
    row = pl.BlockSpec((EXPERT_TILE * TOKEN_ROWS, LANES), lambda i, f, te, na: (i, 0))
    return pl.pallas_call(
        _experts_kernel,
        grid_spec=pltpu.PrefetchScalarGridSpec(
            num_scalar_prefetch=2,
            grid=(n_tiles, fe // tf),
            in_specs=[row,
                      pl.BlockSpec((None, d, tf), lambda i, f, te, na: (te[i], 0, f)),
                      pl.BlockSpec((None, d, tf), lambda i, f, te, na: (te[i], 0, f)),
                      pl.BlockSpec((None, tf, d), lambda i, f, te, na: (te[i], f, 0))],
            out_specs=row,
            scratch_shapes=[pltpu.VMEM((EXPERT_TILE, d), BF16), pltpu.VMEM((EXPERT_TILE, d), F32)]),
        out_shape=jax.ShapeDtypeStruct(xs.shape, F32),
        compiler_params=_params("parallel", "arbitrary"),
        name="moe_experts",
    )(tile_expert, n_active, xs, wg, wu, wd)


def _combine_kernel(pos_ref, nxt_ref, ys_hbm, x_ref, rec_ref, g_ref, b_ref, o_ref, y_ref, sem):
    i = pl.program_id(0)
    n = pl.num_programs(0)
    slot = i % 2

    def copies(p_ref, s, t):
        return (_token_copy(ys_hbm, p_ref[0, 0, t], y_ref.at[s, 0], t, sem.at[s]),
                _token_copy(ys_hbm, p_ref[0, 0, ROUTE_TILE + t], y_ref.at[s, 1], t, sem.at[s]))

    def start_all(p_ref, s):
        def body(t, carry):
            for cp in copies(p_ref, s, t):
                cp.start()
            return carry
        lax.fori_loop(0, ROUTE_TILE, body, 0, unroll=DMA_UNROLL)

    @pl.when(i == 0)
    def _():
        start_all(pos_ref, slot)

    @pl.when(i + 1 < n)
    def _():
        start_all(nxt_ref, 1 - slot)

    def wait_body(t, carry):
        for cp in copies(pos_ref, slot, t):
            cp.wait()
        return carry
    lax.fori_loop(0, ROUTE_TILE, wait_body, 0, unroll=DMA_UNROLL)

    rec = rec_ref[...]
    lane = lax.broadcasted_iota(jnp.int32, rec.shape, 1)
    g1 = jnp.sum(jnp.where(lane == REC_G1, rec, 0.0), axis=-1, keepdims=True)
    g2 = jnp.sum(jnp.where(lane == REC_G2, rec, 0.0), axis=-1, keepdims=True)
    y1_ref = y_ref.at[slot, 0]
    y2_ref = y_ref.at[slot, 1]
    f = jnp.concatenate([g1 * _from_token_tiles(y1_ref, ROUTE_TILE, j) + g2 * _from_token_tiles(y2_ref, ROUTE_TILE, j)
                         for j in range(TOKEN_ROWS)], axis=1)
    o_ref[...] = _layer_norm(ALPHA * x_ref[...] + f, g_ref[...], b_ref[...])


def _moe_combine_ln(ys, pos, x, rec, g, b):
    t, d = x.shape
    n = t // ROUTE_TILE
    row = pl.BlockSpec((ROUTE_TILE, d), lambda i: (i, 0))
    vec = pl.BlockSpec((1, d), lambda i: (0, 0))
    return pl.pallas_call(
        _combine_kernel,
        grid=(n,),
        in_specs=[pl.BlockSpec((1, 1, 2 * ROUTE_TILE), lambda i: (i, 0, 0), memory_space=pltpu.SMEM),
                  pl.BlockSpec((1, 1, 2 * ROUTE_TILE), lambda i: (jnp.minimum(i + 1, n - 1), 0, 0),
                               memory_space=pltpu.SMEM),
                  pl.BlockSpec(memory_space=pl.ANY),
                  row,
                  pl.BlockSpec((ROUTE_TILE, LANES), lambda i: (i, 0)),
                  vec, vec],
        out_specs=row,
        out_shape=jax.ShapeDtypeStruct((t, d), F32),
        scratch_shapes=[pltpu.VMEM((2, 2, ROUTE_TILE * TOKEN_ROWS, LANES), F32), pltpu.SemaphoreType.DMA((2,))],
        compiler_params=_params("arbitrary"),
        name="moe_combine_ln",
    )(pos, pos, ys, x, rec, g, b)


def _moe_res_ln(x, rec, counts, wg, wu, wd, g, b):
    t, d = x.shape
    n_tiles = 2 * t // EXPERT_TILE + N_EXPERTS
    cnt = counts[0, :N_EXPERTS].astype(jnp.int32)
    padded = (cnt + EXPERT_TILE - 1) // EXPERT_TILE * EXPERT_TILE
    ends = jnp.cumsum(padded)
    starts = ends - padded
    e1 = rec[:, REC_E1].astype(jnp.int32)
    e2 = rec[:, REC_E2].astype(jnp.int32)
    pos1 = starts[e1] + rec[:, REC_R1].astype(jnp.int32)
    pos2 = starts[e2] + rec[:, REC_R2].astype(jnp.int32)
    pos = jnp.concatenate([pos1.reshape(-1, 1, ROUTE_TILE), pos2.reshape(-1, 1, ROUTE_TILE)], axis=2)
    tile_start = jnp.arange(n_tiles, dtype=jnp.int32) * EXPERT_TILE
    tile_expert = jnp.minimum(jnp.sum(tile_start[:, None] >= ends[None, :], axis=1), N_EXPERTS - 1).astype(jnp.int32)
    n_active = (ends[-1:] // EXPERT_TILE).astype(jnp.int32)
    last_expert = tile_expert[jnp.maximum(n_active[0] - 1, 0)]
    tile_expert = jnp.where(jnp.arange(n_tiles) < n_active[0], tile_expert, last_expert)

    xs = _moe_dispatch(x, pos, n_tiles * EXPERT_TILE)
    ys = _moe_experts(xs, tile_expert, n_active, wg, wu, wd)
    return _moe_combine_ln(ys, pos, x, rec, g, b)


def _prepare_weights(na_w_qkv, na_rpb, na_w_o, ffn_w_gate, ffn_w_up, ffn_w_down,
                     rwkv_mu, rwkv_w_rkv, rwkv_w0, rwkv_w1, rwkv_w2, rwkv_a0, rwkv_a1, rwkv_a2,
                     rwkv_g1, rwkv_g2, rwkv_k_k, rwkv_k_a, rwkv_r_k, rwkv_lnx_g, rwkv_lnx_b, rwkv_w_o,
                     moe_w_router, moe_b_router, moe_w_gate, moe_w_up, moe_w_down,
                     ln_mix_g, ln_mix_b, ln_ffn_g, ln_ffn_b):
    d = D_MODEL
    vec = lambda a: a.reshape(1, d).astype(F32)
    gl_pad = 2 * LANES - GATE_LORA
    router_w = jnp.pad(moe_w_router[0], ((0, 0), (0, LANES - N_EXPERTS)))
    router_hi = router_w.astype(BF16)
    router_lo = (router_w - router_hi.astype(F32)).astype(BF16)
    return dict(
        qkv_w=na_w_qkv[0].astype(BF16),
        qkv_scale=jnp.concatenate([jnp.full((1, d), NA_HEAD_DIM ** -0.5 * LOG2E, F32), jnp.ones((1, 2 * d), F32)], axis=1),
        na_bias=_na_bias_table(na_rpb[0]),
        na_wo=na_w_o[0].astype(BF16),
        ffn_wg=ffn_w_gate[0].astype(BF16), ffn_wu=ffn_w_up[0].astype(BF16), ffn_wd=ffn_w_down[0].astype(BF16),
        mu=rwkv_mu[0].astype(F32),
        wrkv=rwkv_w_rkv[0].astype(BF16),
        w1=jnp.concatenate([rwkv_w1[0, 0], rwkv_w1[0, 1]], axis=1).astype(BF16),
        w2=jnp.concatenate([rwkv_w2[0, 0], rwkv_w2[0, 1]], axis=0).astype(BF16),
        w0=rwkv_w0[0].astype(F32),
        a1=jnp.concatenate([rwkv_a1[0, 0], rwkv_a1[0, 1]], axis=1).astype(BF16),
        a2=jnp.concatenate([rwkv_a2[0, 0], rwkv_a2[0, 1]], axis=0).astype(BF16),
        a0=rwkv_a0[0].astype(F32),
        g1=jnp.pad(rwkv_g1[0], ((0, 0), (0, gl_pad))).astype(BF16),
        g2=jnp.pad(rwkv_g2[0], ((0, gl_pad), (0, 0))).astype(BF16),
        k_k=vec(rwkv_k_k[0]), k_a=vec(rwkv_k_a[0]), r_k=vec(rwkv_r_k[0]),
        lnx_g=vec(rwkv_lnx_g[0]), lnx_b=vec(rwkv_lnx_b[0]),
        rwkv_wo=rwkv_w_o[0].astype(BF16),
        router_w=jnp.stack([router_hi, router_lo]),
        router_b=jnp.pad(moe_b_router[0], (0, LANES - N_EXPERTS)).reshape(1, LANES).astype(F32),
        moe_wg=moe_w_gate[0].astype(BF16), moe_wu=moe_w_up[0].astype(BF16), moe_wd=moe_w_down[0].astype(BF16),
        ln_mix_g=ln_mix_g.astype(F32), ln_mix_b=ln_mix_b.astype(F32),
        ln_ffn_g=ln_ffn_g.astype(F32), ln_ffn_b=ln_ffn_b.astype(F32),
    )


def _trunk(x3, w):
    batch, seq_len, d = x3.shape
    rows = seq_len // GRID_W
    x = x3.reshape(batch * seq_len, d)
    ln = lambda a, i: a[i].reshape(1, d)

    qkv = _qkv_proj(x, w["qkv_w"], w["qkv_scale"])
    att = _na_attention(qkv, w["na_bias"], batch, rows)
    x = _proj_res_ln(att, w["na_wo"], x, ln(w["ln_mix_g"], 0), ln(w["ln_mix_b"], 0))
    x = _ffn_res_ln(x, w["ffn_wg"], w["ffn_wu"], w["ffn_wd"], ln(w["ln_ffn_g"], 0), ln(w["ln_ffn_b"], 0))

    r, k, kk, v, g, lw0, lw1, ia0, ia1 = _rwkv_proj(x, seq_len, w["mu"], w["wrkv"], w["w1"], w["w2"], w["w0"],
                                                    w["a1"], w["a2"], w["a0"], w["g1"], w["g2"], w["k_k"])
    yf, yb = _rwkv_scan(r, k, kk, v, lw0, lw1, ia0, ia1, w["k_a"], batch, seq_len)
    x = _rwkv_out(yf, yb, r, k, v, g, ia0, ia1, x, w["lnx_g"], w["lnx_b"], w["r_k"], w["k_a"], w["rwkv_wo"],
                  ln(w["ln_mix_g"], 1), ln(w["ln_mix_b"], 1))
    rec, counts = _router(x, w["router_w"], w["router_b"])
    x = _moe_res_ln(x, rec, counts, w["moe_wg"], w["moe_wu"], w["moe_wd"], ln(w["ln_ffn_g"], 1), ln(w["ln_ffn_b"], 1))
    return x.reshape(batch, seq_len, d)


def kernel(x_prompt, x_sample, na_w_qkv, na_rpb, na_w_o, ffn_w_gate, ffn_w_up, ffn_w_down, rwkv_mu, rwkv_w_rkv, rwkv_w0, rwkv_w1, rwkv_w2, rwkv_a0, rwkv_a1, rwkv_a2, rwkv_g1, rwkv_g2, rwkv_k_k, rwkv_k_a, rwkv_r_k, rwkv_lnx_g, rwkv_lnx_b, rwkv_w_o, moe_w_router, moe_b_router, moe_w_gate, moe_w_up, moe_w_down, ln_mix_g, ln_mix_b, ln_ffn_g, ln_ffn_b):
    w = _prepare_weights(na_w_qkv, na_rpb, na_w_o, ffn_w_gate, ffn_w_up, ffn_w_down,
                         rwkv_mu, rwkv_w_rkv, rwkv_w0, rwkv_w1, rwkv_w2, rwkv_a0, rwkv_a1, rwkv_a2,
                         rwkv_g1, rwkv_g2, rwkv_k_k, rwkv_k_a, rwkv_r_k, rwkv_lnx_g, rwkv_lnx_b, rwkv_w_o,
                         moe_w_router, moe_b_router, moe_w_gate, moe_w_up, moe_w_down,
                         ln_mix_g, ln_mix_b, ln_ffn_g, ln_ffn_b)
    return (_trunk(x_prompt, w), _trunk(x_sample, w))
```

```python
import functools
import math

import numpy as np
import jax
import jax.numpy as jnp
from jax import lax
from jax.experimental import pallas as pl
from jax.experimental.pallas import tpu as pltpu

D_MODEL = 1024
DEPTH = 2
GRID_W = 64
NA_HEAD_DIM = 32
NA_HEADS = D_MODEL // NA_HEAD_DIM
NA_ROWS = 8
NA_COLS = 16
RWKV_HEAD_DIM = 64
DECAY_LORA = 64
ICLR_LORA = 64
GATE_LORA = 160
GN_EPS = 64e-5
D_FF = 2816
N_EXPERTS = 8
D_EXPERT = 3584
LN_EPS = 1e-5
ALPHA = (2 * DEPTH) ** 0.25
LOG2E = math.log2(math.e)

LANES = 128
SCAN_CHUNK = 64
SCAN_BLOCK = 256
VMEM_LIMIT = 56 * 1024 * 1024

F32 = jnp.float32
BF16 = jnp.bfloat16


def _params(*sem):
    return pltpu.CompilerParams(dimension_semantics=sem, vmem_limit_bytes=VMEM_LIMIT)


def _dot(a, b):
    return jnp.dot(a, b, preferred_element_type=F32)


def _dot_nt(a, b):
    return lax.dot_general(a, b, (((1,), (1,)), ((), ())), preferred_element_type=F32)


def _dot_tn(a, b):
    return lax.dot_general(a, b, (((0,), (0,)), ((), ())), preferred_element_type=F32)


def _split(a):
    hi = a.astype(BF16)
    lo = (a - hi.astype(F32)).astype(BF16)
    return hi, lo


def _layer_norm(z, g, b):
    mu = jnp.mean(z, axis=-1, keepdims=True)
    zc = z - mu
    var = jnp.mean(zc * zc, axis=-1, keepdims=True)
    return zc * lax.rsqrt(var + LN_EPS) * g + b


def _sigmoid(z):
    return 1.0 / (1.0 + jnp.exp(-z))


def _head_sum_matrix(head_dim):
    lane = np.arange(LANES)
    return jnp.asarray((lane[:, None] // head_dim) == (lane[None, :] // head_dim), BF16)


def _qkv_kernel(x_ref, w_ref, s_ref, o_ref):
    acc = _dot(x_ref[...].astype(BF16), w_ref[...])
    o_ref[...] = (acc * s_ref[...]).astype(o_ref.dtype)


def _qkv_proj(x, w, colscale, tm=512):
    t, d = x.shape
    n = w.shape[1]
    return pl.pallas_call(
        _qkv_kernel,
        grid=(t // tm,),
        in_specs=[pl.BlockSpec((tm, d), lambda i: (i, 0)),
                  pl.BlockSpec((d, n), lambda i: (0, 0)),
                  pl.BlockSpec((1, n), lambda i: (0, 0))],
        out_specs=pl.BlockSpec((tm, n), lambda i: (i, 0)),
        out_shape=jax.ShapeDtypeStruct((t, n), BF16),
        compiler_params=_params("parallel"),
        name="qkv_proj",
    )(x, w, colscale)


NA_HEADS_PER_GROUP = LANES // NA_HEAD_DIM
NA_GROUPS = D_MODEL // LANES
NA_WIN = NA_ROWS * GRID_W


NA_GROUP_ROWS = NA_HEADS_PER_GROUP * GRID_W
NA_ROW_PAIRS = NA_ROWS // 2
NA_BIAS_PAIRS = 2 * NA_ROWS - 2


NA_STEP_ROWS = 2
NA_FETCH_ROWS = NA_ROWS + NA_STEP_ROWS - 1


def _na_win_start(r, rows):
    return jnp.clip(r - NA_ROWS // 2, 0, rows - NA_ROWS)


def _na_fetch_start(r0, rows):
    return jnp.minimum(_na_win_start(r0, rows), rows - NA_FETCH_ROWS)


def _na_kernel(rows, q_ref, k_ref, v_ref, b_ref, o_ref):
    r0 = pl.program_id(1) * NA_STEP_ROWS
    fetch = _na_fetch_start(r0, rows)
    lane_head = lax.broadcasted_iota(jnp.int32, (GRID_W, LANES), 1) // NA_HEAD_DIM
    groups = [slice(g * LANES, (g + 1) * LANES) for g in range(NA_GROUPS)]
    units = []
    for ri in range(NA_STEP_ROWS):
        win = _na_win_start(r0 + ri, rows)
        keys = pl.ds(pl.multiple_of((win - fetch) * GRID_W, GRID_W), NA_WIN)
        for g in range(NA_GROUPS):
            units.append((slice(ri * GRID_W, (ri + 1) * GRID_W), keys, r0 + ri - win, g))

    scores = []
    for qrows, keys, _, g in units:
        qg = q_ref[qrows, groups[g]]
        zero = jnp.zeros_like(qg)
        qm = jnp.concatenate([jnp.where(lane_head == h, qg, zero) for h in range(NA_HEADS_PER_GROUP)], axis=0)
        scores.append(_dot_nt(qm, k_ref[keys, groups[g]]))

    probs, denoms = [], []
    for (_, _, off, g), s in zip(units, scores):
        hq = pl.ds(g * NA_GROUP_ROWS, NA_GROUP_ROWS)
        s = jnp.concatenate([s[:, m * LANES:(m + 1) * LANES] + b_ref[2 * m - off + NA_ROWS - 1, hq, :]
                             for m in range(NA_ROW_PAIRS)], axis=1)
        p = jnp.exp2(s - jnp.max(s, axis=-1, keepdims=True))
        denoms.append(jnp.sum(p, axis=-1, keepdims=True))
        probs.append(p.astype(BF16))

    for (qrows, keys, _, g), p, l in zip(units, probs, denoms):
        o4 = _dot(p, v_ref[keys, groups[g]]) / l
        og = jnp.zeros((GRID_W, LANES), F32)
        for h in range(NA_HEADS_PER_GROUP):
            og = og + jnp.where(lane_head == h, o4[h * GRID_W:(h + 1) * GRID_W], 0.0)
        o_ref[qrows, groups[g]] = og.astype(o_ref.dtype)


def _na_attention(qkv, bias, batch, rows):
    t = qkv.shape[0]
    steps = rows // NA_STEP_ROWS
    q_rows = NA_STEP_ROWS * GRID_W

    def q_map(b, s):
        return (b * steps + s, 0)

    def k_map(b, s):
        return ((b * rows + _na_fetch_start(s * NA_STEP_ROWS, rows)) * GRID_W, D_MODEL)

    def v_map(b, s):
        return ((b * rows + _na_fetch_start(s * NA_STEP_ROWS, rows)) * GRID_W, 2 * D_MODEL)

    window = (pl.Element(NA_FETCH_ROWS * GRID_W), pl.Element(D_MODEL))
    return pl.pallas_call(
        functools.partial(_na_kernel, rows),
        grid=(batch, steps),
        in_specs=[pl.BlockSpec((q_rows, D_MODEL), q_map),
                  pl.BlockSpec(window, k_map),
                  pl.BlockSpec(window, v_map),
                  pl.BlockSpec(bias.shape, lambda b, s: (0, 0, 0), pipeline_mode=pl.Buffered(1))],
        out_specs=pl.BlockSpec((q_rows, D_MODEL), q_map),
        out_shape=jax.ShapeDtypeStruct((t, D_MODEL), BF16),
        compiler_params=_params("parallel", "arbitrary"),
        name="na_attention",
    )(qkv, qkv, qkv, bias)


def _na_bias_kernel(rpb_ref, oh_ref, mask_ref, o_ref):
    x = rpb_ref[...]
    hi = x.astype(BF16)
    r1 = x - hi.astype(F32)
    mid = r1.astype(BF16)
    lo = (r1 - mid.astype(F32)).astype(BF16)
    oh = oh_ref[...]
    o_ref[...] = (_dot(hi, oh) + _dot(mid, oh) + _dot(lo, oh)) * LOG2E + mask_ref[...]


def _na_bias_table(rpb, tn=1024):
    n_dr, n_dc = 2 * NA_ROWS - 1, 2 * NA_COLS - 1
    c = np.arange(GRID_W)
    q_start = np.clip(c - NA_COLS // 2, 0, GRID_W - NA_COLS)
    kc = np.arange(GRID_W)
    valid = (kc[None, :] >= q_start[:, None]) & (kc[None, :] < q_start[:, None] + NA_COLS)
    dc = kc[None, :] - c[:, None] + NA_COLS - 1
    onehot = (np.arange(LANES)[:, None, None] == dc[None]) & valid[None]
    onehot = jnp.asarray(onehot.reshape(LANES, GRID_W * GRID_W), BF16)
    mask = jnp.asarray(np.where(valid, 0.0, -np.inf).reshape(1, GRID_W * GRID_W), F32)
    rpb2 = jnp.pad(rpb.reshape(NA_HEADS * n_dr, n_dc).astype(F32), ((0, 0), (0, LANES - n_dc)))
    nrow, ncol = rpb2.shape[0], GRID_W * GRID_W
    flat = pl.pallas_call(
        _na_bias_kernel,
        grid=(ncol // tn,),
        in_specs=[pl.BlockSpec((nrow, LANES), lambda j: (0, 0)),
                  pl.BlockSpec((LANES, tn), lambda j: (0, j)),
                  pl.BlockSpec((1, tn), lambda j: (0, j))],
        out_specs=pl.BlockSpec((nrow, tn), lambda j: (0, j)),
        out_shape=jax.ShapeDtypeStruct((nrow, ncol), F32),
        compiler_params=_params("parallel"),
        name="na_bias_table",
    )(rpb2, onehot, mask)
    toe = flat.reshape(NA_HEADS, n_dr, GRID_W, GRID_W)
    pairs = jnp.stack([toe[:, 0:NA_BIAS_PAIRS], toe[:, 1:NA_BIAS_PAIRS + 1]], axis=3)
    return jnp.transpose(pairs, (1, 0, 2, 3, 4)).reshape(NA_BIAS_PAIRS, NA_HEADS * GRID_W, LANES)


def _proj_ln_kernel(a_ref, w_ref, x_ref, g_ref, b_ref, o_ref):
    h = _dot(a_ref[...], w_ref[...])
    o_ref[...] = _layer_norm(ALPHA * x_ref[...] + h, g_ref[...], b_ref[...])


def _proj_res_ln(a, w, x, g, b, tm=512):
    t, d = x.shape
    row = pl.BlockSpec((tm, d), lambda i: (i, 0))
    vec = pl.BlockSpec((1, d), lambda i: (0, 0))
    return pl.pallas_call(
        _proj_ln_kernel,
        grid=(t // tm,),
        in_specs=[row, pl.BlockSpec((d, d), lambda i: (0, 0)), row, vec, vec],
        out_specs=row,
        out_shape=jax.ShapeDtypeStruct((t, d), F32),
        compiler_params=_params("parallel"),
        name="proj_res_ln",
    )(a, w, x, g, b)


def _ffn_kernel(x_ref, wg_ref, wu_ref, wd_ref, g_ref, b_ref, o_ref, xb_ref, acc_ref):
    f = pl.program_id(1)

    @pl.when(f == 0)
    def _():
        xb_ref[...] = x_ref[...].astype(BF16)
        acc_ref[...] = jnp.zeros_like(acc_ref)

    xb = xb_ref[...]
    gate = _dot(xb, wg_ref[...])
    up = _dot(xb, wu_ref[...])
    h = (gate * _sigmoid(gate) * up).astype(BF16)
    acc_ref[...] += _dot(h, wd_ref[...])

    @pl.when(f == pl.num_programs(1) - 1)
    def _():
        o_ref[...] = _layer_norm(ALPHA * x_ref[...] + acc_ref[...], g_ref[...], b_ref[...])


def _ffn_res_ln(x, wg, wu, wd, g, b, tm=512, tf=1408):
    t, d = x.shape
    ff = wg.shape[1]
    row = pl.BlockSpec((tm, d), lambda i, f: (i, 0))
    vec = pl.BlockSpec((1, d), lambda i, f: (0, 0))
    return pl.pallas_call(
        _ffn_kernel,
        grid=(t // tm, ff // tf),
        in_specs=[row,
                  pl.BlockSpec((d, tf), lambda i, f: (0, f)),
                  pl.BlockSpec((d, tf), lambda i, f: (0, f)),
                  pl.BlockSpec((tf, d), lambda i, f: (f, 0)),
                  vec, vec],
        out_specs=row,
        out_shape=jax.ShapeDtypeStruct((t, d), F32),
        scratch_shapes=[pltpu.VMEM((tm, d), BF16), pltpu.VMEM((tm, d), F32)],
        compiler_params=_params("parallel", "arbitrary"),
        name="ffn_res_ln",
    )(x, wg, wu, wd, g, b)


HALO = 8


def _rwkv_proj_kernel(seq_len, x_ref, xp_ref, xn_ref, mu_ref, wrkv_ref, w1_ref, w2_ref, w0_ref,
                      a1_ref, a2_ref, a0_ref, g1_ref, g2_ref, kscale_ref, hs_ref,
                      r_ref, k_ref, kk_ref, v_ref, g_ref, lw0_ref, lw1_ref, ia0_ref, ia1_ref):
    tm = x_ref.shape[0]
    i = pl.program_id(0)
    x = x_ref[...]
    row = lax.broadcasted_iota(jnp.int32, x.shape, 0)
    first_in_seq = (i * tm) % seq_len == 0
    last_in_seq = ((i + 1) * tm) % seq_len == 0
    prev_row = jnp.where(first_in_seq, 0.0, xp_ref[HALO - 1:HALO, :])
    next_row = jnp.where(last_in_seq, 0.0, xn_ref[0:1, :])
    x_prev = jnp.where(row == 0, prev_row, pltpu.roll(x, 1, axis=0))
    x_next = jnp.where(row == tm - 1, next_row, pltpu.roll(x, tm - 1, axis=0))
    xx = 0.5 * (x_prev + x_next) - x

    mixes = [(x + xx * mu_ref[j:j + 1, :]).astype(BF16) for j in range(6)]
    rf = _dot(mixes[0], wrkv_ref[0])
    kf = _dot(mixes[2], wrkv_ref[1])
    vf = _dot(mixes[3], wrkv_ref[2])
    lo_pre = _dot(mixes[1], w1_ref[...])
    al = _dot(mixes[4], a1_ref[...])
    gl_pre = _dot(mixes[5], g1_ref[...])

    r_ref[...] = rf.astype(r_ref.dtype)
    k_ref[...] = kf.astype(k_ref.dtype)
    v_ref[...] = vf.astype(v_ref.dtype)
    lane = lax.broadcasted_iota(jnp.int32, (tm, LANES), 1)
    dir0 = lane < DECAY_LORA
    lo = jnp.tanh(lo_pre)
    lz = [jnp.where(dir0 if z == 0 else ~dir0, lo, 0.0).astype(BF16) for z in range(2)]
    az = [jnp.where(dir0 if z == 0 else ~dir0, al, 0.0).astype(BF16) for z in range(2)]
    gl = _sigmoid(gl_pre).astype(BF16)
    groups = [slice(gi * LANES, (gi + 1) * LANES) for gi in range(D_MODEL // LANES)]
    kkr = [kf[:, cols] * kscale_ref[:, cols] for cols in groups]
    sq = [_split(q * q) for q in kkr]

    wl = [_dot(lz[z], w2_ref[...]) for z in range(2)]
    ai = [_dot(az[z], a2_ref[...]) for z in range(2)]
    gf = _dot(gl, g2_ref[...])
    hsum = hs_ref[...]
    norm2 = [_dot(hi, hsum) + _dot(lo_, hsum) for hi, lo_ in sq]

    for z, out in enumerate((lw0_ref, lw1_ref)):
        out[...] = (-math.exp(-0.5)) * _sigmoid(w0_ref[z:z + 1, :] + wl[z])
    for z, out in enumerate((ia0_ref, ia1_ref)):
        out[...] = _sigmoid(a0_ref[z:z + 1, :] + ai[z]).astype(out.dtype)
    g_ref[...] = gf.astype(g_ref.dtype)
    for cols, q, n2 in zip(groups, kkr, norm2):
        kk_ref[:, cols] = (q / jnp.maximum(jnp.sqrt(n2), 1e-12)).astype(kk_ref.dtype)


def _rwkv_proj(x, seq_len, mu, wrkv, w1, w2, w0, a1, a2, a0, g1, g2, kscale, tm=256):
    t, d = x.shape
    nb = t // HALO
    step = tm // HALO
    row = pl.BlockSpec((tm, d), lambda i: (i, 0))
    full = lambda shape: pl.BlockSpec(shape, lambda i: (0,) * len(shape))
    hsum = _head_sum_matrix(RWKV_HEAD_DIM)
    outs = [jax.ShapeDtypeStruct((t, d), BF16)] * 5 + [jax.ShapeDtypeStruct((t, d), F32)] * 2 + \
           [jax.ShapeDtypeStruct((t, d), BF16)] * 2
    return pl.pallas_call(
        functools.partial(_rwkv_proj_kernel, seq_len),
        grid=(t // tm,),
        in_specs=[row,
                  pl.BlockSpec((HALO, d), lambda i: (jnp.maximum(i * step - 1, 0), 0)),
                  pl.BlockSpec((HALO, d), lambda i: (jnp.minimum((i + 1) * step, nb - 1), 0)),
                  full(mu.shape), full(wrkv.shape), full(w1.shape), full(w2.shape), full(w0.shape),
                  full(a1.shape), full(a2.shape), full(a0.shape), full(g1.shape), full(g2.shape),
                  full(kscale.shape), full(hsum.shape)],
        out_specs=[row] * 9,
        out_shape=outs,
        compiler_params=_params("parallel"),
        name="rwkv_proj",
    )(x, x, x, mu, wrkv, w1, w2, w0, a1, a2, a0, g1, g2, kscale, hsum)


def _pair_blockdiag(y, lane_a):
    zero = jnp.zeros_like(y)
    return jnp.concatenate([jnp.where(lane_a, y, zero), jnp.where(lane_a, zero, y)], axis=0)


def _pair_matmul(x, y, lane_a):
    return _dot(x.astype(BF16), _pair_blockdiag(y.astype(BF16), lane_a))


def _scan_prepare(chains):
    c = SCAN_CHUNK
    t_idx = lax.broadcasted_iota(jnp.int32, (c, LANES), 0)
    s_idx = lax.broadcasted_iota(jnp.int32, (c, LANES), 1) % c
    lane_a = lax.broadcasted_iota(jnp.int32, (c, LANES), 1) < RWKV_HEAD_DIM
    eye = (s_idx == t_idx).astype(F32)
    tri_r = lax.broadcasted_iota(jnp.int32, (c, c), 0)
    tri_c = lax.broadcasted_iota(jnp.int32, (c, c), 1)
    bd = (lax.broadcasted_iota(jnp.int32, (LANES, LANES), 0) // RWKV_HEAD_DIM ==
          lax.broadcasted_iota(jnp.int32, (LANES, LANES), 1) // RWKV_HEAD_DIM)
    before = {rev: ((s_idx > t_idx) if rev else (s_idx < t_idx)) for rev in (False, True)}
    upto = {rev: before[rev] | (s_idx == t_idx) for rev in (False, True)}
    cum = {rev: ((tri_c >= tri_r) if rev else (tri_c <= tri_r)).astype(BF16) for rev in (False, True)}
    n = len(chains)

    lw_split = [jnp.concatenate(_split(ch[6]), axis=1) for ch in chains]
    lwc2 = [_dot(cum[chains[i][0]], lw_split[i]) for i in range(n)]
    lwc = [x[:, 0:LANES] + x[:, LANES:2 * LANES] for x in lwc2]
    vals = [ch[5] for ch in chains]

    at, rt, bh, kh, w_tot, lhs, rhs = [], [], [], [], [], [], []
    for (rev, ka, r, k, kk, v, lw, ia), lc in zip(chains, lwc):
        last = lc[0:1, :] if rev else lc[c - 1:c, :]
        w_in = jnp.exp(lc)
        w_ex = jnp.exp(lc - lw)
        w_inv = jnp.exp(-lc)
        w_end = jnp.exp(last)
        w_out = w_end * w_inv
        kf = k.astype(F32)
        kkf = kk.astype(F32)
        iaf = ia.astype(F32)
        kd = kf * (1.0 + (iaf - 1.0) * ka)
        bvec = kkf * iaf
        a_t = (-kkf) * w_ex
        r_t = r.astype(F32) * w_in
        at.append(a_t)
        rt.append(r_t)
        bh.append((bvec * w_out).astype(BF16))
        kh.append((kd * w_out).astype(BF16))
        w_tot.append(w_end)
        lhs.append(jnp.concatenate([a_t, r_t], axis=0).astype(BF16))
        rhs.append(jnp.concatenate([_pair_blockdiag((bvec * w_inv).astype(BF16), lane_a),
                                    _pair_blockdiag((kd * w_inv).astype(BF16), lane_a)], axis=0))

    gram = [_dot_nt(lhs[i], rhs[i]) for i in range(n)]
    a_ab = [jnp.where(before[chains[i][0]], gram[i][0:c, 0:LANES], 0.0) for i in range(n)]
    a_ak = [jnp.where(before[chains[i][0]], gram[i][0:c, LANES:2 * LANES], 0.0) for i in range(n)]
    a_rb = [jnp.where(upto[chains[i][0]], gram[i][c:2 * c, 0:LANES], 0.0) for i in range(n)]
    a_rk = [jnp.where(upto[chains[i][0]], gram[i][c:2 * c, LANES:2 * LANES], 0.0) for i in range(n)]

    tinv = [eye + a for a in a_ab]
    pw = [_pair_matmul(a, a, lane_a) for a in a_ab]
    n_sq = int(math.log2(c)) - 1
    for j in range(n_sq):
        final = j == n_sq - 1
        lhs_j = [(tinv[i] if final else jnp.concatenate([tinv[i], pw[i]], axis=0)).astype(BF16) for i in range(n)]
        res = [_dot(lhs_j[i], _pair_blockdiag(pw[i].astype(BF16), lane_a)) for i in range(n)]
        tinv = [tinv[i] + res[i][0:c] for i in range(n)]
        if not final:
            pw = [res[i][c:2 * c] for i in range(n)]

    av = [_pair_matmul(a_ak[i], vals[i], lane_a) for i in range(n)]
    y_loc = [_pair_matmul(a_rk[i], vals[i], lane_a) for i in range(n)]
    pq = [_dot(tinv[i].astype(BF16),
               jnp.concatenate([_pair_blockdiag(at[i].astype(BF16), lane_a),
                                _pair_blockdiag(av[i].astype(BF16), lane_a)], axis=1)) for i in range(n)]
    pqb = [x.astype(BF16) for x in pq]
    mn = [_dot_tn(pqb[i], bh[i]) for i in range(n)]
    kv = [_dot_tn(vals[i], kh[i]) for i in range(n)]
    arb = [_dot(a_rb[i].astype(BF16),
                jnp.concatenate([_pair_blockdiag(pqb[i][:, 0:LANES], lane_a),
                                 _pair_blockdiag(pqb[i][:, LANES:2 * LANES], lane_a)], axis=1)) for i in range(n)]
    out = []
    for i in range(n):
        ry = (rt[i] + arb[i][:, 0:LANES]).astype(BF16)
        yc = arb[i][:, LANES:2 * LANES] + y_loc[i]
        m_state = jnp.where(bd, mn[i][0:LANES], 0.0).astype(BF16)
        n_state = jnp.where(bd, mn[i][LANES:2 * LANES] + kv[i], 0.0)
        out.append((ry, yc, m_state, n_state, w_tot[i]))
    return out


def _scan_kernel(n_pairs, rf_ref, kf_ref, kkf_ref, vf_ref, lwf_ref, iaf_ref,
                 rb_ref, kb_ref, kkb_ref, vb_ref, lwb_ref, iab_ref, ka_ref, yf_ref, yb_ref, s_ref):
    @pl.when(pl.program_id(2) == 0)
    def _():
        s_ref[...] = jnp.zeros_like(s_ref)

    c = SCAN_CHUNK
    n_chunks = SCAN_BLOCK // c
    dirs = ((False, (rf_ref, kf_ref, kkf_ref, vf_ref, lwf_ref, iaf_ref), yf_ref),
            (True, (rb_ref, kb_ref, kkb_ref, vb_ref, lwb_ref, iab_ref), yb_ref))
    chains, where = [], []
    for step in range(n_chunks):
        for p in range(n_pairs):
            cols = slice(p * LANES, (p + 1) * LANES)
            for di, (rev, refs, _) in enumerate(dirs):
                ci = n_chunks - 1 - step if rev else step
                rows = slice(ci * c, (ci + 1) * c)
                chains.append((rev, ka_ref[:, cols]) + tuple(ref[rows, cols] for ref in refs))
                where.append((2 * p + di, di, rows, cols))
    pre = _scan_prepare(chains)
    state = [s_ref[j] for j in range(2 * n_pairs)]
    for (j, di, rows, cols), (ry, yc, m_state, n_state, w_total) in zip(where, pre):
        s = state[j]
        sb = s.astype(BF16)
        dirs[di][2][rows, cols] = _dot_nt(ry, sb) + yc
        state[j] = s * w_total + _dot(sb, m_state) + n_state
    for j in range(2 * n_pairs):
        s_ref[j] = state[j]


def _rwkv_scan(r, k, kk, v, lw0, lw1, ia0, ia1, ka, batch, seq_len, n_pairs=8):
    t, d = r.shape
    ns = seq_len // SCAN_BLOCK
    width = n_pairs * LANES
    fwd = pl.BlockSpec((SCAN_BLOCK, width), lambda b, h, s: (b * ns + s, h))
    bwd = pl.BlockSpec((SCAN_BLOCK, width), lambda b, h, s: (b * ns + ns - 1 - s, h))
    vec = pl.BlockSpec((1, width), lambda b, h, s: (0, h))
    return pl.pallas_call(
        functools.partial(_scan_kernel, n_pairs),
        grid=(batch, d // width, ns),
        in_specs=[fwd] * 6 + [bwd] * 6 + [vec],
        out_specs=[fwd, bwd],
        out_shape=[jax.ShapeDtypeStruct((t, d), F32)] * 2,
        scratch_shapes=[pltpu.VMEM((2 * n_pairs, LANES, LANES), F32)],
        compiler_params=_params("parallel", "parallel", "arbitrary"),
        name="rwkv_scan",
    )(r, k, kk, v, lw0, ia0, r, k, kk, v, lw1, ia1, ka)


def _rwkv_out_kernel(yf_ref, yb_ref, r_ref, k_ref, v_ref, g_ref, ia0_ref, ia1_ref, x_ref,
                     lg_ref, lb_ref, rk_ref, ka_ref, hs_ref, wo_ref, ng_ref, nb_ref, o_ref, z_ref):
    hsum = hs_ref[...]
    inv = 1.0 / RWKV_HEAD_DIM
    groups = [slice(gi * LANES, (gi + 1) * LANES) for gi in range(D_MODEL // LANES)]

    def head_sums(vals):
        parts = [_split(v) for v in vals]
        return [_dot(hi, hsum) + _dot(lo, hsum) for hi, lo in parts]

    y = [yf_ref[:, cols] + yb_ref[:, cols] for cols in groups]
    rkk = []
    for cols in groups:
        ia = ia0_ref[:, cols].astype(F32) + ia1_ref[:, cols].astype(F32)
        ksum = k_ref[:, cols].astype(F32) * (2.0 + (ia - 2.0) * ka_ref[:, cols])
        rkk.append(r_ref[:, cols].astype(F32) * ksum * rk_ref[:, cols])
    sums = head_sums(y + rkk)
    n_g = len(groups)
    yc = [y[gi] - sums[gi] * inv for gi in range(n_g)]
    var = head_sums([c * c for c in yc])
    for gi, cols in enumerate(groups):
        yn = yc[gi] * lax.rsqrt(var[gi] * inv + GN_EPS) * lg_ref[:, cols] + lb_ref[:, cols]
        bonus = sums[n_g + gi] * v_ref[:, cols].astype(F32)
        z_ref[:, cols] = ((yn + bonus) * g_ref[:, cols].astype(F32)).astype(BF16)
    h = _dot(z_ref[...], wo_ref[...])
    o_ref[...] = _layer_norm(ALPHA * x_ref[...] + h, ng_ref[...], nb_ref[...])


def _rwkv_out(yf, yb, r, k, v, g, ia0, ia1, x, lnx_g, lnx_b, r_k, k_a, wo, ng, nb, tm=256):
    t, d = x.shape
    row = pl.BlockSpec((tm, d), lambda i: (i, 0))
    vec = pl.BlockSpec((1, d), lambda i: (0, 0))
    return pl.pallas_call(
        _rwkv_out_kernel,
        grid=(t // tm,),
        in_specs=[row] * 9 + [vec] * 4 + [pl.BlockSpec((LANES, LANES), lambda i: (0, 0)),
                                          pl.BlockSpec((d, d), lambda i: (0, 0)), vec, vec],
        out_specs=row,
        out_shape=jax.ShapeDtypeStruct((t, d), F32),
        scratch_shapes=[pltpu.VMEM((tm, d), BF16)],
        compiler_params=_params("parallel"),
        name="rwkv_out",
    )(yf, yb, r, k, v, g, ia0, ia1, x, lnx_g, lnx_b, r_k, k_a, _head_sum_matrix(RWKV_HEAD_DIM), wo, ng, nb)


REC_E1, REC_E2, REC_R1, REC_R2, REC_G1, REC_G2 = range(6)


def _router_kernel(x_ref, w_ref, b_ref, tri_ref, rec_ref, cnt_ref, run_ref):
    @pl.when(pl.program_id(0) == 0)
    def _():
        run_ref[...] = jnp.zeros_like(run_ref)

    xh, xl = _split(x_ref[...])
    wh = w_ref[0]
    wl = w_ref[1]
    logits = _dot(xh, wh) + (_dot(xh, wl) + _dot(xl, wh)) + b_ref[...]
    lane = lax.broadcasted_iota(jnp.int32, logits.shape, 1)
    logits = jnp.where(lane < N_EXPERTS, logits, -jnp.inf)
    m1 = jnp.max(logits, axis=-1, keepdims=True)
    i1 = jnp.min(jnp.where(logits == m1, lane, LANES), axis=-1, keepdims=True)
    rest = jnp.where(lane == i1, -jnp.inf, logits)
    m2 = jnp.max(rest, axis=-1, keepdims=True)
    i2 = jnp.min(jnp.where(rest == m2, lane, LANES), axis=-1, keepdims=True)
    e2 = jnp.exp(m2 - m1)
    g1 = 1.0 / (1.0 + e2)
    g2 = e2 / (1.0 + e2)

    hot1 = lane == i1
    hot2 = lane == i2
    oh1 = hot1.astype(BF16)
    oh2 = hot2.astype(BF16)
    tri = tri_ref[...]
    run = run_ref[...]
    n1 = jnp.sum(oh1.astype(F32), axis=0, keepdims=True)
    n2 = jnp.sum(oh2.astype(F32), axis=0, keepdims=True)
    before1 = run + _dot(tri, oh1)
    before2 = run + n1 + _dot(tri, oh2)
    r1 = jnp.sum(jnp.where(hot1, before1, 0.0), axis=-1, keepdims=True)
    r2 = jnp.sum(jnp.where(hot2, before2, 0.0), axis=-1, keepdims=True)
    run = run + n1 + n2
    run_ref[...] = run
    cnt_ref[...] = run

    rec = jnp.zeros(logits.shape, F32)
    for idx, val in ((REC_E1, i1.astype(F32)), (REC_E2, i2.astype(F32)), (REC_R1, r1), (REC_R2, r2),
                     (REC_G1, g1), (REC_G2, g2)):
        rec = jnp.where(lane == idx, val, rec)
    rec_ref[...] = rec


def _router(x, w_hilo, bias, tm=512):
    t, d = x.shape
    tri = jnp.asarray(np.tril(np.ones((tm, tm), np.float32), -1), BF16)
    return pl.pallas_call(
        _router_kernel,
        grid=(t // tm,),
        in_specs=[pl.BlockSpec((tm, d), lambda i: (i, 0)),
                  pl.BlockSpec((2, d, LANES), lambda i: (0, 0, 0)),
                  pl.BlockSpec((1, LANES), lambda i: (0, 0)),
                  pl.BlockSpec((tm, tm), lambda i: (0, 0))],
        out_specs=[pl.BlockSpec((tm, LANES), lambda i: (i, 0)),
                   pl.BlockSpec((1, LANES), lambda i: (0, 0))],
        out_shape=[jax.ShapeDtypeStruct((t, LANES), F32), jax.ShapeDtypeStruct((1, LANES), F32)],
        scratch_shapes=[pltpu.VMEM((1, LANES), F32)],
        compiler_params=_params("arbitrary"),
        name="moe_router",
    )(x, w_hilo, bias, tri)


EXPERT_TILE = 1024
ROUTE_TILE = 512
TOKEN_ROWS = D_MODEL // LANES
DMA_UNROLL = 8


def _token_copy(src, src_tok, dst, dst_tok, sem):
    def tile(tok):
        return pl.ds(pl.multiple_of(tok * TOKEN_ROWS, TOKEN_ROWS), TOKEN_ROWS)
    return pltpu.make_async_copy(src.at[tile(src_tok)], dst.at[tile(dst_tok)], sem)


def _to_token_tiles(ref, n_tok, value):
    for j in range(TOKEN_ROWS):
        ref[pl.ds(j, n_tok, stride=TOKEN_ROWS), :] = value[:, j * LANES:(j + 1) * LANES]


def _from_token_tiles(ref, n_tok, j):
    return ref[pl.ds(j, n_tok, stride=TOKEN_ROWS), :]


def _dispatch_kernel(pos_ref, x_ref, xs_in, xs_hbm, xt_ref, sem):
    del xs_in
    _to_token_tiles(xt_ref, ROUTE_TILE, x_ref[...])

    def copies(t):
        return (_token_copy(xt_ref, t, xs_hbm, pos_ref[0, 0, t], sem),
                _token_copy(xt_ref, t, xs_hbm, pos_ref[0, 0, ROUTE_TILE + t], sem))

    def start(t, carry):
        for cp in copies(t):
            cp.start()
        return carry

    def wait(t, carry):
        for cp in copies(t):
            cp.wait()
        return carry

    lax.fori_loop(0, ROUTE_TILE, start, 0, unroll=DMA_UNROLL)
    lax.fori_loop(0, ROUTE_TILE, wait, 0, unroll=DMA_UNROLL)


def _moe_dispatch(x, pos, n_rows):
    t, d = x.shape
    return pl.pallas_call(
        _dispatch_kernel,
        grid=(t // ROUTE_TILE,),
        in_specs=[pl.BlockSpec((1, 1, 2 * ROUTE_TILE), lambda i: (i, 0, 0), memory_space=pltpu.SMEM),
                  pl.BlockSpec((ROUTE_TILE, d), lambda i: (i, 0)),
                  pl.BlockSpec(memory_space=pl.ANY)],
        out_specs=pl.BlockSpec(memory_space=pl.ANY),
        out_shape=jax.ShapeDtypeStruct((n_rows * TOKEN_ROWS, LANES), F32),
        scratch_shapes=[pltpu.VMEM((ROUTE_TILE * TOKEN_ROWS, LANES), F32), pltpu.SemaphoreType.DMA(())],
        input_output_aliases={2: 0},
        compiler_params=_params("arbitrary"),
        name="moe_dispatch",
    )(pos, x, jnp.zeros((n_rows * TOKEN_ROWS, LANES), F32))


def _experts_kernel(te_ref, na_ref, x_ref, wg_ref, wu_ref, wd_ref, o_ref, xb_ref, acc_ref):
    i = pl.program_id(0)
    f = pl.program_id(1)
    active = i < na_ref[0]

    @pl.when(active & (f == 0))
    def _():
        for j in range(TOKEN_ROWS):
            xb_ref[:, j * LANES:(j + 1) * LANES] = _from_token_tiles(x_ref, EXPERT_TILE, j).astype(BF16)
        acc_ref[...] = jnp.zeros_like(acc_ref)

    @pl.when(active)
    def _():
        xb = xb_ref[...]
        gate = _dot(xb, wg_ref[...])
        up = _dot(xb, wu_ref[...])
        h = (gate * _sigmoid(gate) * up).astype(BF16)
        acc_ref[...] += _dot(h, wd_ref[...])

    @pl.when(f == pl.num_programs(1) - 1)
    def _():
        _to_token_tiles(o_ref, EXPERT_TILE, jnp.where(active, acc_ref[...], 0.0))


def _moe_experts(xs, tile_expert, n_active, wg, wu, wd, tf=896):
    d = D_MODEL
    n_tiles = xs.shape[0] // (EXPERT_TILE * TOKEN_ROWS)
    fe = wg.shape[2]
    row = pl.BlockSpec((EXPERT_TILE * TOKEN_ROWS, LANES), lambda i, f, te, na: (i, 0))
    return pl.pallas_call(
        _experts_kernel,
        grid_spec=pltpu.PrefetchScalarGridSpec(
            num_scalar_prefetch=2,
            grid=(n_tiles, fe // tf),
            in_specs=[row,
                      pl.BlockSpec((None, d, tf), lambda i, f, te, na: (te[i], 0, f)),
                      pl.BlockSpec((None, d, tf), lambda i, f, te, na: (te[i], 0, f)),
                      pl.BlockSpec((None, tf, d), lambda i, f, te, na: (te[i], f, 0))],
            out_specs=row,
            scratch_shapes=[pltpu.VMEM((EXPERT_TILE, d), BF16), pltpu.VMEM((EXPERT_TILE, d), F32)]),
        out_shape=jax.ShapeDtypeStruct(xs.shape, F32),
        compiler_params=_params("parallel", "arbitrary"),
        name="moe_experts",
    )(tile_expert, n_active, xs, wg, wu, wd)


def _combine_kernel(pos_ref, nxt_ref, ys_hbm, x_ref, rec_ref, g_ref, b_ref, o_ref, y_ref, sem):
    i = pl.program_id(0)
    n = pl.num_programs(0)
    slot = i % 2

    def copies(p_ref, s, t):
        return (_token_copy(ys_hbm, p_ref[0, 0, t], y_ref.at[s, 0], t, sem.at[s]),
                _token_copy(ys_hbm, p_ref[0, 0, ROUTE_TILE + t], y_ref.at[s, 1], t, sem.at[s]))

    def start_all(p_ref, s):
        def body(t, carry):
            for cp in copies(p_ref, s, t):
                cp.start()
            return carry
        lax.fori_loop(0, ROUTE_TILE, body, 0, unroll=DMA_UNROLL)

    @pl.when(i == 0)
    def _():
        start_all(pos_ref, slot)

    @pl.when(i + 1 < n)
    def _():
        start_all(nxt_ref, 1 - slot)

    def wait_body(t, carry):
        for cp in copies(pos_ref, slot, t):
            cp.wait()
        return carry
    lax.fori_loop(0, ROUTE_TILE, wait_body, 0, unroll=DMA_UNROLL)

    rec = rec_ref[...]
    lane = lax.broadcasted_iota(jnp.int32, rec.shape, 1)
    g1 = jnp.sum(jnp.where(lane == REC_G1, rec, 0.0), axis=-1, keepdims=True)
    g2 = jnp.sum(jnp.where(lane == REC_G2, rec, 0.0), axis=-1, keepdims=True)
    y1_ref = y_ref.at[slot, 0]
    y2_ref = y_ref.at[slot, 1]
    f = jnp.concatenate([g1 * _from_token_tiles(y1_ref, ROUTE_TILE, j) + g2 * _from_token_tiles(y2_ref, ROUTE_TILE, j)
                         for j in range(TOKEN_ROWS)], axis=1)
    o_ref[...] = _layer_norm(ALPHA * x_ref[...] + f, g_ref[...], b_ref[...])


def _moe_combine_ln(ys, pos, x, rec, g, b):
    t, d = x.shape
    n = t // ROUTE_TILE
    row = pl.BlockSpec((ROUTE_TILE, d), lambda i: (i, 0))
    vec = pl.BlockSpec((1, d), lambda i: (0, 0))
    return pl.pallas_call(
        _combine_kernel,
        grid=(n,),
        in_specs=[pl.BlockSpec((1, 1, 2 * ROUTE_TILE), lambda i: (i, 0, 0), memory_space=pltpu.SMEM),
                  pl.BlockSpec((1, 1, 2 * ROUTE_TILE), lambda i: (jnp.minimum(i + 1, n - 1), 0, 0),
                               memory_space=pltpu.SMEM),
                  pl.BlockSpec(memory_space=pl.ANY),
                  row,
                  pl.BlockSpec((ROUTE_TILE, LANES), lambda i: (i, 0)),
                  vec, vec],
        out_specs=row,
        out_shape=jax.ShapeDtypeStruct((t, d), F32),
        scratch_shapes=[pltpu.VMEM((2, 2, ROUTE_TILE * TOKEN_ROWS, LANES), F32), pltpu.SemaphoreType.DMA((2,))],
        compiler_params=_params("arbitrary"),
        name="moe_combine_ln",
    )(pos, pos, ys, x, rec, g, b)


def _moe_res_ln(x, rec, counts, wg, wu, wd, g, b):
    t, d = x.shape
    n_tiles = 2 * t // EXPERT_TILE + N_EXPERTS
    cnt = counts[0, :N_EXPERTS].astype(jnp.int32)
    padded = (cnt + EXPERT_TILE - 1) // EXPERT_TILE * EXPERT_TILE
    ends = jnp.cumsum(padded)
    starts = ends - padded
    e1 = rec[:, REC_E1].astype(jnp.int32)
    e2 = rec[:, REC_E2].astype(jnp.int32)
    pos1 = starts[e1] + rec[:, REC_R1].astype(jnp.int32)
    pos2 = starts[e2] + rec[:, REC_R2].astype(jnp.int32)
    pos = jnp.concatenate([pos1.reshape(-1, 1, ROUTE_TILE), pos2.reshape(-1, 1, ROUTE_TILE)], axis=2)
    tile_start = jnp.arange(n_tiles, dtype=jnp.int32) * EXPERT_TILE
    tile_expert = jnp.minimum(jnp.sum(tile_start[:, None] >= ends[None, :], axis=1), N_EXPERTS - 1).astype(jnp.int32)
    n_active = (ends[-1:] // EXPERT_TILE).astype(jnp.int32)
    last_expert = tile_expert[jnp.maximum(n_active[0] - 1, 0)]
    tile_expert = jnp.where(jnp.arange(n_tiles) < n_active[0], tile_expert, last_expert)

    xs = _moe_dispatch(x, pos, n_tiles * EXPERT_TILE)
    ys = _moe_experts(xs, tile_expert, n_active, wg, wu, wd)
    return _moe_combine_ln(ys, pos, x, rec, g, b)


def _prepare_weights(na_w_qkv, na_rpb, na_w_o, ffn_w_gate, ffn_w_up, ffn_w_down,
                     rwkv_mu, rwkv_w_rkv, rwkv_w0, rwkv_w1, rwkv_w2, rwkv_a0, rwkv_a1, rwkv_a2,
                     rwkv_g1, rwkv_g2, rwkv_k_k, rwkv_k_a, rwkv_r_k, rwkv_lnx_g, rwkv_lnx_b, rwkv_w_o,
                     moe_w_router, moe_b_router, moe_w_gate, moe_w_up, moe_w_down,
                     ln_mix_g, ln_mix_b, ln_ffn_g, ln_ffn_b):
    d = D_MODEL
    vec = lambda a: a.reshape(1, d).astype(F32)
    gl_pad = 2 * LANES - GATE_LORA
    router_w = jnp.pad(moe_w_router[0], ((0, 0), (0, LANES - N_EXPERTS)))
    router_hi = router_w.astype(BF16)
    router_lo = (router_w - router_hi.astype(F32)).astype(BF16)
    return dict(
        qkv_w=na_w_qkv[0].astype(BF16),
        qkv_scale=jnp.concatenate([jnp.full((1, d), NA_HEAD_DIM ** -0.5 * LOG2E, F32), jnp.ones((1, 2 * d), F32)], axis=1),
        na_bias=_na_bias_table(na_rpb[0]),
        na_wo=na_w_o[0].astype(BF16),
        ffn_wg=ffn_w_gate[0].astype(BF16), ffn_wu=ffn_w_up[0].astype(BF16), ffn_wd=ffn_w_down[0].astype(BF16),
        mu=rwkv_mu[0].astype(F32),
        wrkv=rwkv_w_rkv[0].astype(BF16),
        w1=jnp.concatenate([rwkv_w1[0, 0], rwkv_w1[0, 1]], axis=1).astype(BF16),
        w2=jnp.concatenate([rwkv_w2[0, 0], rwkv_w2[0, 1]], axis=0).astype(BF16),
        w0=rwkv_w0[0].astype(F32),
        a1=jnp.concatenate([rwkv_a1[0, 0], rwkv_a1[0, 1]], axis=1).astype(BF16),
        a2=jnp.concatenate([rwkv_a2[0, 0], rwkv_a2[0, 1]], axis=0).astype(BF16),
        a0=rwkv_a0[0].astype(F32),
        g1=jnp.pad(rwkv_g1[0], ((0, 0), (0, gl_pad))).astype(BF16),
        g2=jnp.pad(rwkv_g2[0], ((0, gl_pad), (0, 0))).astype(BF16),
        k_k=vec(rwkv_k_k[0]), k_a=vec(rwkv_k_a[0]), r_k=vec(rwkv_r_k[0]),
        lnx_g=vec(rwkv_lnx_g[0]), lnx_b=vec(rwkv_lnx_b[0]),
        rwkv_wo=rwkv_w_o[0].astype(BF16),
        router_w=jnp.stack([router_hi, router_lo]),
        router_b=jnp.pad(moe_b_router[0], (0, LANES - N_EXPERTS)).reshape(1, LANES).astype(F32),
        moe_wg=moe_w_gate[0].astype(BF16), moe_wu=moe_w_up[0].astype(BF16), moe_wd=moe_w_down[0].astype(BF16),
        ln_mix_g=ln_mix_g.astype(F32), ln_mix_b=ln_mix_b.astype(F32),
        ln_ffn_g=ln_ffn_g.astype(F32), ln_ffn_b=ln_ffn_b.astype(F32),
    )


def _trunk(x3, w):
    batch, seq_len, d = x3.shape
    rows = seq_len // GRID_W
    x = x3.reshape(batch * seq_len, d)
    ln = lambda a, i: a[i].reshape(1, d)

    qkv = _qkv_proj(x, w["qkv_w"], w["qkv_scale"])
    att = _na_attention(qkv, w["na_bias"], batch, rows)
    x = _proj_res_ln(att, w["na_wo"], x, ln(w["ln_mix_g"], 0), ln(w["ln_mix_b"], 0))
    x = _ffn_res_ln(x, w["ffn_wg"], w["ffn_wu"], w["ffn_wd"], ln(w["ln_ffn_g"], 0), ln(w["ln_ffn_b"], 0))

    r, k, kk, v, g, lw0, lw1, ia0, ia1 = _rwkv_proj(x, seq_len, w["mu"], w["wrkv"], w["w1"], w["w2"], w["w0"],
                                                    w["a1"], w["a2"], w["a0"], w["g1"], w["g2"], w["k_k"])
    yf, yb = _rwkv_scan(r, k, kk, v, lw0, lw1, ia0, ia1, w["k_a"], batch, seq_len)
    x = _rwkv_out(yf, yb, r, k, v, g, ia0, ia1, x, w["lnx_g"], w["lnx_b"], w["r_k"], w["k_a"], w["rwkv_wo"],
                  ln(w["ln_mix_g"], 1), ln(w["ln_mix_b"], 1))
    rec, counts = _router(x, w["router_w"], w["router_b"])
    x = _moe_res_ln(x, rec, counts, w["moe_wg"], w["moe_wu"], w["moe_wd"], ln(w["ln_ffn_g"], 1), ln(w["ln_ffn_b"], 1))
    return x.reshape(batch, seq_len, d)


def kernel(x_prompt, x_sample, na_w_qkv, na_rpb, na_w_o, ffn_w_gate, ffn_w_up, ffn_w_down, rwkv_mu, rwkv_w_rkv, rwkv_w0, rwkv_w1, rwkv_w2, rwkv_a0, rwkv_a1, rwkv_a2, rwkv_g1, rwkv_g2, rwkv_k_k, rwkv_k_a, rwkv_r_k, rwkv_lnx_g, rwkv_lnx_b, rwkv_w_o, moe_w_router, moe_b_router, moe_w_gate, moe_w_up, moe_w_down, ln_mix_g, ln_mix_b, ln_ffn_g, ln_ffn_b):
    w = _prepare_weights(na_w_qkv, na_rpb, na_w_o, ffn_w_gate, ffn_w_up, ffn_w_down,
                         rwkv_mu, rwkv_w_rkv, rwkv_w0, rwkv_w1, rwkv_w2, rwkv_a0, rwkv_a1, rwkv_a2,
                         rwkv_g1, rwkv_g2, rwkv_k_k, rwkv_k_a, rwkv_r_k, rwkv_lnx_g, rwkv_lnx_b, rwkv_w_o,
                         moe_w_router, moe_b_router, moe_w_gate, moe_w_up, moe_w_down,
                         ln_mix_g, ln_mix_b, ln_ffn_g, ln_ffn_b)
    return (_trunk(x_prompt, w), _trunk(x_sample, w))
```

```python
import functools
import math

import numpy as np
import jax
import jax.numpy as jnp
from jax import lax
from jax.experimental import pallas as pl
from jax.experimental.pallas import tpu as pltpu

D_MODEL = 1024
DEPTH = 2
GRID_W = 64
NA_HEAD_DIM = 32
NA_HEADS = D_MODEL // NA_HEAD_DIM
NA_ROWS = 8
NA_COLS = 16
RWKV_HEAD_DIM = 64
DECAY_LORA = 64
ICLR_LORA = 64
GATE_LORA = 160
GN_EPS = 64e-5
D_FF = 2816
N_EXPERTS = 8
D_EXPERT = 3584
LN_EPS = 1e-5
ALPHA = (2 * DEPTH) ** 0.25
LOG2E = math.log2(math.e)

LANES = 128
SCAN_CHUNK = 64
SCAN_BLOCK = 256
VMEM_LIMIT = 56 * 1024 * 1024

F32 = jnp.float32
BF16 = jnp.bfloat16


def _params(*sem):
    return pltpu.CompilerParams(dimension_semantics=sem, vmem_limit_bytes=VMEM_LIMIT)


def _dot(a, b):
    return jnp.dot(a, b, preferred_element_type=F32)


def _dot_nt(a, b):
    return lax.dot_general(a, b, (((1,), (1,)), ((), ())), preferred_element_type=F32)


def _dot_tn(a, b):
    return lax.dot_general(a, b, (((0,), (0,)), ((), ())), preferred_element_type=F32)


def _split(a):
    hi = a.astype(BF16)
    lo = (a - hi.astype(F32)).astype(BF16)
    return hi, lo


def _layer_norm(z, g, b):
    mu = jnp.mean(z, axis=-1, keepdims=True)
    zc = z - mu
    var = jnp.mean(zc * zc, axis=-1, keepdims=True)
    return zc * lax.rsqrt(var + LN_EPS) * g + b


def _sigmoid(z):
    return 1.0 / (1.0 + jnp.exp(-z))


def _head_sum_matrix(head_dim):
    lane = np.arange(LANES)
    return jnp.asarray((lane[:, None] // head_dim) == (lane[None, :] // head_dim), BF16)


def _qkv_kernel(x_ref, w_ref, s_ref, o_ref):
    acc = _dot(x_ref[...].astype(BF16), w_ref[...])
    o_ref[...] = (acc * s_ref[...]).astype(o_ref.dtype)


def _qkv_proj(x, w, colscale, tm=512):
    t, d = x.shape
    n = w.shape[1]
    return pl.pallas_call(
        _qkv_kernel,
        grid=(t // tm,),
        in_specs=[pl.BlockSpec((tm, d), lambda i: (i, 0)),
                  pl.BlockSpec((d, n), lambda i: (0, 0)),
                  pl.BlockSpec((1, n), lambda i: (0, 0))],
        out_specs=pl.BlockSpec((tm, n), lambda i: (i, 0)),
        out_shape=jax.ShapeDtypeStruct((t, n), BF16),
        compiler_params=_params("parallel"),
        name="qkv_proj",
    )(x, w, colscale)


NA_HEADS_PER_GROUP = LANES // NA_HEAD_DIM
NA_GROUPS = D_MODEL // LANES
NA_WIN = NA_ROWS * GRID_W


NA_GROUP_ROWS = NA_HEADS_PER_GROUP * GRID_W
NA_ROW_PAIRS = NA_ROWS // 2
NA_BIAS_PAIRS = 2 * NA_ROWS - 2


NA_STEP_ROWS = 2
NA_FETCH_ROWS = NA_ROWS + NA_STEP_ROWS - 1


def _na_win_start(r, rows):
    return jnp.clip(r - NA_ROWS // 2, 0, rows - NA_ROWS)


def _na_fetch_start(r0, rows):
    return jnp.minimum(_na_win_start(r0, rows), rows - NA_FETCH_ROWS)


def _na_kernel(rows, q_ref, k_ref, v_ref, b_ref, o_ref):
    r0 = pl.program_id(1) * NA_STEP_ROWS
    fetch = _na_fetch_start(r0, rows)
    lane_head = lax.broadcasted_iota(jnp.int32, (GRID_W, LANES), 1) // NA_HEAD_DIM
    groups = [slice(g * LANES, (g + 1) * LANES) for g in range(NA_GROUPS)]
    units = []
    for ri in range(NA_STEP_ROWS):
        win = _na_win_start(r0 + ri, rows)
        keys = pl.ds(pl.multiple_of((win - fetch) * GRID_W, GRID_W), NA_WIN)
        for g in range(NA_GROUPS):
            units.append((slice(ri * GRID_W, (ri + 1) * GRID_W), keys, r0 + ri - win, g))

    scores = []
    for qrows, keys, _, g in units:
        qg = q_ref[qrows, groups[g]]
        zero = jnp.zeros_like(qg)
        qm = jnp.concatenate([jnp.where(lane_head == h, qg, zero) for h in range(NA_HEADS_PER_GROUP)], axis=0)
        scores.append(_dot_nt(qm, k_ref[keys, groups[g]]))

    probs, denoms = [], []
    for (_, _, off, g), s in zip(units, scores):
        hq = pl.ds(g * NA_GROUP_ROWS, NA_GROUP_ROWS)
        s = jnp.concatenate([s[:, m * LANES:(m + 1) * LANES] + b_ref[2 * m - off + NA_ROWS - 1, hq, :]
                             for m in range(NA_ROW_PAIRS)], axis=1)
        p = jnp.exp2(s - jnp.max(s, axis=-1, keepdims=True))
        denoms.append(jnp.sum(p, axis=-1, keepdims=True))
        probs.append(p.astype(BF16))

    for (qrows, keys, _, g), p, l in zip(units, probs, denoms):
        o4 = _dot(p, v_ref[keys, groups[g]]) / l
        og = jnp.zeros((GRID_W, LANES), F32)
        for h in range(NA_HEADS_PER_GROUP):
            og = og + jnp.where(lane_head == h, o4[h * GRID_W:(h + 1) * GRID_W], 0.0)
        o_ref[qrows, groups[g]] = og.astype(o_ref.dtype)


def _na_attention(qkv, bias, batch, rows):
    t = qkv.shape[0]
    steps = rows // NA_STEP_ROWS
    q_rows = NA_STEP_ROWS * GRID_W

    def q_map(b, s):
        return (b * steps + s, 0)

    def k_map(b, s):
        return ((b * rows + _na_fetch_start(s * NA_STEP_ROWS, rows)) * GRID_W, D_MODEL)

    def v_map(b, s):
        return ((b * rows + _na_fetch_start(s * NA_STEP_ROWS, rows)) * GRID_W, 2 * D_MODEL)

    window = (pl.Element(NA_FETCH_ROWS * GRID_W), pl.Element(D_MODEL))
    return pl.pallas_call(
        functools.partial(_na_kernel, rows),
        grid=(batch, steps),
        in_specs=[pl.BlockSpec((q_rows, D_MODEL), q_map),
                  pl.BlockSpec(window, k_map),
                  pl.BlockSpec(window, v_map),
                  pl.BlockSpec(bias.shape, lambda b, s: (0, 0, 0), pipeline_mode=pl.Buffered(1))],
        out_specs=pl.BlockSpec((q_rows, D_MODEL), q_map),
        out_shape=jax.ShapeDtypeStruct((t, D_MODEL), BF16),
        compiler_params=_params("parallel", "arbitrary"),
        name="na_attention",
    )(qkv, qkv, qkv, bias)


def _na_bias_kernel(rpb_ref, oh_ref, mask_ref, o_ref):
    x = rpb_ref[...]
    hi = x.astype(BF16)
    r1 = x - hi.astype(F32)
    mid = r1.astype(BF16)
    lo = (r1 - mid.astype(F32)).astype(BF16)
    oh = oh_ref[...]
    o_ref[...] = (_dot(hi, oh) + _dot(mid, oh) + _dot(lo, oh)) * LOG2E + mask_ref[...]


def _na_bias_table(rpb, tn=1024):
    n_dr, n_dc = 2 * NA_ROWS - 1, 2 * NA_COLS - 1
    c = np.arange(GRID_W)
    q_start = np.clip(c - NA_COLS // 2, 0, GRID_W - NA_COLS)
    kc = np.arange(GRID_W)
    valid = (kc[None, :] >= q_start[:, None]) & (kc[None, :] < q_start[:, None] + NA_COLS)
    dc = kc[None, :] - c[:, None] + NA_COLS - 1
    onehot = (np.arange(LANES)[:, None, None] == dc[None]) & valid[None]
    onehot = jnp.asarray(onehot.reshape(LANES, GRID_W * GRID_W), BF16)
    mask = jnp.asarray(np.where(valid, 0.0, -np.inf).reshape(1, GRID_W * GRID_W), F32)
    rpb2 = jnp.pad(rpb.reshape(NA_HEADS * n_dr, n_dc).astype(F32), ((0, 0), (0, LANES - n_dc)))
    nrow, ncol = rpb2.shape[0], GRID_W * GRID_W
    flat = pl.pallas_call(
        _na_bias_kernel,
        grid=(ncol // tn,),
        in_specs=[pl.BlockSpec((nrow, LANES), lambda j: (0, 0)),
                  pl.BlockSpec((LANES, tn), lambda j: (0, j)),
                  pl.BlockSpec((1, tn), lambda j: (0, j))],
        out_specs=pl.BlockSpec((nrow, tn), lambda j: (0, j)),
        out_shape=jax.ShapeDtypeStruct((nrow, ncol), F32),
        compiler_params=_params("parallel"),
        name="na_bias_table",
    )(rpb2, onehot, mask)
    toe = flat.reshape(NA_HEADS, n_dr, GRID_W, GRID_W)
    pairs = jnp.stack([toe[:, 0:NA_BIAS_PAIRS], toe[:, 1:NA_BIAS_PAIRS + 1]], axis=3)
    return jnp.transpose(pairs, (1, 0, 2, 3, 4)).reshape(NA_BIAS_PAIRS, NA_HEADS * GRID_W, LANES)


def _proj_ln_kernel(a_ref, w_ref, x_ref, g_ref, b_ref, o_ref):
    h = _dot(a_ref[...], w_ref[...])
    o_ref[...] = _layer_norm(ALPHA * x_ref[...] + h, g_ref[...], b_ref[...])


def _proj_res_ln(a, w, x, g, b, tm=512):
    t, d = x.shape
    row = pl.BlockSpec((tm, d), lambda i: (i, 0))
    vec = pl.BlockSpec((1, d), lambda i: (0, 0))
    return pl.pallas_call(
        _proj_ln_kernel,
        grid=(t // tm,),
        in_specs=[row, pl.BlockSpec((d, d), lambda i: (0, 0)), row, vec, vec],
        out_specs=row,
        out_shape=jax.ShapeDtypeStruct((t, d), F32),
        compiler_params=_params("parallel"),
        name="proj_res_ln",
    )(a, w, x, g, b)


def _ffn_kernel(x_ref, wg_ref, wu_ref, wd_ref, g_ref, b_ref, o_ref, xb_ref, acc_ref):
    f = pl.program_id(1)

    @pl.when(f == 0)
    def _():
        xb_ref[...] = x_ref[...].astype(BF16)
        acc_ref[...] = jnp.zeros_like(acc_ref)

    xb = xb_ref[...]
    gate = _dot(xb, wg_ref[...])
    up = _dot(xb, wu_ref[...])
    h = (gate * _sigmoid(gate) * up).astype(BF16)
    acc_ref[...] += _dot(h, wd_ref[...])

    @pl.when(f == pl.num_programs(1) - 1)
    def _():
        o_ref[...] = _layer_norm(ALPHA * x_ref[...] + acc_ref[...], g_ref[...], b_ref[...])


def _ffn_res_ln(x, wg, wu, wd, g, b, tm=512, tf=1408):
    t, d = x.shape
    ff = wg.shape[1]
    row = pl.BlockSpec((tm, d), lambda i, f: (i, 0))
    vec = pl.BlockSpec((1, d), lambda i, f: (0, 0))
    return pl.pallas_call(
        _ffn_kernel,
        grid=(t // tm, ff // tf),
        in_specs=[row,
                  pl.BlockSpec((d, tf), lambda i, f: (0, f)),
                  pl.BlockSpec((d, tf), lambda i, f: (0, f)),
                  pl.BlockSpec((tf, d), lambda i, f: (f, 0)),
                  vec, vec],
        out_specs=row,
        out_shape=jax.ShapeDtypeStruct((t, d), F32),
        scratch_shapes=[pltpu.VMEM((tm, d), BF16), pltpu.VMEM((tm, d), F32)],
        compiler_params=_params("parallel", "arbitrary"),
        name="ffn_res_ln",
    )(x, wg, wu, wd, g, b)


HALO = 8


def _rwkv_proj_kernel(seq_len, x_ref, xp_ref, xn_ref, mu_ref, wrkv_ref, w1_ref, w2_ref, w0_ref,
                      a1_ref, a2_ref, a0_ref, g1_ref, g2_ref, kscale_ref, hs_ref,
                      r_ref, k_ref, kk_ref, v_ref, g_ref, lw0_ref, lw1_ref, ia0_ref, ia1_ref):
    tm = x_ref.shape[0]
    i = pl.program_id(0)
    x = x_ref[...]
    row = lax.broadcasted_iota(jnp.int32, x.shape, 0)
    first_in_seq = (i * tm) % seq_len == 0
    last_in_seq = ((i + 1) * tm) % seq_len == 0
    prev_row = jnp.where(first_in_seq, 0.0, xp_ref[HALO - 1:HALO, :])
    next_row = jnp.where(last_in_seq, 0.0, xn_ref[0:1, :])
    x_prev = jnp.where(row == 0, prev_row, pltpu.roll(x, 1, axis=0))
    x_next = jnp.where(row == tm - 1, next_row, pltpu.roll(x, tm - 1, axis=0))
    xx = 0.5 * (x_prev + x_next) - x

    mixes = [(x + xx * mu_ref[j:j + 1, :]).astype(BF16) for j in range(6)]
    rf = _dot(mixes[0], wrkv_ref[0])
    kf = _dot(mixes[2], wrkv_ref[1])
    vf = _dot(mixes[3], wrkv_ref[2])
    lo_pre = _dot(mixes[1], w1_ref[...])
    al = _dot(mixes[4], a1_ref[...])
    gl_pre = _dot(mixes[5], g1_ref[...])

    r_ref[...] = rf.astype(r_ref.dtype)
    k_ref[...] = kf.astype(k_ref.dtype)
    v_ref[...] = vf.astype(v_ref.dtype)
    lane = lax.broadcasted_iota(jnp.int32, (tm, LANES), 1)
    dir0 = lane < DECAY_LORA
    lo = jnp.tanh(lo_pre)
    lz = [jnp.where(dir0 if z == 0 else ~dir0, lo, 0.0).astype(BF16) for z in range(2)]
    az = [jnp.where(dir0 if z == 0 else ~dir0, al, 0.0).astype(BF16) for z in range(2)]
    gl = _sigmoid(gl_pre).astype(BF16)
    groups = [slice(gi * LANES, (gi + 1) * LANES) for gi in range(D_MODEL // LANES)]
    kkr = [kf[:, cols] * kscale_ref[:, cols] for cols in groups]
    sq = [_split(q * q) for q in kkr]

    wl = [_dot(lz[z], w2_ref[...]) for z in range(2)]
    ai = [_dot(az[z], a2_ref[...]) for z in range(2)]
    gf = _dot(gl, g2_ref[...])
    hsum = hs_ref[...]
    norm2 = [_dot(hi, hsum) + _dot(lo_, hsum) for hi, lo_ in sq]

    for z, out in enumerate((lw0_ref, lw1_ref)):
        out[...] = (-math.exp(-0.5)) * _sigmoid(w0_ref[z:z + 1, :] + wl[z])
    for z, out in enumerate((ia0_ref, ia1_ref)):
        out[...] = _sigmoid(a0_ref[z:z + 1, :] + ai[z]).astype(out.dtype)
    g_ref[...] = gf.astype(g_ref.dtype)
    for cols, q, n2 in zip(groups, kkr, norm2):
        kk_ref[:, cols] = (q / jnp.maximum(jnp.sqrt(n2), 1e-12)).astype(kk_ref.dtype)


def _rwkv_proj(x, seq_len, mu, wrkv, w1, w2, w0, a1, a2, a0, g1, g2, kscale, tm=256):
    t, d = x.shape
    nb = t // HALO
    step = tm // HALO
    row = pl.BlockSpec((tm, d), lambda i: (i, 0))
    full = lambda shape: pl.BlockSpec(shape, lambda i: (0,) * len(shape))
    hsum = _head_sum_matrix(RWKV_HEAD_DIM)
    outs = [jax.ShapeDtypeStruct((t, d), BF16)] * 5 + [jax.ShapeDtypeStruct((t, d), F32)] * 2 + \
           [jax.ShapeDtypeStruct((t, d), BF16)] * 2
    return pl.pallas_call(
        functools.partial(_rwkv_proj_kernel, seq_len),
        grid=(t // tm,),
        in_specs=[row,
                  pl.BlockSpec((HALO, d), lambda i: (jnp.maximum(i * step - 1, 0), 0)),
                  pl.BlockSpec((HALO, d), lambda i: (jnp.minimum((i + 1) * step, nb - 1), 0)),
                  full(mu.shape), full(wrkv.shape), full(w1.shape), full(w2.shape), full(w0.shape),
                  full(a1.shape), full(a2.shape), full(a0.shape), full(g1.shape), full(g2.shape),
                  full(kscale.shape), full(hsum.shape)],
        out_specs=[row] * 9,
        out_shape=outs,
        compiler_params=_params("parallel"),
        name="rwkv_proj",
    )(x, x, x, mu, wrkv, w1, w2, w0, a1, a2, a0, g1, g2, kscale, hsum)


def _pair_blockdiag(y, lane_a):
    zero = jnp.zeros_like(y)
    return jnp.concatenate([jnp.where(lane_a, y, zero), jnp.where(lane_a, zero, y)], axis=0)


def _pair_matmul(x, y, lane_a):
    return _dot(x.astype(BF16), _pair_blockdiag(y.astype(BF16), lane_a))


def _scan_prepare(chains):
    c = SCAN_CHUNK
    t_idx = lax.broadcasted_iota(jnp.int32, (c, LANES), 0)
    s_idx = lax.broadcasted_iota(jnp.int32, (c, LANES), 1) % c
    lane_a = lax.broadcasted_iota(jnp.int32, (c, LANES), 1) < RWKV_HEAD_DIM
    eye = (s_idx == t_idx).astype(F32)
    tri_r = lax.broadcasted_iota(jnp.int32, (c, c), 0)
    tri_c = lax.broadcasted_iota(jnp.int32, (c, c), 1)
    bd = (lax.broadcasted_iota(jnp.int32, (LANES, LANES), 0) // RWKV_HEAD_DIM ==
          lax.broadcasted_iota(jnp.int32, (LANES, LANES), 1) // RWKV_HEAD_DIM)
    before = {rev: ((s_idx > t_idx) if rev else (s_idx < t_idx)) for rev in (False, True)}
    upto = {rev: before[rev] | (s_idx == t_idx) for rev in (False, True)}
    cum = {rev: ((tri_c >= tri_r) if rev else (tri_c <= tri_r)).astype(BF16) for rev in (False, True)}
    n = len(chains)

    lw_split = [jnp.concatenate(_split(ch[6]), axis=1) for ch in chains]
    lwc2 = [_dot(cum[chains[i][0]], lw_split[i]) for i in range(n)]
    lwc = [x[:, 0:LANES] + x[:, LANES:2 * LANES] for x in lwc2]
    vals = [ch[5] for ch in chains]

    at, rt, bh, kh, w_tot, lhs, rhs = [], [], [], [], [], [], []
    for (rev, ka, r, k, kk, v, lw, ia), lc in zip(chains, lwc):
        last = lc[0:1, :] if rev else lc[c - 1:c, :]
        w_in = jnp.exp(lc)
        w_ex = jnp.exp(lc - lw)
        w_inv = jnp.exp(-lc)
        w_end = jnp.exp(last)
        w_out = w_end * w_inv
        kf = k.astype(F32)
        kkf = kk.astype(F32)
        iaf = ia.astype(F32)
        kd = kf * (1.0 + (iaf - 1.0) * ka)
        bvec = kkf * iaf
        a_t = (-kkf) * w_ex
        r_t = r.astype(F32) * w_in
        at.append(a_t)
        rt.append(r_t)
        bh.append((bvec * w_out).astype(BF16))
        kh.append((kd * w_out).astype(BF16))
        w_tot.append(w_end)
        lhs.append(jnp.concatenate([a_t, r_t], axis=0).astype(BF16))
        rhs.append(jnp.concatenate([_pair_blockdiag((bvec * w_inv).astype(BF16), lane_a),
                                    _pair_blockdiag((kd * w_inv).astype(BF16), lane_a)], axis=0))

    gram = [_dot_nt(lhs[i], rhs[i]) for i in range(n)]
    a_ab = [jnp.where(before[chains[i][0]], gram[i][0:c, 0:LANES], 0.0) for i in range(n)]
    a_ak = [jnp.where(before[chains[i][0]], gram[i][0:c, LANES:2 * LANES], 0.0) for i in range(n)]
    a_rb = [jnp.where(upto[chains[i][0]], gram[i][c:2 * c, 0:LANES], 0.0) for i in range(n)]
    a_rk = [jnp.where(upto[chains[i][0]], gram[i][c:2 * c, LANES:2 * LANES], 0.0) for i in range(n)]

    tinv = [eye + a for a in a_ab]
    pw = [_pair_matmul(a, a, lane_a) for a in a_ab]
    n_sq = int(math.log2(c)) - 1
    for j in range(n_sq):
        final = j == n_sq - 1
        lhs_j = [(tinv[i] if final else jnp.concatenate([tinv[i], pw[i]], axis=0)).astype(BF16) for i in range(n)]
        res = [_dot(lhs_j[i], _pair_blockdiag(pw[i].astype(BF16), lane_a)) for i in range(n)]
        tinv = [tinv[i] + res[i][0:c] for i in range(n)]
        if not final:
            pw = [res[i][c:2 * c] for i in range(n)]

    av = [_pair_matmul(a_ak[i], vals[i], lane_a) for i in range(n)]
    y_loc = [_pair_matmul(a_rk[i], vals[i], lane_a) for i in range(n)]
    pq = [_dot(tinv[i].astype(BF16),
               jnp.concatenate([_pair_blockdiag(at[i].astype(BF16), lane_a),
                                _pair_blockdiag(av[i].astype(BF16), lane_a)], axis=1)) for i in range(n)]
    pqb = [x.astype(BF16) for x in pq]
    mn = [_dot_tn(pqb[i], bh[i]) for i in range(n)]
    kv = [_dot_tn(vals[i], kh[i]) for i in range(n)]
    arb = [_dot(a_rb[i].astype(BF16),
                jnp.concatenate([_pair_blockdiag(pqb[i][:, 0:LANES], lane_a),
                                 _pair_blockdiag(pqb[i][:, LANES:2 * LANES], lane_a)], axis=1)) for i in range(n)]
    out = []
    for i in range(n):
        ry = (rt[i] + arb[i][:, 0:LANES]).astype(BF16)
        yc = arb[i][:, LANES:2 * LANES] + y_loc[i]
        m_state = jnp.where(bd, mn[i][0:LANES], 0.0).astype(BF16)
        n_state = jnp.where(bd, mn[i][LANES:2 * LANES] + kv[i], 0.0)
        out.append((ry, yc, m_state, n_state, w_tot[i]))
    return out


def _scan_kernel(n_pairs, rf_ref, kf_ref, kkf_ref, vf_ref, lwf_ref, iaf_ref,
                 rb_ref, kb_ref, kkb_ref, vb_ref, lwb_ref, iab_ref, ka_ref, yf_ref, yb_ref, s_ref):
    @pl.when(pl.program_id(2) == 0)
    def _():
        s_ref[...] = jnp.zeros_like(s_ref)

    c = SCAN_CHUNK
    n_chunks = SCAN_BLOCK // c
    dirs = ((False, (rf_ref, kf_ref, kkf_ref, vf_ref, lwf_ref, iaf_ref), yf_ref),
            (True, (rb_ref, kb_ref, kkb_ref, vb_ref, lwb_ref, iab_ref), yb_ref))
    chains, where = [], []
    for step in range(n_chunks):
        for p in range(n_pairs):
            cols = slice(p * LANES, (p + 1) * LANES)
            for di, (rev, refs, _) in enumerate(dirs):
                ci = n_chunks - 1 - step if rev else step
                rows = slice(ci * c, (ci + 1) * c)
                chains.append((rev, ka_ref[:, cols]) + tuple(ref[rows, cols] for ref in refs))
                where.append((2 * p + di, di, rows, cols))
    pre = _scan_prepare(chains)
    state = [s_ref[j] for j in range(2 * n_pairs)]
    for (j, di, rows, cols), (ry, yc, m_state, n_state, w_total) in zip(where, pre):
        s = state[j]
        sb = s.astype(BF16)
        dirs[di][2][rows, cols] = _dot_nt(ry, sb) + yc
        state[j] = s * w_total + _dot(sb, m_state) + n_state
    for j in range(2 * n_pairs):
        s_ref[j] = state[j]


def _rwkv_scan(r, k, kk, v, lw0, lw1, ia0, ia1, ka, batch, seq_len, n_pairs=8):
    t, d = r.shape
    ns = seq_len // SCAN_BLOCK
    width = n_pairs * LANES
    fwd = pl.BlockSpec((SCAN_BLOCK, width), lambda b, h, s: (b * ns + s, h))
    bwd = pl.BlockSpec((SCAN_BLOCK, width), lambda b, h, s: (b * ns + ns - 1 - s, h))
    vec = pl.BlockSpec((1, width), lambda b, h, s: (0, h))
    return pl.pallas_call(
        functools.partial(_scan_kernel, n_pairs),
        grid=(batch, d // width, ns),
        in_specs=[fwd] * 6 + [bwd] * 6 + [vec],
        out_specs=[fwd, bwd],
        out_shape=[jax.ShapeDtypeStruct((t, d), F32)] * 2,
        scratch_shapes=[pltpu.VMEM((2 * n_pairs, LANES, LANES), F32)],
        compiler_params=_params("parallel", "parallel", "arbitrary"),
        name="rwkv_scan",
    )(r, k, kk, v, lw0, ia0, r, k, kk, v, lw1, ia1, ka)


def _rwkv_out_kernel(yf_ref, yb_ref, r_ref, k_ref, v_ref, g_ref, ia0_ref, ia1_ref, x_ref,
                     lg_ref, lb_ref, rk_ref, ka_ref, hs_ref, wo_ref, ng_ref, nb_ref, o_ref, z_ref):
    hsum = hs_ref[...]
    inv = 1.0 / RWKV_HEAD_DIM
    groups = [slice(gi * LANES, (gi + 1) * LANES) for gi in range(D_MODEL // LANES)]

    def head_sums(vals):
        parts = [_split(v) for v in vals]
        return [_dot(hi, hsum) + _dot(lo, hsum) for hi, lo in parts]

    y = [yf_ref[:, cols] + yb_ref[:, cols] for cols in groups]
    rkk = []
    for cols in groups:
        ia = ia0_ref[:, cols].astype(F32) + ia1_ref[:, cols].astype(F32)
        ksum = k_ref[:, cols].astype(F32) * (2.0 + (ia - 2.0) * ka_ref[:, cols])
        rkk.append(r_ref[:, cols].astype(F32) * ksum * rk_ref[:, cols])
    sums = head_sums(y + rkk)
    n_g = len(groups)
    yc = [y[gi] - sums[gi] * inv for gi in range(n_g)]
    var = head_sums([c * c for c in yc])
    for gi, cols in enumerate(groups):
        yn = yc[gi] * lax.rsqrt(var[gi] * inv + GN_EPS) * lg_ref[:, cols] + lb_ref[:, cols]
        bonus = sums[n_g + gi] * v_ref[:, cols].astype(F32)
        z_ref[:, cols] = ((yn + bonus) * g_ref[:, cols].astype(F32)).astype(BF16)
    h = _dot(z_ref[...], wo_ref[...])
    o_ref[...] = _layer_norm(ALPHA * x_ref[...] + h, ng_ref[...], nb_ref[...])


def _rwkv_out(yf, yb, r, k, v, g, ia0, ia1, x, lnx_g, lnx_b, r_k, k_a, wo, ng, nb, tm=256):
    t, d = x.shape
    row = pl.BlockSpec((tm, d), lambda i: (i, 0))
    vec = pl.BlockSpec((1, d), lambda i: (0, 0))
    return pl.pallas_call(
        _rwkv_out_kernel,
        grid=(t // tm,),
        in_specs=[row] * 9 + [vec] * 4 + [pl.BlockSpec((LANES, LANES), lambda i: (0, 0)),
                                          pl.BlockSpec((d, d), lambda i: (0, 0)), vec, vec],
        out_specs=row,
        out_shape=jax.ShapeDtypeStruct((t, d), F32),
        scratch_shapes=[pltpu.VMEM((tm, d), BF16)],
        compiler_params=_params("parallel"),
        name="rwkv_out",
    )(yf, yb, r, k, v, g, ia0, ia1, x, lnx_g, lnx_b, r_k, k_a, _head_sum_matrix(RWKV_HEAD_DIM), wo, ng, nb)


REC_E1, REC_E2, REC_R1, REC_R2, REC_G1, REC_G2 = range(6)


def _router_kernel(x_ref, w_ref, b_ref, tri_ref, rec_ref, cnt_ref, run_ref):
    @pl.when(pl.program_id(0) == 0)
    def _():
        run_ref[...] = jnp.zeros_like(run_ref)

    xh, xl = _split(x_ref[...])
    wh = w_ref[0]
    wl = w_ref[1]
    logits = _dot(xh, wh) + (_dot(xh, wl) + _dot(xl, wh)) + b_ref[...]
    lane = lax.broadcasted_iota(jnp.int32, logits.shape, 1)
    logits = jnp.where(lane < N_EXPERTS, logits, -jnp.inf)
    m1 = jnp.max(logits, axis=-1, keepdims=True)
    i1 = jnp.min(jnp.where(logits == m1, lane, LANES), axis=-1, keepdims=True)
    rest = jnp.where(lane == i1, -jnp.inf, logits)
    m2 = jnp.max(rest, axis=-1, keepdims=True)
    i2 = jnp.min(jnp.where(rest == m2, lane, LANES), axis=-1, keepdims=True)
    e2 = jnp.exp(m2 - m1)
    g1 = 1.0 / (1.0 + e2)
    g2 = e2 / (1.0 + e2)

    hot1 = lane == i1
    hot2 = lane == i2
    oh1 = hot1.astype(BF16)
    oh2 = hot2.astype(BF16)
    tri = tri_ref[...]
    run = run_ref[...]
    n1 = jnp.sum(oh1.astype(F32), axis=0, keepdims=True)
    n2 = jnp.sum(oh2.astype(F32), axis=0, keepdims=True)
    before1 = run + _dot(tri, oh1)
    before2 = run + n1 + _dot(tri, oh2)
    r1 = jnp.sum(jnp.where(hot1, before1, 0.0), axis=-1, keepdims=True)
    r2 = jnp.sum(jnp.where(hot2, before2, 0.0), axis=-1, keepdims=True)
    run = run + n1 + n2
    run_ref[...] = run
    cnt_ref[...] = run

    rec = jnp.zeros(logits.shape, F32)
    for idx, val in ((REC_E1, i1.astype(F32)), (REC_E2, i2.astype(F32)), (REC_R1, r1), (REC_R2, r2),
                     (REC_G1, g1), (REC_G2, g2)):
        rec = jnp.where(lane == idx, val, rec)
    rec_ref[...] = rec


def _router(x, w_hilo, bias, tm=512):
    t, d = x.shape
    tri = jnp.asarray(np.tril(np.ones((tm, tm), np.float32), -1), BF16)
    return pl.pallas_call(
        _router_kernel,
        grid=(t // tm,),
        in_specs=[pl.BlockSpec((tm, d), lambda i: (i, 0)),
                  pl.BlockSpec((2, d, LANES), lambda i: (0, 0, 0)),
                  pl.BlockSpec((1, LANES), lambda i: (0, 0)),
                  pl.BlockSpec((tm, tm), lambda i: (0, 0))],
        out_specs=[pl.BlockSpec((tm, LANES), lambda i: (i, 0)),
                   pl.BlockSpec((1, LANES), lambda i: (0, 0))],
        out_shape=[jax.ShapeDtypeStruct((t, LANES), F32), jax.ShapeDtypeStruct((1, LANES), F32)],
        scratch_shapes=[pltpu.VMEM((1, LANES), F32)],
        compiler_params=_params("arbitrary"),
        name="moe_router",
    )(x, w_hilo, bias, tri)


EXPERT_TILE = 1024
ROUTE_TILE = 512
TOKEN_ROWS = D_MODEL // LANES
DMA_UNROLL = 8


def _token_copy(src, src_tok, dst, dst_tok, sem):
    def tile(tok):
        return pl.ds(pl.multiple_of(tok * TOKEN_ROWS, TOKEN_ROWS), TOKEN_ROWS)
    return pltpu.make_async_copy(src.at[tile(src_tok)], dst.at[tile(dst_tok)], sem)


def _to_token_tiles(ref, n_tok, value):
    for j in range(TOKEN_ROWS):
        ref[pl.ds(j, n_tok, stride=TOKEN_ROWS), :] = value[:, j * LANES:(j + 1) * LANES]


def _from_token_tiles(ref, n_tok, j):
    return ref[pl.ds(j, n_tok, stride=TOKEN_ROWS), :]


def _dispatch_kernel(pos_ref, x_ref, xs_in, xs_hbm, xt_ref, sem):
    del xs_in
    _to_token_tiles(xt_ref, ROUTE_TILE, x_ref[...])

    def copies(t):
        return (_token_copy(xt_ref, t, xs_hbm, pos_ref[0, 0, t], sem),
                _token_copy(xt_ref, t, xs_hbm, pos_ref[0, 0, ROUTE_TILE + t], sem))

    def start(t, carry):
        for cp in copies(t):
            cp.start()
        return carry

    def wait(t, carry):
        for cp in copies(t):
            cp.wait()
        return carry

    lax.fori_loop(0, ROUTE_TILE, start, 0, unroll=DMA_UNROLL)
    lax.fori_loop(0, ROUTE_TILE, wait, 0, unroll=DMA_UNROLL)


def _moe_dispatch(x, pos, n_rows):
    t, d = x.shape
    return pl.pallas_call(
        _dispatch_kernel,
        grid=(t // ROUTE_TILE,),
        in_specs=[pl.BlockSpec((1, 1, 2 * ROUTE_TILE), lambda i: (i, 0, 0), memory_space=pltpu.SMEM),
                  pl.BlockSpec((ROUTE_TILE, d), lambda i: (i, 0)),
                  pl.BlockSpec(memory_space=pl.ANY)],
        out_specs=pl.BlockSpec(memory_space=pl.ANY),
        out_shape=jax.ShapeDtypeStruct((n_rows * TOKEN_ROWS, LANES), F32),
        scratch_shapes=[pltpu.VMEM((ROUTE_TILE * TOKEN_ROWS, LANES), F32), pltpu.SemaphoreType.DMA(())],
        input_output_aliases={2: 0},
        compiler_params=_params("arbitrary"),
        name="moe_dispatch",
    )(pos, x, jnp.zeros((n_rows * TOKEN_ROWS, LANES), F32))


EXPERT_HEAD_ROWS = EXPERT_TILE // 4


def _experts_kernel(te_ref, na_ref, nv_ref, x_ref, wg_ref, wu_ref, wd_ref, o_ref, xb_ref, acc_ref):
    i = pl.program_id(0)
    f = pl.program_id(1)
    active = i < na_ref[0]
    short = nv_ref[i] <= EXPERT_HEAD_ROWS

    @pl.when(active & (f == 0))
    def _():
        for j in range(TOKEN_ROWS):
            xb_ref[:, j * LANES:(j + 1) * LANES] = _from_token_tiles(x_ref, EXPERT_TILE, j).astype(BF16)
        acc_ref[...] = jnp.zeros_like(acc_ref)

    def swiglu(rows):
        xb = xb_ref[rows, :]
        gate = _dot(xb, wg_ref[...])
        up = _dot(xb, wu_ref[...])
        h = (gate * _sigmoid(gate) * up).astype(BF16)
        acc_ref[rows, :] += _dot(h, wd_ref[...])

    @pl.when(active & jnp.logical_not(short))
    def _():
        swiglu(slice(0, EXPERT_TILE))

    @pl.when(active & short)
    def _():
        swiglu(slice(0, EXPERT_HEAD_ROWS))

    @pl.when(f == pl.num_programs(1) - 1)
    def _():
        _to_token_tiles(o_ref, EXPERT_TILE, jnp.where(active, acc_ref[...], 0.0))


def _moe_experts(xs, tile_expert, n_active, tile_rows, wg, wu, wd, tf=512):
    d = D_MODEL
    n_tiles = xs.shape[0] // (EXPERT_TILE * TOKEN_ROWS)
    fe = wg.shape[2]
    row = pl.BlockSpec((EXPERT_TILE * TOKEN_ROWS, LANES), lambda i, f, te, na, nv: (i, 0))
    return pl.pallas_call(
        _experts_kernel,
        grid_spec=pltpu.PrefetchScalarGridSpec(
            num_scalar_prefetch=3,
            grid=(n_tiles, fe // tf),
            in_specs=[row,
                      pl.BlockSpec((None, d, tf), lambda i, f, te, na, nv: (te[i], 0, f)),
                      pl.BlockSpec((None, d, tf), lambda i, f, te, na, nv: (te[i], 0, f)),
                      pl.BlockSpec((None, tf, d), lambda i, f, te, na, nv: (te[i], f, 0))],
            out_specs=row,
            scratch_shapes=[pltpu.VMEM((EXPERT_TILE, d), BF16), pltpu.VMEM((EXPERT_TILE, d), F32)]),
        out_shape=jax.ShapeDtypeStruct(xs.shape, F32),
        compiler_params=_params("parallel", "arbitrary"),
        name="moe_experts",
    )(tile_expert, n_active, tile_rows, xs, wg, wu, wd)


def _combine_kernel(pos_ref, nxt_ref, ys_hbm, x_ref, rec_ref, g_ref, b_ref, o_ref, y_ref, sem):
    i = pl.program_id(0)
    n = pl.num_programs(0)
    slot = i % 2

    def copies(p_ref, s, t):
        return (_token_copy(ys_hbm, p_ref[0, 0, t], y_ref.at[s, 0], t, sem.at[s]),
                _token_copy(ys_hbm, p_ref[0, 0, ROUTE_TILE + t], y_ref.at[s, 1], t, sem.at[s]))

    def start_all(p_ref, s):
        def body(t, carry):
            for cp in copies(p_ref, s, t):
                cp.start()
            return carry
        lax.fori_loop(0, ROUTE_TILE, body, 0, unroll=DMA_UNROLL)

    @pl.when(i == 0)
    def _():
        start_all(pos_ref, slot)

    @pl.when(i + 1 < n)
    def _():
        start_all(nxt_ref, 1 - slot)

    def wait_body(t, carry):
        for cp in copies(pos_ref, slot, t):
            cp.wait()
        return carry
    lax.fori_loop(0, ROUTE_TILE, wait_body, 0, unroll=DMA_UNROLL)

    rec = rec_ref[...]
    lane = lax.broadcasted_iota(jnp.int32, rec.shape, 1)
    g1 = jnp.sum(jnp.where(lane == REC_G1, rec, 0.0), axis=-1, keepdims=True)
    g2 = jnp.sum(jnp.where(lane == REC_G2, rec, 0.0), axis=-1, keepdims=True)
    y1_ref = y_ref.at[slot, 0]
    y2_ref = y_ref.at[slot, 1]
    f = jnp.concatenate([g1 * _from_token_tiles(y1_ref, ROUTE_TILE, j) + g2 * _from_token_tiles(y2_ref, ROUTE_TILE, j)
                         for j in range(TOKEN_ROWS)], axis=1)
    o_ref[...] = _layer_norm(ALPHA * x_ref[...] + f, g_ref[...], b_ref[...])


def _moe_combine_ln(ys, pos, x, rec, g, b):
    t, d = x.shape
    n = t // ROUTE_TILE
    row = pl.BlockSpec((ROUTE_TILE, d), lambda i: (i, 0))
    vec = pl.BlockSpec((1, d), lambda i: (0, 0))
    return pl.pallas_call(
        _combine_kernel,
        grid=(n,),
        in_specs=[pl.BlockSpec((1, 1, 2 * ROUTE_TILE), lambda i: (i, 0, 0), memory_space=pltpu.SMEM),
                  pl.BlockSpec((1, 1, 2 * ROUTE_TILE), lambda i: (jnp.minimum(i + 1, n - 1), 0, 0),
                               memory_space=pltpu.SMEM),
                  pl.BlockSpec(memory_space=pl.ANY),
                  row,
                  pl.BlockSpec((ROUTE_TILE, LANES), lambda i: (i, 0)),
                  vec, vec],
        out_specs=row,
        out_shape=jax.ShapeDtypeStruct((t, d), F32),
        scratch_shapes=[pltpu.VMEM((2, 2, ROUTE_TILE * TOKEN_ROWS, LANES), F32), pltpu.SemaphoreType.DMA((2,))],
        compiler_params=_params("arbitrary"),
        name="moe_combine_ln",
    )(pos, pos, ys, x, rec, g, b)


def _moe_res_ln(x, rec, counts, wg, wu, wd, g, b):
    t, d = x.shape
    n_tiles = 2 * t // EXPERT_TILE + N_EXPERTS
    cnt = counts[0, :N_EXPERTS].astype(jnp.int32)
    padded = (cnt + EXPERT_TILE - 1) // EXPERT_TILE * EXPERT_TILE
    ends = jnp.cumsum(padded)
    starts = ends - padded
    e1 = rec[:, REC_E1].astype(jnp.int32)
    e2 = rec[:, REC_E2].astype(jnp.int32)
    pos1 = starts[e1] + rec[:, REC_R1].astype(jnp.int32)
    pos2 = starts[e2] + rec[:, REC_R2].astype(jnp.int32)
    pos = jnp.concatenate([pos1.reshape(-1, 1, ROUTE_TILE), pos2.reshape(-1, 1, ROUTE_TILE)], axis=2)
    tile_start = jnp.arange(n_tiles, dtype=jnp.int32) * EXPERT_TILE
    tile_expert = jnp.minimum(jnp.sum(tile_start[:, None] >= ends[None, :], axis=1), N_EXPERTS - 1).astype(jnp.int32)
    n_active = (ends[-1:] // EXPERT_TILE).astype(jnp.int32)
    tile_rows = jnp.clip(starts[tile_expert] + cnt[tile_expert] - tile_start, 0, EXPERT_TILE).astype(jnp.int32)
    last_expert = tile_expert[jnp.maximum(n_active[0] - 1, 0)]
    tile_expert = jnp.where(jnp.arange(n_tiles) < n_active[0], tile_expert, last_expert)

    xs = _moe_dispatch(x, pos, n_tiles * EXPERT_TILE)
    ys = _moe_experts(xs, tile_expert, n_active, tile_rows, wg, wu, wd)
    return _moe_combine_ln(ys, pos, x, rec, g, b)


def _prepare_weights(na_w_qkv, na_rpb, na_w_o, ffn_w_gate, ffn_w_up, ffn_w_down,
                     rwkv_mu, rwkv_w_rkv, rwkv_w0, rwkv_w1, rwkv_w2, rwkv_a0, rwkv_a1, rwkv_a2,
                     rwkv_g1, rwkv_g2, rwkv_k_k, rwkv_k_a, rwkv_r_k, rwkv_lnx_g, rwkv_lnx_b, rwkv_w_o,
                     moe_w_router, moe_b_router, moe_w_gate, moe_w_up, moe_w_down,
                     ln_mix_g, ln_mix_b, ln_ffn_g, ln_ffn_b):
    d = D_MODEL
    vec = lambda a: a.reshape(1, d).astype(F32)
    gl_pad = 2 * LANES - GATE_LORA
    router_w = jnp.pad(moe_w_router[0], ((0, 0), (0, LANES - N_EXPERTS)))
    router_hi = router_w.astype(BF16)
    router_lo = (router_w - router_hi.astype(F32)).astype(BF16)
    return dict(
        qkv_w=na_w_qkv[0].astype(BF16),
        qkv_scale=jnp.concatenate([jnp.full((1, d), NA_HEAD_DIM ** -0.5 * LOG2E, F32), jnp.ones((1, 2 * d), F32)], axis=1),
        na_bias=_na_bias_table(na_rpb[0]),
        na_wo=na_w_o[0].astype(BF16),
        ffn_wg=ffn_w_gate[0].astype(BF16), ffn_wu=ffn_w_up[0].astype(BF16), ffn_wd=ffn_w_down[0].astype(BF16),
        mu=rwkv_mu[0].astype(F32),
        wrkv=rwkv_w_rkv[0].astype(BF16),
        w1=jnp.concatenate([rwkv_w1[0, 0], rwkv_w1[0, 1]], axis=1).astype(BF16),
        w2=jnp.concatenate([rwkv_w2[0, 0], rwkv_w2[0, 1]], axis=0).astype(BF16),
        w0=rwkv_w0[0].astype(F32),
        a1=jnp.concatenate([rwkv_a1[0, 0], rwkv_a1[0, 1]], axis=1).astype(BF16),
        a2=jnp.concatenate([rwkv_a2[0, 0], rwkv_a2[0, 1]], axis=0).astype(BF16),
        a0=rwkv_a0[0].astype(F32),
        g1=jnp.pad(rwkv_g1[0], ((0, 0), (0, gl_pad))).astype(BF16),
        g2=jnp.pad(rwkv_g2[0], ((0, gl_pad), (0, 0))).astype(BF16),
        k_k=vec(rwkv_k_k[0]), k_a=vec(rwkv_k_a[0]), r_k=vec(rwkv_r_k[0]),
        lnx_g=vec(rwkv_lnx_g[0]), lnx_b=vec(rwkv_lnx_b[0]),
        rwkv_wo=rwkv_w_o[0].astype(BF16),
        router_w=jnp.stack([router_hi, router_lo]),
        router_b=jnp.pad(moe_b_router[0], (0, LANES - N_EXPERTS)).reshape(1, LANES).astype(F32),
        moe_wg=moe_w_gate[0].astype(BF16), moe_wu=moe_w_up[0].astype(BF16), moe_wd=moe_w_down[0].astype(BF16),
        ln_mix_g=ln_mix_g.astype(F32), ln_mix_b=ln_mix_b.astype(F32),
        ln_ffn_g=ln_ffn_g.astype(F32), ln_ffn_b=ln_ffn_b.astype(F32),
    )


def _trunk(x3, w):
    batch, seq_len, d = x3.shape
    rows = seq_len // GRID_W
    x = x3.reshape(batch * seq_len, d)
    ln = lambda a, i: a[i].reshape(1, d)

    qkv = _qkv_proj(x, w["qkv_w"], w["qkv_scale"])
    att = _na_attention(qkv, w["na_bias"], batch, rows)
    x = _proj_res_ln(att, w["na_wo"], x, ln(w["ln_mix_g"], 0), ln(w["ln_mix_b"], 0))
    x = _ffn_res_ln(x, w["ffn_wg"], w["ffn_wu"], w["ffn_wd"], ln(w["ln_ffn_g"], 0), ln(w["ln_ffn_b"], 0))

    r, k, kk, v, g, lw0, lw1, ia0, ia1 = _rwkv_proj(x, seq_len, w["mu"], w["wrkv"], w["w1"], w["w2"], w["w0"],
                                                    w["a1"], w["a2"], w["a0"], w["g1"], w["g2"], w["k_k"])
    yf, yb = _rwkv_scan(r, k, kk, v, lw0, lw1, ia0, ia1, w["k_a"], batch, seq_len)
    x = _rwkv_out(yf, yb, r, k, v, g, ia0, ia1, x, w["lnx_g"], w["lnx_b"], w["r_k"], w["k_a"], w["rwkv_wo"],
                  ln(w["ln_mix_g"], 1), ln(w["ln_mix_b"], 1))
    rec, counts = _router(x, w["router_w"], w["router_b"])
    x = _moe_res_ln(x, rec, counts, w["moe_wg"], w["moe_wu"], w["moe_wd"], ln(w["ln_ffn_g"], 1), ln(w["ln_ffn_b"], 1))
    return x.reshape(batch, seq_len, d)


def kernel(x_prompt, x_sample, na_w_qkv, na_rpb, na_w_o, ffn_w_gate, ffn_w_up, ffn_w_down, rwkv_mu, rwkv_w_rkv, rwkv_w0, rwkv_w1, rwkv_w2, rwkv_a0, rwkv_a1, rwkv_a2, rwkv_g1, rwkv_g2, rwkv_k_k, rwkv_k_a, rwkv_r_k, rwkv_lnx_g, rwkv_lnx_b, rwkv_w_o, moe_w_router, moe_b_router, moe_w_gate, moe_w_up, moe_w_down, ln_mix_g, ln_mix_b, ln_ffn_g, ln_ffn_b):
    w = _prepare_weights(na_w_qkv, na_rpb, na_w_o, ffn_w_gate, ffn_w_up, ffn_w_down,
                         rwkv_mu, rwkv_w_rkv, rwkv_w0, rwkv_w1, rwkv_w2, rwkv_a0, rwkv_a1, rwkv_a2,
                         rwkv_g1, rwkv_g2, rwkv_k_k, rwkv_k_a, rwkv_r_k, rwkv_lnx_g, rwkv_lnx_b, rwkv_w_o,
                         moe_w_router, moe_b_router, moe_w_gate, moe_w_up, moe_w_down,
                         ln_mix_g, ln_mix_b, ln_ffn_g, ln_ffn_b)
    return (_trunk(x_prompt, w), _trunk(x_sample, w))
```

```python
import functools
import math

import numpy as np
import jax
import jax.numpy as jnp
from jax import lax
from jax.experimental import pallas as pl
from jax.experimental.pallas import tpu as pltpu

D_MODEL = 1024
DEPTH = 2
GRID_W = 64
NA_HEAD_DIM = 32
NA_HEADS = D_MODEL // NA_HEAD_DIM
NA_ROWS = 8
NA_COLS = 16
RWKV_HEAD_DIM = 64
DECAY_LORA = 64
GATE_LORA = 160
GN_EPS = 64e-5
N_EXPERTS = 8
LN_EPS = 1e-5
ALPHA = (2 * DEPTH) ** 0.25
LOG2E = math.log2(math.e)

LANES = 128
SCAN_CHUNK = 64
SCAN_BLOCK = 256
VMEM_LIMIT = 56 * 1024 * 1024

F32 = jnp.float32
BF16 = jnp.bfloat16


def _params(*sem):
    return pltpu.CompilerParams(dimension_semantics=sem, vmem_limit_bytes=VMEM_LIMIT)


def _dot(a, b):
    return jnp.dot(a, b, preferred_element_type=F32)


def _dot_nt(a, b):
    return lax.dot_general(a, b, (((1,), (1,)), ((), ())), preferred_element_type=F32)


def _dot_tn(a, b):
    return lax.dot_general(a, b, (((0,), (0,)), ((), ())), preferred_element_type=F32)


def _split(a):
    hi = a.astype(BF16)
    lo = (a - hi.astype(F32)).astype(BF16)
    return hi, lo


def _layer_norm(z, g, b):
    mu = jnp.mean(z, axis=-1, keepdims=True)
    zc = z - mu
    var = jnp.mean(zc * zc, axis=-1, keepdims=True)
    return zc * lax.rsqrt(var + LN_EPS) * g + b


def _sigmoid(z):
    return 1.0 / (1.0 + jnp.exp(-z))


def _head_sum_matrix(head_dim):
    lane = np.arange(LANES)
    return jnp.asarray((lane[:, None] // head_dim) == (lane[None, :] // head_dim), BF16)


def _qkv_kernel(x_ref, w_ref, s_ref, o_ref):
    acc = _dot(x_ref[...].astype(BF16), w_ref[...])
    o_ref[...] = (acc * s_ref[...]).astype(o_ref.dtype)


def _qkv_proj(x, w, colscale, tm=512):
    t, d = x.shape
    n = w.shape[1]
    return pl.pallas_call(
        _qkv_kernel,
        grid=(t // tm,),
        in_specs=[pl.BlockSpec((tm, d), lambda i: (i, 0)),
                  pl.BlockSpec((d, n), lambda i: (0, 0)),
                  pl.BlockSpec((1, n), lambda i: (0, 0))],
        out_specs=pl.BlockSpec((tm, n), lambda i: (i, 0)),
        out_shape=jax.ShapeDtypeStruct((t, n), BF16),
        compiler_params=_params("parallel"),
        name="qkv_proj",
    )(x, w, colscale)


NA_HEADS_PER_GROUP = LANES // NA_HEAD_DIM
NA_GROUPS = D_MODEL // LANES
NA_WIN = NA_ROWS * GRID_W


NA_GROUP_ROWS = NA_HEADS_PER_GROUP * GRID_W
NA_ROW_PAIRS = NA_ROWS // 2
NA_BIAS_PAIRS = 2 * NA_ROWS - 2


NA_STEP_ROWS = 4
NA_FETCH_ROWS = NA_ROWS + NA_STEP_ROWS - 1


def _na_win_start(r, rows):
    return jnp.clip(r - NA_ROWS // 2, 0, rows - NA_ROWS)


def _na_fetch_start(r0, rows):
    return jnp.minimum(_na_win_start(r0, rows), rows - NA_FETCH_ROWS)


def _na_kernel(rows, q_ref, k_ref, v_ref, b_ref, o_ref):
    r0 = pl.program_id(1) * NA_STEP_ROWS
    fetch = _na_fetch_start(r0, rows)
    lane_head = lax.broadcasted_iota(jnp.int32, (GRID_W, LANES), 1) // NA_HEAD_DIM
    groups = [slice(g * LANES, (g + 1) * LANES) for g in range(NA_GROUPS)]
    units = []
    for ri in range(NA_STEP_ROWS):
        win = _na_win_start(r0 + ri, rows)
        keys = pl.ds(pl.multiple_of((win - fetch) * GRID_W, GRID_W), NA_WIN)
        for g in range(NA_GROUPS):
            units.append((slice(ri * GRID_W, (ri + 1) * GRID_W), keys, r0 + ri - win, g))

    scores = []
    for qrows, keys, _, g in units:
        qg = q_ref[qrows, groups[g]]
        zero = jnp.zeros_like(qg)
        qm = jnp.concatenate([jnp.where(lane_head == h, qg, zero) for h in range(NA_HEADS_PER_GROUP)], axis=0)
        scores.append(_dot_nt(qm, k_ref[keys, groups[g]]))

    probs, denoms = [], []
    for (_, _, off, g), s in zip(units, scores):
        hq = pl.ds(g * NA_GROUP_ROWS, NA_GROUP_ROWS)
        s = jnp.concatenate([s[:, m * LANES:(m + 1) * LANES] + b_ref[2 * m - off + NA_ROWS - 1, hq, :]
                             for m in range(NA_ROW_PAIRS)], axis=1)
        p = jnp.exp2(s - jnp.max(s, axis=-1, keepdims=True))
        denoms.append(jnp.sum(p, axis=-1, keepdims=True))
        probs.append(p.astype(BF16))

    for (qrows, keys, _, g), p, l in zip(units, probs, denoms):
        o4 = _dot(p, v_ref[keys, groups[g]]) / l
        og = jnp.zeros((GRID_W, LANES), F32)
        for h in range(NA_HEADS_PER_GROUP):
            og = og + jnp.where(lane_head == h, o4[h * GRID_W:(h + 1) * GRID_W], 0.0)
        o_ref[qrows, groups[g]] = og.astype(o_ref.dtype)


def _na_attention(qkv, bias, batch, rows):
    t = qkv.shape[0]
    steps = rows // NA_STEP_ROWS
    q_rows = NA_STEP_ROWS * GRID_W

    def q_map(b, s):
        return (b * steps + s, 0)

    def k_map(b, s):
        return ((b * rows + _na_fetch_start(s * NA_STEP_ROWS, rows)) * GRID_W, D_MODEL)

    def v_map(b, s):
        return ((b * rows + _na_fetch_start(s * NA_STEP_ROWS, rows)) * GRID_W, 2 * D_MODEL)

    window = (pl.Element(NA_FETCH_ROWS * GRID_W), pl.Element(D_MODEL))
    return pl.pallas_call(
        functools.partial(_na_kernel, rows),
        grid=(batch, steps),
        in_specs=[pl.BlockSpec((q_rows, D_MODEL), q_map),
                  pl.BlockSpec(window, k_map),
                  pl.BlockSpec(window, v_map),
                  pl.BlockSpec(bias.shape, lambda b, s: (0, 0, 0), pipeline_mode=pl.Buffered(1))],
        out_specs=pl.BlockSpec((q_rows, D_MODEL), q_map),
        out_shape=jax.ShapeDtypeStruct((t, D_MODEL), BF16),
        compiler_params=_params("parallel", "arbitrary"),
        name="na_attention",
    )(qkv, qkv, qkv, bias)


def _na_bias_kernel(rpb_ref, oh_ref, mask_ref, o_ref):
    x = rpb_ref[...]
    hi = x.astype(BF16)
    r1 = x - hi.astype(F32)
    mid = r1.astype(BF16)
    lo = (r1 - mid.astype(F32)).astype(BF16)
    oh = oh_ref[...]
    o_ref[...] = (_dot(hi, oh) + _dot(mid, oh) + _dot(lo, oh)) * LOG2E + mask_ref[...]


def _na_bias_table(rpb, tn=1024):
    n_dr, n_dc = 2 * NA_ROWS - 1, 2 * NA_COLS - 1
    c = np.arange(GRID_W)
    q_start = np.clip(c - NA_COLS // 2, 0, GRID_W - NA_COLS)
    kc = np.arange(GRID_W)
    valid = (kc[None, :] >= q_start[:, None]) & (kc[None, :] < q_start[:, None] + NA_COLS)
    dc = kc[None, :] - c[:, None] + NA_COLS - 1
    onehot = (np.arange(LANES)[:, None, None] == dc[None]) & valid[None]
    onehot = jnp.asarray(onehot.reshape(LANES, GRID_W * GRID_W), BF16)
    mask = jnp.asarray(np.where(valid, 0.0, -np.inf).reshape(1, GRID_W * GRID_W), F32)
    rpb2 = jnp.pad(rpb.reshape(NA_HEADS * n_dr, n_dc).astype(F32), ((0, 0), (0, LANES - n_dc)))
    nrow, ncol = rpb2.shape[0], GRID_W * GRID_W
    flat = pl.pallas_call(
        _na_bias_kernel,
        grid=(ncol // tn,),
        in_specs=[pl.BlockSpec((nrow, LANES), lambda j: (0, 0)),
                  pl.BlockSpec((LANES, tn), lambda j: (0, j)),
                  pl.BlockSpec((1, tn), lambda j: (0, j))],
        out_specs=pl.BlockSpec((nrow, tn), lambda j: (0, j)),
        out_shape=jax.ShapeDtypeStruct((nrow, ncol), F32),
        compiler_params=_params("parallel"),
        name="na_bias_table",
    )(rpb2, onehot, mask)
    toe = flat.reshape(NA_HEADS, n_dr, GRID_W, GRID_W)
    pairs = jnp.stack([toe[:, 0:NA_BIAS_PAIRS], toe[:, 1:NA_BIAS_PAIRS + 1]], axis=3)
    return jnp.transpose(pairs, (1, 0, 2, 3, 4)).reshape(NA_BIAS_PAIRS, NA_HEADS * GRID_W, LANES)


def _proj_ln_kernel(a_ref, w_ref, x_ref, g_ref, b_ref, o_ref):
    h = _dot(a_ref[...], w_ref[...])
    o_ref[...] = _layer_norm(ALPHA * x_ref[...] + h, g_ref[...], b_ref[...])


def _proj_res_ln(a, w, x, g, b, tm=512):
    t, d = x.shape
    row = pl.BlockSpec((tm, d), lambda i: (i, 0))
    vec = pl.BlockSpec((1, d), lambda i: (0, 0))
    return pl.pallas_call(
        _proj_ln_kernel,
        grid=(t // tm,),
        in_specs=[row, pl.BlockSpec((d, d), lambda i: (0, 0)), row, vec, vec],
        out_specs=row,
        out_shape=jax.ShapeDtypeStruct((t, d), F32),
        compiler_params=_params("parallel"),
        name="proj_res_ln",
    )(a, w, x, g, b)


def _ffn_kernel(x_ref, wg_ref, wu_ref, wd_ref, g_ref, b_ref, o_ref, xb_ref, acc_ref):
    f = pl.program_id(1)

    @pl.when(f == 0)
    def _():
        xb_ref[...] = x_ref[...].astype(BF16)
        acc_ref[...] = jnp.zeros_like(acc_ref)

    xb = xb_ref[...]
    gate = _dot(xb, wg_ref[...])
    up = _dot(xb, wu_ref[...])
    h = (gate * _sigmoid(gate) * up).astype(BF16)
    acc_ref[...] += _dot(h, wd_ref[...])

    @pl.when(f == pl.num_programs(1) - 1)
    def _():
        o_ref[...] = _layer_norm(ALPHA * x_ref[...] + acc_ref[...], g_ref[...], b_ref[...])


def _ffn_res_ln(x, wg, wu, wd, g, b, tm=512, tf=1408):
    t, d = x.shape
    ff = wg.shape[1]
    row = pl.BlockSpec((tm, d), lambda i, f: (i, 0))
    vec = pl.BlockSpec((1, d), lambda i, f: (0, 0))
    return pl.pallas_call(
        _ffn_kernel,
        grid=(t // tm, ff // tf),
        in_specs=[row,
                  pl.BlockSpec((d, tf), lambda i, f: (0, f)),
                  pl.BlockSpec((d, tf), lambda i, f: (0, f)),
                  pl.BlockSpec((tf, d), lambda i, f: (f, 0)),
                  vec, vec],
        out_specs=row,
        out_shape=jax.ShapeDtypeStruct((t, d), F32),
        scratch_shapes=[pltpu.VMEM((tm, d), BF16), pltpu.VMEM((tm, d), F32)],
        compiler_params=_params("parallel", "arbitrary"),
        name="ffn_res_ln",
    )(x, wg, wu, wd, g, b)


HALO = 8


def _rwkv_proj_kernel(seq_len, x_ref, xp_ref, xn_ref, mu_ref, wrkv_ref, w1_ref, w2_ref, w0_ref,
                      a1_ref, a2_ref, a0_ref, g1_ref, g2_ref, kscale_ref, hs_ref,
                      r_ref, k_ref, kk_ref, v_ref, g_ref, lw0_ref, lw1_ref, ia0_ref, ia1_ref):
    tm = x_ref.shape[0]
    i = pl.program_id(0)
    x = x_ref[...]
    row = lax.broadcasted_iota(jnp.int32, x.shape, 0)
    first_in_seq = (i * tm) % seq_len == 0
    last_in_seq = ((i + 1) * tm) % seq_len == 0
    prev_row = jnp.where(first_in_seq, 0.0, xp_ref[HALO - 1:HALO, :])
    next_row = jnp.where(last_in_seq, 0.0, xn_ref[0:1, :])
    x_prev = jnp.where(row == 0, prev_row, pltpu.roll(x, 1, axis=0))
    x_next = jnp.where(row == tm - 1, next_row, pltpu.roll(x, tm - 1, axis=0))
    xx = 0.5 * (x_prev + x_next) - x

    mixes = [(x + xx * mu_ref[j:j + 1, :]).astype(BF16) for j in range(6)]
    rf = _dot(mixes[0], wrkv_ref[0])
    kf = _dot(mixes[2], wrkv_ref[1])
    vf = _dot(mixes[3], wrkv_ref[2])
    lo_pre = _dot(mixes[1], w1_ref[...])
    al = _dot(mixes[4], a1_ref[...])
    gl_pre = _dot(mixes[5], g1_ref[...])

    r_ref[...] = rf.astype(r_ref.dtype)
    k_ref[...] = kf.astype(k_ref.dtype)
    v_ref[...] = vf.astype(v_ref.dtype)
    lane = lax.broadcasted_iota(jnp.int32, (tm, LANES), 1)
    dir0 = lane < DECAY_LORA
    lo = jnp.tanh(lo_pre)
    lz = [jnp.where(dir0 if z == 0 else ~dir0, lo, 0.0).astype(BF16) for z in range(2)]
    az = [jnp.where(dir0 if z == 0 else ~dir0, al, 0.0).astype(BF16) for z in range(2)]
    gl = _sigmoid(gl_pre).astype(BF16)
    groups = [slice(gi * LANES, (gi + 1) * LANES) for gi in range(D_MODEL // LANES)]
    kkr = [kf[:, cols] * kscale_ref[:, cols] for cols in groups]
    sq = [_split(q * q) for q in kkr]

    wl = [_dot(lz[z], w2_ref[...]) for z in range(2)]
    ai = [_dot(az[z], a2_ref[...]) for z in range(2)]
    gf = _dot(gl, g2_ref[...])
    hsum = hs_ref[...]
    norm2 = [_dot(hi, hsum) + _dot(lo_, hsum) for hi, lo_ in sq]

    for z, out in enumerate((lw0_ref, lw1_ref)):
        out[...] = (-math.exp(-0.5)) * _sigmoid(w0_ref[z:z + 1, :] + wl[z])
    for z, out in enumerate((ia0_ref, ia1_ref)):
        out[...] = _sigmoid(a0_ref[z:z + 1, :] + ai[z]).astype(out.dtype)
    g_ref[...] = gf.astype(g_ref.dtype)
    for cols, q, n2 in zip(groups, kkr, norm2):
        kk_ref[:, cols] = (q / jnp.maximum(jnp.sqrt(n2), 1e-12)).astype(kk_ref.dtype)


def _rwkv_proj(x, seq_len, mu, wrkv, w1, w2, w0, a1, a2, a0, g1, g2, kscale, tm=256):
    t, d = x.shape
    nb = t // HALO
    step = tm // HALO
    row = pl.BlockSpec((tm, d), lambda i: (i, 0))
    full = lambda shape: pl.BlockSpec(shape, lambda i: (0,) * len(shape))
    hsum = _head_sum_matrix(RWKV_HEAD_DIM)
    outs = [jax.ShapeDtypeStruct((t, d), BF16)] * 5 + [jax.ShapeDtypeStruct((t, d), F32)] * 2 + \
           [jax.ShapeDtypeStruct((t, d), BF16)] * 2
    return pl.pallas_call(
        functools.partial(_rwkv_proj_kernel, seq_len),
        grid=(t // tm,),
        in_specs=[row,
                  pl.BlockSpec((HALO, d), lambda i: (jnp.maximum(i * step - 1, 0), 0)),
                  pl.BlockSpec((HALO, d), lambda i: (jnp.minimum((i + 1) * step, nb - 1), 0)),
                  full(mu.shape), full(wrkv.shape), full(w1.shape), full(w2.shape), full(w0.shape),
                  full(a1.shape), full(a2.shape), full(a0.shape), full(g1.shape), full(g2.shape),
                  full(kscale.shape), full(hsum.shape)],
        out_specs=[row] * 9,
        out_shape=outs,
        compiler_params=_params("parallel"),
        name="rwkv_proj",
    )(x, x, x, mu, wrkv, w1, w2, w0, a1, a2, a0, g1, g2, kscale, hsum)


def _pair_blockdiag(y, lane_a):
    zero = jnp.zeros_like(y)
    return jnp.concatenate([jnp.where(lane_a, y, zero), jnp.where(lane_a, zero, y)], axis=0)


def _pair_matmul(x, y, lane_a):
    return _dot(x.astype(BF16), _pair_blockdiag(y.astype(BF16), lane_a))


def _scan_prepare(chains):
    c = SCAN_CHUNK
    t_idx = lax.broadcasted_iota(jnp.int32, (c, LANES), 0)
    s_idx = lax.broadcasted_iota(jnp.int32, (c, LANES), 1) % c
    lane_a = lax.broadcasted_iota(jnp.int32, (c, LANES), 1) < RWKV_HEAD_DIM
    eye = (s_idx == t_idx).astype(F32)
    tri_r = lax.broadcasted_iota(jnp.int32, (c, c), 0)
    tri_c = lax.broadcasted_iota(jnp.int32, (c, c), 1)
    bd = (lax.broadcasted_iota(jnp.int32, (LANES, LANES), 0) // RWKV_HEAD_DIM ==
          lax.broadcasted_iota(jnp.int32, (LANES, LANES), 1) // RWKV_HEAD_DIM)
    before = {rev: ((s_idx > t_idx) if rev else (s_idx < t_idx)) for rev in (False, True)}
    upto = {rev: before[rev] | (s_idx == t_idx) for rev in (False, True)}
    cum = {rev: ((tri_c >= tri_r) if rev else (tri_c <= tri_r)).astype(BF16) for rev in (False, True)}
    n = len(chains)

    lw_split = [jnp.concatenate(_split(ch[6]), axis=1) for ch in chains]
    lwc2 = [_dot(cum[chains[i][0]], lw_split[i]) for i in range(n)]
    lwc = [x[:, 0:LANES] + x[:, LANES:2 * LANES] for x in lwc2]
    vals = [ch[5] for ch in chains]

    at, rt, bh, kh, w_tot, lhs, rhs = [], [], [], [], [], [], []
    for (rev, ka, r, k, kk, v, lw, ia), lc in zip(chains, lwc):
        last = lc[0:1, :] if rev else lc[c - 1:c, :]
        w_in = jnp.exp(lc)
        w_ex = jnp.exp(lc - lw)
        w_inv = jnp.exp(-lc)
        w_end = jnp.exp(last)
        w_out = w_end * w_inv
        kf = k.astype(F32)
        kkf = kk.astype(F32)
        iaf = ia.astype(F32)
        kd = kf * (1.0 + (iaf - 1.0) * ka)
        bvec = kkf * iaf
        a_t = (-kkf) * w_ex
        r_t = r.astype(F32) * w_in
        at.append(a_t)
        rt.append(r_t)
        bh.append((bvec * w_out).astype(BF16))
        kh.append((kd * w_out).astype(BF16))
        w_tot.append(w_end)
        lhs.append(jnp.concatenate([a_t, r_t], axis=0).astype(BF16))
        rhs.append(jnp.concatenate([_pair_blockdiag((bvec * w_inv).astype(BF16), lane_a),
                                    _pair_blockdiag((kd * w_inv).astype(BF16), lane_a)], axis=0))

    gram = [_dot_nt(lhs[i], rhs[i]) for i in range(n)]
    a_ab = [jnp.where(before[chains[i][0]], gram[i][0:c, 0:LANES], 0.0) for i in range(n)]
    a_ak = [jnp.where(before[chains[i][0]], gram[i][0:c, LANES:2 * LANES], 0.0) for i in range(n)]
    a_rb = [jnp.where(upto[chains[i][0]], gram[i][c:2 * c, 0:LANES], 0.0) for i in range(n)]
    a_rk = [jnp.where(upto[chains[i][0]], gram[i][c:2 * c, LANES:2 * LANES], 0.0) for i in range(n)]

    tinv = [eye + a for a in a_ab]
    pw = [_pair_matmul(a, a, lane_a) for a in a_ab]
    n_sq = int(math.log2(c)) - 1
    for j in range(n_sq):
        final = j == n_sq - 1
        lhs_j = [(tinv[i] if final else jnp.concatenate([tinv[i], pw[i]], axis=0)).astype(BF16) for i in range(n)]
        res = [_dot(lhs_j[i], _pair_blockdiag(pw[i].astype(BF16), lane_a)) for i in range(n)]
        tinv = [tinv[i] + res[i][0:c] for i in range(n)]
        if not final:
            pw = [res[i][c:2 * c] for i in range(n)]

    av = [_pair_matmul(a_ak[i], vals[i], lane_a) for i in range(n)]
    y_loc = [_pair_matmul(a_rk[i], vals[i], lane_a) for i in range(n)]
    pq = [_dot(tinv[i].astype(BF16),
               jnp.concatenate([_pair_blockdiag(at[i].astype(BF16), lane_a),
                                _pair_blockdiag(av[i].astype(BF16), lane_a)], axis=1)) for i in range(n)]
    pqb = [x.astype(BF16) for x in pq]
    mn = [_dot_tn(pqb[i], bh[i]) for i in range(n)]
    kv = [_dot_tn(vals[i], kh[i]) for i in range(n)]
    arb = [_dot(a_rb[i].astype(BF16),
                jnp.concatenate([_pair_blockdiag(pqb[i][:, 0:LANES], lane_a),
                                 _pair_blockdiag(pqb[i][:, LANES:2 * LANES], lane_a)], axis=1)) for i in range(n)]
    out = []
    for i in range(n):
        ry = (rt[i] + arb[i][:, 0:LANES]).astype(BF16)
        yc = arb[i][:, LANES:2 * LANES] + y_loc[i]
        m_state = jnp.where(bd, mn[i][0:LANES], 0.0).astype(BF16)
        n_state = jnp.where(bd, mn[i][LANES:2 * LANES] + kv[i], 0.0)
        out.append((ry, yc, m_state, n_state, w_tot[i]))
    return out


def _scan_kernel(n_pairs, rf_ref, kf_ref, kkf_ref, vf_ref, lwf_ref, iaf_ref,
                 rb_ref, kb_ref, kkb_ref, vb_ref, lwb_ref, iab_ref, ka_ref, yf_ref, yb_ref, s_ref):
    @pl.when(pl.program_id(2) == 0)
    def _():
        s_ref[...] = jnp.zeros_like(s_ref)

    c = SCAN_CHUNK
    n_chunks = SCAN_BLOCK // c
    dirs = ((False, (rf_ref, kf_ref, kkf_ref, vf_ref, lwf_ref, iaf_ref), yf_ref),
            (True, (rb_ref, kb_ref, kkb_ref, vb_ref, lwb_ref, iab_ref), yb_ref))
    chains, where = [], []
    for step in range(n_chunks):
        for p in range(n_pairs):
            cols = slice(p * LANES, (p + 1) * LANES)
            for di, (rev, refs, _) in enumerate(dirs):
                ci = n_chunks - 1 - step if rev else step
                rows = slice(ci * c, (ci + 1) * c)
                chains.append((rev, ka_ref[:, cols]) + tuple(ref[rows, cols] for ref in refs))
                where.append((2 * p + di, di, rows, cols))
    pre = _scan_prepare(chains)
    state = [s_ref[j] for j in range(2 * n_pairs)]
    for (j, di, rows, cols), (ry, yc, m_state, n_state, w_total) in zip(where, pre):
        s = state[j]
        sb = s.astype(BF16)
        dirs[di][2][rows, cols] = _dot_nt(ry, sb) + yc
        state[j] = s * w_total + _dot(sb, m_state) + n_state
    for j in range(2 * n_pairs):
        s_ref[j] = state[j]


def _rwkv_scan(r, k, kk, v, lw0, lw1, ia0, ia1, ka, batch, seq_len, n_pairs=8):
    t, d = r.shape
    ns = seq_len // SCAN_BLOCK
    width = n_pairs * LANES
    fwd = pl.BlockSpec((SCAN_BLOCK, width), lambda b, h, s: (b * ns + s, h))
    bwd = pl.BlockSpec((SCAN_BLOCK, width), lambda b, h, s: (b * ns + ns - 1 - s, h))
    vec = pl.BlockSpec((1, width), lambda b, h, s: (0, h))
    return pl.pallas_call(
        functools.partial(_scan_kernel, n_pairs),
        grid=(batch, d // width, ns),
        in_specs=[fwd] * 6 + [bwd] * 6 + [vec],
        out_specs=[fwd, bwd],
        out_shape=[jax.ShapeDtypeStruct((t, d), F32)] * 2,
        scratch_shapes=[pltpu.VMEM((2 * n_pairs, LANES, LANES), F32)],
        compiler_params=_params("parallel", "parallel", "arbitrary"),
        name="rwkv_scan",
    )(r, k, kk, v, lw0, ia0, r, k, kk, v, lw1, ia1, ka)


def _rwkv_out_kernel(yf_ref, yb_ref, r_ref, k_ref, v_ref, g_ref, ia0_ref, ia1_ref, x_ref,
                     lg_ref, lb_ref, rk_ref, ka_ref, hs_ref, wo_ref, ng_ref, nb_ref, o_ref, z_ref):
    hsum = hs_ref[...]
    inv = 1.0 / RWKV_HEAD_DIM
    groups = [slice(gi * LANES, (gi + 1) * LANES) for gi in range(D_MODEL // LANES)]

    def head_sums(vals):
        parts = [_split(v) for v in vals]
        return [_dot(hi, hsum) + _dot(lo, hsum) for hi, lo in parts]

    y = [yf_ref[:, cols] + yb_ref[:, cols] for cols in groups]
    rkk = []
    for cols in groups:
        ia = ia0_ref[:, cols].astype(F32) + ia1_ref[:, cols].astype(F32)
        ksum = k_ref[:, cols].astype(F32) * (2.0 + (ia - 2.0) * ka_ref[:, cols])
        rkk.append(r_ref[:, cols].astype(F32) * ksum * rk_ref[:, cols])
    sums = head_sums(y + rkk)
    n_g = len(groups)
    yc = [y[gi] - sums[gi] * inv for gi in range(n_g)]
    var = head_sums([c * c for c in yc])
    for gi, cols in enumerate(groups):
        yn = yc[gi] * lax.rsqrt(var[gi] * inv + GN_EPS) * lg_ref[:, cols] + lb_ref[:, cols]
        bonus = sums[n_g + gi] * v_ref[:, cols].astype(F32)
        z_ref[:, cols] = ((yn + bonus) * g_ref[:, cols].astype(F32)).astype(BF16)
    h = _dot(z_ref[...], wo_ref[...])
    o_ref[...] = _layer_norm(ALPHA * x_ref[...] + h, ng_ref[...], nb_ref[...])


def _rwkv_out(yf, yb, r, k, v, g, ia0, ia1, x, lnx_g, lnx_b, r_k, k_a, wo, ng, nb, tm=256):
    t, d = x.shape
    row = pl.BlockSpec((tm, d), lambda i: (i, 0))
    vec = pl.BlockSpec((1, d), lambda i: (0, 0))
    return pl.pallas_call(
        _rwkv_out_kernel,
        grid=(t // tm,),
        in_specs=[row] * 9 + [vec] * 4 + [pl.BlockSpec((LANES, LANES), lambda i: (0, 0)),
                                          pl.BlockSpec((d, d), lambda i: (0, 0)), vec, vec],
        out_specs=row,
        out_shape=jax.ShapeDtypeStruct((t, d), F32),
        scratch_shapes=[pltpu.VMEM((tm, d), BF16)],
        compiler_params=_params("parallel"),
        name="rwkv_out",
    )(yf, yb, r, k, v, g, ia0, ia1, x, lnx_g, lnx_b, r_k, k_a, _head_sum_matrix(RWKV_HEAD_DIM), wo, ng, nb)


REC_E1, REC_E2, REC_R1, REC_R2, REC_G1, REC_G2 = range(6)


def _router_kernel(x_ref, w_ref, b_ref, tri_ref, rec_ref, cnt_ref, run_ref):
    @pl.when(pl.program_id(0) == 0)
    def _():
        run_ref[...] = jnp.zeros_like(run_ref)

    xh, xl = _split(x_ref[...])
    wh = w_ref[0]
    wl = w_ref[1]
    logits = _dot(xh, wh) + (_dot(xh, wl) + _dot(xl, wh)) + b_ref[...]
    lane = lax.broadcasted_iota(jnp.int32, logits.shape, 1)
    logits = jnp.where(lane < N_EXPERTS, logits, -jnp.inf)
    m1 = jnp.max(logits, axis=-1, keepdims=True)
    i1 = jnp.min(jnp.where(logits == m1, lane, LANES), axis=-1, keepdims=True)
    rest = jnp.where(lane == i1, -jnp.inf, logits)
    m2 = jnp.max(rest, axis=-1, keepdims=True)
    i2 = jnp.min(jnp.where(rest == m2, lane, LANES), axis=-1, keepdims=True)
    e2 = jnp.exp(m2 - m1)
    g1 = 1.0 / (1.0 + e2)
    g2 = e2 / (1.0 + e2)

    hot1 = lane == i1
    hot2 = lane == i2
    oh1 = hot1.astype(BF16)
    oh2 = hot2.astype(BF16)
    tri = tri_ref[...]
    run = run_ref[...]
    n1 = jnp.sum(oh1.astype(F32), axis=0, keepdims=True)
    n2 = jnp.sum(oh2.astype(F32), axis=0, keepdims=True)
    before1 = run + _dot(tri, oh1)
    before2 = run + n1 + _dot(tri, oh2)
    r1 = jnp.sum(jnp.where(hot1, before1, 0.0), axis=-1, keepdims=True)
    r2 = jnp.sum(jnp.where(hot2, before2, 0.0), axis=-1, keepdims=True)
    run = run + n1 + n2
    run_ref[...] = run
    cnt_ref[...] = run

    rec = jnp.zeros(logits.shape, F32)
    for idx, val in ((REC_E1, i1.astype(F32)), (REC_E2, i2.astype(F32)), (REC_R1, r1), (REC_R2, r2),
                     (REC_G1, g1), (REC_G2, g2)):
        rec = jnp.where(lane == idx, val, rec)
    rec_ref[...] = rec


def _router(x, w_hilo, bias, tm=512):
    t, d = x.shape
    tri = jnp.asarray(np.tril(np.ones((tm, tm), np.float32), -1), BF16)
    return pl.pallas_call(
        _router_kernel,
        grid=(t // tm,),
        in_specs=[pl.BlockSpec((tm, d), lambda i: (i, 0)),
                  pl.BlockSpec((2, d, LANES), lambda i: (0, 0, 0)),
                  pl.BlockSpec((1, LANES), lambda i: (0, 0)),
                  pl.BlockSpec((tm, tm), lambda i: (0, 0))],
        out_specs=[pl.BlockSpec((tm, LANES), lambda i: (i, 0)),
                   pl.BlockSpec((1, LANES), lambda i: (0, 0))],
        out_shape=[jax.ShapeDtypeStruct((t, LANES), F32), jax.ShapeDtypeStruct((1, LANES), F32)],
        scratch_shapes=[pltpu.VMEM((1, LANES), F32)],
        compiler_params=_params("arbitrary"),
        name="moe_router",
    )(x, w_hilo, bias, tri)


EXPERT_TILE = 1024
ROUTE_TILE = 512
TOKEN_ROWS = D_MODEL // LANES
DMA_UNROLL = 8


def _token_copy(src, src_tok, dst, dst_tok, sem):
    def tile(tok):
        return pl.ds(pl.multiple_of(tok * TOKEN_ROWS, TOKEN_ROWS), TOKEN_ROWS)
    return pltpu.make_async_copy(src.at[tile(src_tok)], dst.at[tile(dst_tok)], sem)


def _to_token_tiles(ref, n_tok, value):
    for j in range(TOKEN_ROWS):
        ref[pl.ds(j, n_tok, stride=TOKEN_ROWS), :] = value[:, j * LANES:(j + 1) * LANES]


def _from_token_tiles(ref, n_tok, j):
    return ref[pl.ds(j, n_tok, stride=TOKEN_ROWS), :]


def _dispatch_kernel(pos_ref, x_ref, xs_in, xs_hbm, xt_ref, sem):
    del xs_in
    _to_token_tiles(xt_ref, ROUTE_TILE, x_ref[...])

    def copies(t):
        return (_token_copy(xt_ref, t, xs_hbm, pos_ref[0, 0, t], sem),
                _token_copy(xt_ref, t, xs_hbm, pos_ref[0, 0, ROUTE_TILE + t], sem))

    def start(t, carry):
        for cp in copies(t):
            cp.start()
        return carry

    def wait(t, carry):
        for cp in copies(t):
            cp.wait()
        return carry

    lax.fori_loop(0, ROUTE_TILE, start, 0, unroll=DMA_UNROLL)
    lax.fori_loop(0, ROUTE_TILE, wait, 0, unroll=DMA_UNROLL)


def _moe_dispatch(x, pos, n_rows):
    t, d = x.shape
    return pl.pallas_call(
        _dispatch_kernel,
        grid=(t // ROUTE_TILE,),
        in_specs=[pl.BlockSpec((1, 1, 2 * ROUTE_TILE), lambda i: (i, 0, 0), memory_space=pltpu.SMEM),
                  pl.BlockSpec((ROUTE_TILE, d), lambda i: (i, 0)),
                  pl.BlockSpec(memory_space=pl.ANY)],
        out_specs=pl.BlockSpec(memory_space=pl.ANY),
        out_shape=jax.ShapeDtypeStruct((n_rows * TOKEN_ROWS, LANES), F32),
        scratch_shapes=[pltpu.VMEM((ROUTE_TILE * TOKEN_ROWS, LANES), F32), pltpu.SemaphoreType.DMA(())],
        input_output_aliases={2: 0},
        compiler_params=_params("arbitrary"),
        name="moe_dispatch",
    )(pos, x, jnp.zeros((n_rows * TOKEN_ROWS, LANES), F32))


EXPERT_HEAD_ROWS = EXPERT_TILE // 4


def _experts_kernel(te_ref, na_ref, nv_ref, x_ref, wg_ref, wu_ref, wd_ref, o_ref, xb_ref, acc_ref):
    i = pl.program_id(0)
    f = pl.program_id(1)
    active = i < na_ref[0]
    short = nv_ref[i] <= EXPERT_HEAD_ROWS

    @pl.when(active & (f == 0))
    def _():
        for j in range(TOKEN_ROWS):
            xb_ref[:, j * LANES:(j + 1) * LANES] = _from_token_tiles(x_ref, EXPERT_TILE, j).astype(BF16)
        acc_ref[...] = jnp.zeros_like(acc_ref)

    def swiglu(rows):
        xb = xb_ref[rows, :]
        gate = _dot(xb, wg_ref[...])
        up = _dot(xb, wu_ref[...])
        h = (gate * _sigmoid(gate) * up).astype(BF16)
        acc_ref[rows, :] += _dot(h, wd_ref[...])

    @pl.when(active & jnp.logical_not(short))
    def _():
        swiglu(slice(0, EXPERT_TILE))

    @pl.when(active & short)
    def _():
        swiglu(slice(0, EXPERT_HEAD_ROWS))

    @pl.when(f == pl.num_programs(1) - 1)
    def _():
        _to_token_tiles(o_ref, EXPERT_TILE, jnp.where(active, acc_ref[...], 0.0))


def _moe_experts(xs, tile_expert, n_active, tile_rows, wg, wu, wd, tf=512):
    d = D_MODEL
    n_tiles = xs.shape[0] // (EXPERT_TILE * TOKEN_ROWS)
    fe = wg.shape[2]
    row = pl.BlockSpec((EXPERT_TILE * TOKEN_ROWS, LANES), lambda i, f, te, na, nv: (i, 0))
    return pl.pallas_call(
        _experts_kernel,
        grid_spec=pltpu.PrefetchScalarGridSpec(
            num_scalar_prefetch=3,
            grid=(n_tiles, fe // tf),
            in_specs=[row,
                      pl.BlockSpec((None, d, tf), lambda i, f, te, na, nv: (te[i], 0, f)),
                      pl.BlockSpec((None, d, tf), lambda i, f, te, na, nv: (te[i], 0, f)),
                      pl.BlockSpec((None, tf, d), lambda i, f, te, na, nv: (te[i], f, 0))],
            out_specs=row,
            scratch_shapes=[pltpu.VMEM((EXPERT_TILE, d), BF16), pltpu.VMEM((EXPERT_TILE, d), F32)]),
        out_shape=jax.ShapeDtypeStruct(xs.shape, F32),
        compiler_params=_params("parallel", "arbitrary"),
        name="moe_experts",
    )(tile_expert, n_active, tile_rows, xs, wg, wu, wd)


def _combine_kernel(pos_ref, nxt_ref, ys_hbm, x_ref, rec_ref, g_ref, b_ref, o_ref, y_ref, sem):
    i = pl.program_id(0)
    n = pl.num_programs(0)
    slot = i % 2

    def copies(p_ref, s, t):
        return (_token_copy(ys_hbm, p_ref[0, 0, t], y_ref.at[s, 0], t, sem.at[s]),
                _token_copy(ys_hbm, p_ref[0, 0, ROUTE_TILE + t], y_ref.at[s, 1], t, sem.at[s]))

    def start_all(p_ref, s):
        def body(t, carry):
            for cp in copies(p_ref, s, t):
                cp.start()
            return carry
        lax.fori_loop(0, ROUTE_TILE, body, 0, unroll=DMA_UNROLL)

    @pl.when(i == 0)
    def _():
        start_all(pos_ref, slot)

    @pl.when(i + 1 < n)
    def _():
        start_all(nxt_ref, 1 - slot)

    def wait_body(t, carry):
        for cp in copies(pos_ref, slot, t):
            cp.wait()
        return carry
    lax.fori_loop(0, ROUTE_TILE, wait_body, 0, unroll=DMA_UNROLL)

    rec = rec_ref[...]
    lane = lax.broadcasted_iota(jnp.int32, rec.shape, 1)
    g1 = jnp.sum(jnp.where(lane == REC_G1, rec, 0.0), axis=-1, keepdims=True)
    g2 = jnp.sum(jnp.where(lane == REC_G2, rec, 0.0), axis=-1, keepdims=True)
    y1_ref = y_ref.at[slot, 0]
    y2_ref = y_ref.at[slot, 1]
    f = jnp.concatenate([g1 * _from_token_tiles(y1_ref, ROUTE_TILE, j) + g2 * _from_token_tiles(y2_ref, ROUTE_TILE, j)
                         for j in range(TOKEN_ROWS)], axis=1)
    o_ref[...] = _layer_norm(ALPHA * x_ref[...] + f, g_ref[...], b_ref[...])


def _moe_combine_ln(ys, pos, x, rec, g, b):
    t, d = x.shape
    n = t // ROUTE_TILE
    row = pl.BlockSpec((ROUTE_TILE, d), lambda i: (i, 0))
    vec = pl.BlockSpec((1, d), lambda i: (0, 0))
    return pl.pallas_call(
        _combine_kernel,
        grid=(n,),
        in_specs=[pl.BlockSpec((1, 1, 2 * ROUTE_TILE), lambda i: (i, 0, 0), memory_space=pltpu.SMEM),
                  pl.BlockSpec((1, 1, 2 * ROUTE_TILE), lambda i: (jnp.minimum(i + 1, n - 1), 0, 0),
                               memory_space=pltpu.SMEM),
                  pl.BlockSpec(memory_space=pl.ANY),
                  row,
                  pl.BlockSpec((ROUTE_TILE, LANES), lambda i: (i, 0)),
                  vec, vec],
        out_specs=row,
        out_shape=jax.ShapeDtypeStruct((t, d), F32),
        scratch_shapes=[pltpu.VMEM((2, 2, ROUTE_TILE * TOKEN_ROWS, LANES), F32), pltpu.SemaphoreType.DMA((2,))],
        compiler_params=_params("arbitrary"),
        name="moe_combine_ln",
    )(pos, pos, ys, x, rec, g, b)


def _moe_res_ln(x, rec, counts, wg, wu, wd, g, b):
    t, d = x.shape
    n_tiles = 2 * t // EXPERT_TILE + N_EXPERTS
    cnt = counts[0, :N_EXPERTS].astype(jnp.int32)
    padded = (cnt + EXPERT_TILE - 1) // EXPERT_TILE * EXPERT_TILE
    ends = jnp.cumsum(padded)
    starts = ends - padded
    e1 = rec[:, REC_E1].astype(jnp.int32)
    e2 = rec[:, REC_E2].astype(jnp.int32)
    pos1 = starts[e1] + rec[:, REC_R1].astype(jnp.int32)
    pos2 = starts[e2] + rec[:, REC_R2].astype(jnp.int32)
    pos = jnp.concatenate([pos1.reshape(-1, 1, ROUTE_TILE), pos2.reshape(-1, 1, ROUTE_TILE)], axis=2)
    tile_start = jnp.arange(n_tiles, dtype=jnp.int32) * EXPERT_TILE
    tile_expert = jnp.minimum(jnp.sum(tile_start[:, None] >= ends[None, :], axis=1), N_EXPERTS - 1).astype(jnp.int32)
    n_active = (ends[-1:] // EXPERT_TILE).astype(jnp.int32)
    tile_rows = jnp.clip(starts[tile_expert] + cnt[tile_expert] - tile_start, 0, EXPERT_TILE).astype(jnp.int32)
    last_expert = tile_expert[jnp.maximum(n_active[0] - 1, 0)]
    tile_expert = jnp.where(jnp.arange(n_tiles) < n_active[0], tile_expert, last_expert)

    xs = _moe_dispatch(x, pos, n_tiles * EXPERT_TILE)
    ys = _moe_experts(xs, tile_expert, n_active, tile_rows, wg, wu, wd)
    return _moe_combine_ln(ys, pos, x, rec, g, b)


def _prepare_weights(na_w_qkv, na_rpb, na_w_o, ffn_w_gate, ffn_w_up, ffn_w_down,
                     rwkv_mu, rwkv_w_rkv, rwkv_w0, rwkv_w1, rwkv_w2, rwkv_a0, rwkv_a1, rwkv_a2,
                     rwkv_g1, rwkv_g2, rwkv_k_k, rwkv_k_a, rwkv_r_k, rwkv_lnx_g, rwkv_lnx_b, rwkv_w_o,
                     moe_w_router, moe_b_router, moe_w_gate, moe_w_up, moe_w_down,
                     ln_mix_g, ln_mix_b, ln_ffn_g, ln_ffn_b):
    d = D_MODEL
    vec = lambda a: a.reshape(1, d).astype(F32)
    gl_pad = 2 * LANES - GATE_LORA
    router_w = jnp.pad(moe_w_router[0], ((0, 0), (0, LANES - N_EXPERTS)))
    router_hi = router_w.astype(BF16)
    router_lo = (router_w - router_hi.astype(F32)).astype(BF16)
    return dict(
        qkv_w=na_w_qkv[0].astype(BF16),
        qkv_scale=jnp.concatenate([jnp.full((1, d), NA_HEAD_DIM ** -0.5 * LOG2E, F32), jnp.ones((1, 2 * d), F32)], axis=1),
        na_bias=_na_bias_table(na_rpb[0]),
        na_wo=na_w_o[0].astype(BF16),
        ffn_wg=ffn_w_gate[0].astype(BF16), ffn_wu=ffn_w_up[0].astype(BF16), ffn_wd=ffn_w_down[0].astype(BF16),
        mu=rwkv_mu[0].astype(F32),
        wrkv=rwkv_w_rkv[0].astype(BF16),
        w1=jnp.concatenate([rwkv_w1[0, 0], rwkv_w1[0, 1]], axis=1).astype(BF16),
        w2=jnp.concatenate([rwkv_w2[0, 0], rwkv_w2[0, 1]], axis=0).astype(BF16),
        w0=rwkv_w0[0].astype(F32),
        a1=jnp.concatenate([rwkv_a1[0, 0], rwkv_a1[0, 1]], axis=1).astype(BF16),
        a2=jnp.concatenate([rwkv_a2[0, 0], rwkv_a2[0, 1]], axis=0).astype(BF16),
        a0=rwkv_a0[0].astype(F32),
        g1=jnp.pad(rwkv_g1[0], ((0, 0), (0, gl_pad))).astype(BF16),
        g2=jnp.pad(rwkv_g2[0], ((0, gl_pad), (0, 0))).astype(BF16),
        k_k=vec(rwkv_k_k[0]), k_a=vec(rwkv_k_a[0]), r_k=vec(rwkv_r_k[0]),
        lnx_g=vec(rwkv_lnx_g[0]), lnx_b=vec(rwkv_lnx_b[0]),
        rwkv_wo=rwkv_w_o[0].astype(BF16),
        router_w=jnp.stack([router_hi, router_lo]),
        router_b=jnp.pad(moe_b_router[0], (0, LANES - N_EXPERTS)).reshape(1, LANES).astype(F32),
        moe_wg=moe_w_gate[0].astype(BF16), moe_wu=moe_w_up[0].astype(BF16), moe_wd=moe_w_down[0].astype(BF16),
        ln_mix_g=ln_mix_g.astype(F32), ln_mix_b=ln_mix_b.astype(F32),
        ln_ffn_g=ln_ffn_g.astype(F32), ln_ffn_b=ln_ffn_b.astype(F32),
    )


def _trunk(x3, w):
    batch, seq_len, d = x3.shape
    rows = seq_len // GRID_W
    x = x3.reshape(batch * seq_len, d)
    ln = lambda a, i: a[i].reshape(1, d)

    qkv = _qkv_proj(x, w["qkv_w"], w["qkv_scale"])
    att = _na_attention(qkv, w["na_bias"], batch, rows)
    x = _proj_res_ln(att, w["na_wo"], x, ln(w["ln_mix_g"], 0), ln(w["ln_mix_b"], 0))
    x = _ffn_res_ln(x, w["ffn_wg"], w["ffn_wu"], w["ffn_wd"], ln(w["ln_ffn_g"], 0), ln(w["ln_ffn_b"], 0))

    r, k, kk, v, g, lw0, lw1, ia0, ia1 = _rwkv_proj(x, seq_len, w["mu"], w["wrkv"], w["w1"], w["w2"], w["w0"],
                                                    w["a1"], w["a2"], w["a0"], w["g1"], w["g2"], w["k_k"])
    yf, yb = _rwkv_scan(r, k, kk, v, lw0, lw1, ia0, ia1, w["k_a"], batch, seq_len)
    x = _rwkv_out(yf, yb, r, k, v, g, ia0, ia1, x, w["lnx_g"], w["lnx_b"], w["r_k"], w["k_a"], w["rwkv_wo"],
                  ln(w["ln_mix_g"], 1), ln(w["ln_mix_b"], 1))
    rec, counts = _router(x, w["router_w"], w["router_b"])
    x = _moe_res_ln(x, rec, counts, w["moe_wg"], w["moe_wu"], w["moe_wd"], ln(w["ln_ffn_g"], 1), ln(w["ln_ffn_b"], 1))
    return x.reshape(batch, seq_len, d)


def kernel(x_prompt, x_sample, na_w_qkv, na_rpb, na_w_o, ffn_w_gate, ffn_w_up, ffn_w_down, rwkv_mu, rwkv_w_rkv, rwkv_w0, rwkv_w1, rwkv_w2, rwkv_a0, rwkv_a1, rwkv_a2, rwkv_g1, rwkv_g2, rwkv_k_k, rwkv_k_a, rwkv_r_k, rwkv_lnx_g, rwkv_lnx_b, rwkv_w_o, moe_w_router, moe_b_router, moe_w_gate, moe_w_up, moe_w_down, ln_mix_g, ln_mix_b, ln_ffn_g, ln_ffn_b):
    w = _prepare_weights(na_w_qkv, na_rpb, na_w_o, ffn_w_gate, ffn_w_up, ffn_w_down,
                         rwkv_mu, rwkv_w_rkv, rwkv_w0, rwkv_w1, rwkv_w2, rwkv_a0, rwkv_a1, rwkv_a2,
                         rwkv_g1, rwkv_g2, rwkv_k_k, rwkv_k_a, rwkv_r_k, rwkv_lnx_g, rwkv_lnx_b, rwkv_w_o,
                         moe_w_router, moe_b_router, moe_w_gate, moe_w_up, moe_w_down,
                         ln_mix_g, ln_mix_b, ln_ffn_g, ln_ffn_b)
    return (_trunk(x_prompt, w), _trunk(x_sample, w))
```

```python
import functools
import math

import numpy as np
import jax
import jax.numpy as jnp
from jax import lax
from jax.experimental import pallas as pl
from jax.experimental.pallas import tpu as pltpu

D_MODEL = 1024
DEPTH = 2
GRID_W = 64
NA_HEAD_DIM = 32
NA_HEADS = D_MODEL // NA_HEAD_DIM
NA_ROWS = 8
NA_COLS = 16
RWKV_HEAD_DIM = 64
DECAY_LORA = 64
GATE_LORA = 160
GN_EPS = 64e-5
N_EXPERTS = 8
LN_EPS = 1e-5
ALPHA = (2 * DEPTH) ** 0.25
LOG2E = math.log2(math.e)

LANES = 128
SCAN_CHUNK = 64
SCAN_BLOCK = 256
VMEM_LIMIT = 56 * 1024 * 1024

F32 = jnp.float32
BF16 = jnp.bfloat16


def _params(*sem):
    return pltpu.CompilerParams(dimension_semantics=sem, vmem_limit_bytes=VMEM_LIMIT)


def _dot(a, b):
    return jnp.dot(a, b, preferred_element_type=F32)


def _dot_nt(a, b):
    return lax.dot_general(a, b, (((1,), (1,)), ((), ())), preferred_element_type=F32)


def _dot_tn(a, b):
    return lax.dot_general(a, b, (((0,), (0,)), ((), ())), preferred_element_type=F32)


def _split(a):
    hi = a.astype(BF16)
    lo = (a - hi.astype(F32)).astype(BF16)
    return hi, lo


def _layer_norm(z, g, b):
    mu = jnp.mean(z, axis=-1, keepdims=True)
    zc = z - mu
    var = jnp.mean(zc * zc, axis=-1, keepdims=True)
    return zc * lax.rsqrt(var + LN_EPS) * g + b


def _sigmoid(z):
    return 1.0 / (1.0 + jnp.exp(-z))


def _head_sum_matrix(head_dim):
    lane = np.arange(LANES)
    return jnp.asarray((lane[:, None] // head_dim) == (lane[None, :] // head_dim), BF16)


def _qkv_kernel(x_ref, w_ref, s_ref, o_ref):
    acc = _dot(x_ref[...].astype(BF16), w_ref[...])
    o_ref[...] = (acc * s_ref[...]).astype(o_ref.dtype)


def _qkv_proj(x, w, colscale, tm=512):
    t, d = x.shape
    n = w.shape[1]
    return pl.pallas_call(
        _qkv_kernel,
        grid=(t // tm,),
        in_specs=[pl.BlockSpec((tm, d), lambda i: (i, 0)),
                  pl.BlockSpec((d, n), lambda i: (0, 0)),
                  pl.BlockSpec((1, n), lambda i: (0, 0))],
        out_specs=pl.BlockSpec((tm, n), lambda i: (i, 0)),
        out_shape=jax.ShapeDtypeStruct((t, n), BF16),
        compiler_params=_params("parallel"),
        name="qkv_proj",
    )(x, w, colscale)


NA_HEADS_PER_GROUP = LANES // NA_HEAD_DIM
NA_GROUPS = D_MODEL // LANES
NA_WIN = NA_ROWS * GRID_W


NA_GROUP_ROWS = NA_HEADS_PER_GROUP * GRID_W
NA_ROW_PAIRS = NA_ROWS // 2
NA_BIAS_PAIRS = 2 * NA_ROWS - 2


NA_STEP_ROWS = 4
NA_FETCH_ROWS = NA_ROWS + NA_STEP_ROWS - 1


def _na_win_start(r, rows):
    return jnp.clip(r - NA_ROWS // 2, 0, rows - NA_ROWS)


def _na_fetch_start(r0, rows):
    return jnp.minimum(_na_win_start(r0, rows), rows - NA_FETCH_ROWS)


def _na_kernel(rows, q_ref, k_ref, v_ref, b_ref, o_ref):
    r0 = pl.program_id(1) * NA_STEP_ROWS
    fetch = _na_fetch_start(r0, rows)
    lane_head = lax.broadcasted_iota(jnp.int32, (GRID_W, LANES), 1) // NA_HEAD_DIM
    groups = [slice(g * LANES, (g + 1) * LANES) for g in range(NA_GROUPS)]
    units = []
    for ri in range(NA_STEP_ROWS):
        win = _na_win_start(r0 + ri, rows)
        keys = pl.ds(pl.multiple_of((win - fetch) * GRID_W, GRID_W), NA_WIN)
        for g in range(NA_GROUPS):
            units.append((slice(ri * GRID_W, (ri + 1) * GRID_W), keys, r0 + ri - win, g))

    scores = []
    for qrows, keys, _, g in units:
        qg = q_ref[qrows, groups[g]]
        zero = jnp.zeros_like(qg)
        qm = jnp.concatenate([jnp.where(lane_head == h, qg, zero) for h in range(NA_HEADS_PER_GROUP)], axis=0)
        scores.append(_dot_nt(qm, k_ref[keys, groups[g]]))

    probs, denoms = [], []
    for (_, _, off, g), s in zip(units, scores):
        hq = pl.ds(g * NA_GROUP_ROWS, NA_GROUP_ROWS)
        s = jnp.concatenate([s[:, m * LANES:(m + 1) * LANES] + b_ref[2 * m - off + NA_ROWS - 1, hq, :]
                             for m in range(NA_ROW_PAIRS)], axis=1)
        p = jnp.exp2(s - jnp.max(s, axis=-1, keepdims=True))
        denoms.append(jnp.sum(p, axis=-1, keepdims=True))
        probs.append(p.astype(BF16))

    for (qrows, keys, _, g), p, l in zip(units, probs, denoms):
        o4 = _dot(p, v_ref[keys, groups[g]]) / l
        og = jnp.zeros((GRID_W, LANES), F32)
        for h in range(NA_HEADS_PER_GROUP):
            og = og + jnp.where(lane_head == h, o4[h * GRID_W:(h + 1) * GRID_W], 0.0)
        o_ref[qrows, groups[g]] = og.astype(o_ref.dtype)


def _na_attention(qkv, bias, batch, rows):
    t = qkv.shape[0]
    steps = rows // NA_STEP_ROWS
    q_rows = NA_STEP_ROWS * GRID_W

    def q_map(b, s):
        return (b * steps + s, 0)

    def k_map(b, s):
        return ((b * rows + _na_fetch_start(s * NA_STEP_ROWS, rows)) * GRID_W, D_MODEL)

    def v_map(b, s):
        return ((b * rows + _na_fetch_start(s * NA_STEP_ROWS, rows)) * GRID_W, 2 * D_MODEL)

    window = (pl.Element(NA_FETCH_ROWS * GRID_W), pl.Element(D_MODEL))
    return pl.pallas_call(
        functools.partial(_na_kernel, rows),
        grid=(batch, steps),
        in_specs=[pl.BlockSpec((q_rows, D_MODEL), q_map),
                  pl.BlockSpec(window, k_map),
                  pl.BlockSpec(window, v_map),
                  pl.BlockSpec(bias.shape, lambda b, s: (0, 0, 0), pipeline_mode=pl.Buffered(1))],
        out_specs=pl.BlockSpec((q_rows, D_MODEL), q_map),
        out_shape=jax.ShapeDtypeStruct((t, D_MODEL), BF16),
        compiler_params=_params("parallel", "arbitrary"),
        name="na_attention",
    )(qkv, qkv, qkv, bias)


def _na_bias_kernel(rpb_ref, oh_ref, mask_ref, o_ref):
    x = rpb_ref[...]
    hi = x.astype(BF16)
    r1 = x - hi.astype(F32)
    mid = r1.astype(BF16)
    lo = (r1 - mid.astype(F32)).astype(BF16)
    oh = oh_ref[...]
    o_ref[...] = (_dot(hi, oh) + _dot(mid, oh) + _dot(lo, oh)) * LOG2E + mask_ref[...]


def _na_bias_table(rpb, tn=1024):
    n_dr, n_dc = 2 * NA_ROWS - 1, 2 * NA_COLS - 1
    c = np.arange(GRID_W)
    q_start = np.clip(c - NA_COLS // 2, 0, GRID_W - NA_COLS)
    kc = np.arange(GRID_W)
    valid = (kc[None, :] >= q_start[:, None]) & (kc[None, :] < q_start[:, None] + NA_COLS)
    dc = kc[None, :] - c[:, None] + NA_COLS - 1
    onehot = (np.arange(LANES)[:, None, None] == dc[None]) & valid[None]
    onehot = jnp.asarray(onehot.reshape(LANES, GRID_W * GRID_W), BF16)
    mask = jnp.asarray(np.where(valid, 0.0, -np.inf).reshape(1, GRID_W * GRID_W), F32)
    rpb2 = jnp.pad(rpb.reshape(NA_HEADS * n_dr, n_dc).astype(F32), ((0, 0), (0, LANES - n_dc)))
    nrow, ncol = rpb2.shape[0], GRID_W * GRID_W
    flat = pl.pallas_call(
        _na_bias_kernel,
        grid=(ncol // tn,),
        in_specs=[pl.BlockSpec((nrow, LANES), lambda j: (0, 0)),
                  pl.BlockSpec((LANES, tn), lambda j: (0, j)),
                  pl.BlockSpec((1, tn), lambda j: (0, j))],
        out_specs=pl.BlockSpec((nrow, tn), lambda j: (0, j)),
        out_shape=jax.ShapeDtypeStruct((nrow, ncol), F32),
        compiler_params=_params("parallel"),
        name="na_bias_table",
    )(rpb2, onehot, mask)
    toe = flat.reshape(NA_HEADS, n_dr, GRID_W, GRID_W)
    pairs = jnp.stack([toe[:, 0:NA_BIAS_PAIRS], toe[:, 1:NA_BIAS_PAIRS + 1]], axis=3)
    return jnp.transpose(pairs, (1, 0, 2, 3, 4)).reshape(NA_BIAS_PAIRS, NA_HEADS * GRID_W, LANES)


def _proj_ln_kernel(a_ref, w_ref, x_ref, g_ref, b_ref, o_ref):
    h = _dot(a_ref[...], w_ref[...])
    o_ref[...] = _layer_norm(ALPHA * x_ref[...] + h, g_ref[...], b_ref[...])


def _proj_res_ln(a, w, x, g, b, tm=512):
    t, d = x.shape
    row = pl.BlockSpec((tm, d), lambda i: (i, 0))
    vec = pl.BlockSpec((1, d), lambda i: (0, 0))
    return pl.pallas_call(
        _proj_ln_kernel,
        grid=(t // tm,),
        in_specs=[row, pl.BlockSpec((d, d), lambda i: (0, 0)), row, vec, vec],
        out_specs=row,
        out_shape=jax.ShapeDtypeStruct((t, d), F32),
        compiler_params=_params("parallel"),
        name="proj_res_ln",
    )(a, w, x, g, b)


def _ffn_kernel(x_ref, wg_ref, wu_ref, wd_ref, g_ref, b_ref, o_ref, xb_ref, acc_ref):
    f = pl.program_id(1)

    @pl.when(f == 0)
    def _():
        xb_ref[...] = x_ref[...].astype(BF16)
        acc_ref[...] = jnp.zeros_like(acc_ref)

    xb = xb_ref[...]
    gate = _dot(xb, wg_ref[...])
    up = _dot(xb, wu_ref[...])
    h = (gate * _sigmoid(gate) * up).astype(BF16)
    acc_ref[...] += _dot(h, wd_ref[...])

    @pl.when(f == pl.num_programs(1) - 1)
    def _():
        o_ref[...] = _layer_norm(ALPHA * x_ref[...] + acc_ref[...], g_ref[...], b_ref[...])


def _ffn_res_ln(x, wg, wu, wd, g, b, tm=512, tf=1408):
    t, d = x.shape
    ff = wg.shape[1]
    row = pl.BlockSpec((tm, d), lambda i, f: (i, 0))
    vec = pl.BlockSpec((1, d), lambda i, f: (0, 0))
    return pl.pallas_call(
        _ffn_kernel,
        grid=(t // tm, ff // tf),
        in_specs=[row,
                  pl.BlockSpec((d, tf), lambda i, f: (0, f)),
                  pl.BlockSpec((d, tf), lambda i, f: (0, f)),
                  pl.BlockSpec((tf, d), lambda i, f: (f, 0)),
                  vec, vec],
        out_specs=row,
        out_shape=jax.ShapeDtypeStruct((t, d), F32),
        scratch_shapes=[pltpu.VMEM((tm, d), BF16), pltpu.VMEM((tm, d), F32)],
        compiler_params=_params("parallel", "arbitrary"),
        name="ffn_res_ln",
    )(x, wg, wu, wd, g, b)


HALO = 8


def _rwkv_proj_kernel(seq_len, x_ref, xp_ref, xn_ref, mu_ref, wrkv_ref, w1_ref, w2_ref, w0_ref,
                      a1_ref, a2_ref, a0_ref, g1_ref, g2_ref, kscale_ref, hs_ref,
                      r_ref, k_ref, kk_ref, v_ref, g_ref, lw0_ref, lw1_ref, ia0_ref, ia1_ref):
    tm = x_ref.shape[0]
    i = pl.program_id(0)
    x = x_ref[...]
    row = lax.broadcasted_iota(jnp.int32, x.shape, 0)
    first_in_seq = (i * tm) % seq_len == 0
    last_in_seq = ((i + 1) * tm) % seq_len == 0
    prev_row = jnp.where(first_in_seq, 0.0, xp_ref[HALO - 1:HALO, :])
    next_row = jnp.where(last_in_seq, 0.0, xn_ref[0:1, :])
    x_prev = jnp.where(row == 0, prev_row, pltpu.roll(x, 1, axis=0))
    x_next = jnp.where(row == tm - 1, next_row, pltpu.roll(x, tm - 1, axis=0))
    xx = 0.5 * (x_prev + x_next) - x

    mixes = [(x + xx * mu_ref[j:j + 1, :]).astype(BF16) for j in range(6)]
    rf = _dot(mixes[0], wrkv_ref[0])
    kf = _dot(mixes[2], wrkv_ref[1])
    vf = _dot(mixes[3], wrkv_ref[2])
    lo_pre = _dot(mixes[1], w1_ref[...])
    al = _dot(mixes[4], a1_ref[...])
    gl_pre = _dot(mixes[5], g1_ref[...])

    r_ref[...] = rf.astype(r_ref.dtype)
    k_ref[...] = kf.astype(k_ref.dtype)
    v_ref[...] = vf.astype(v_ref.dtype)
    lane = lax.broadcasted_iota(jnp.int32, (tm, LANES), 1)
    dir0 = lane < DECAY_LORA
    lo = jnp.tanh(lo_pre)
    lz = [jnp.where(dir0 if z == 0 else ~dir0, lo, 0.0).astype(BF16) for z in range(2)]
    az = [jnp.where(dir0 if z == 0 else ~dir0, al, 0.0).astype(BF16) for z in range(2)]
    gl = _sigmoid(gl_pre).astype(BF16)
    groups = [slice(gi * LANES, (gi + 1) * LANES) for gi in range(D_MODEL // LANES)]
    kkr = [kf[:, cols] * kscale_ref[:, cols] for cols in groups]
    sq = [_split(q * q) for q in kkr]

    wl = [_dot(lz[z], w2_ref[...]) for z in range(2)]
    ai = [_dot(az[z], a2_ref[...]) for z in range(2)]
    gf = _dot(gl, g2_ref[...])
    hsum = hs_ref[...]
    norm2 = [_dot(hi, hsum) + _dot(lo_, hsum) for hi, lo_ in sq]

    for z, out in enumerate((lw0_ref, lw1_ref)):
        out[...] = (-math.exp(-0.5)) * _sigmoid(w0_ref[z:z + 1, :] + wl[z])
    for z, out in enumerate((ia0_ref, ia1_ref)):
        out[...] = _sigmoid(a0_ref[z:z + 1, :] + ai[z]).astype(out.dtype)
    g_ref[...] = gf.astype(g_ref.dtype)
    for cols, q, n2 in zip(groups, kkr, norm2):
        kk_ref[:, cols] = (q / jnp.maximum(jnp.sqrt(n2), 1e-12)).astype(kk_ref.dtype)


def _rwkv_proj(x, seq_len, mu, wrkv, w1, w2, w0, a1, a2, a0, g1, g2, kscale, tm=256):
    t, d = x.shape
    nb = t // HALO
    step = tm // HALO
    row = pl.BlockSpec((tm, d), lambda i: (i, 0))
    full = lambda shape: pl.BlockSpec(shape, lambda i: (0,) * len(shape))
    hsum = _head_sum_matrix(RWKV_HEAD_DIM)
    outs = [jax.ShapeDtypeStruct((t, d), BF16)] * 5 + [jax.ShapeDtypeStruct((t, d), F32)] * 2 + \
           [jax.ShapeDtypeStruct((t, d), BF16)] * 2
    return pl.pallas_call(
        functools.partial(_rwkv_proj_kernel, seq_len),
        grid=(t // tm,),
        in_specs=[row,
                  pl.BlockSpec((HALO, d), lambda i: (jnp.maximum(i * step - 1, 0), 0)),
                  pl.BlockSpec((HALO, d), lambda i: (jnp.minimum((i + 1) * step, nb - 1), 0)),
                  full(mu.shape), full(wrkv.shape), full(w1.shape), full(w2.shape), full(w0.shape),
                  full(a1.shape), full(a2.shape), full(a0.shape), full(g1.shape), full(g2.shape),
                  full(kscale.shape), full(hsum.shape)],
        out_specs=[row] * 9,
        out_shape=outs,
        compiler_params=_params("parallel"),
        name="rwkv_proj",
    )(x, x, x, mu, wrkv, w1, w2, w0, a1, a2, a0, g1, g2, kscale, hsum)


def _pair_blockdiag(y, lane_a):
    zero = jnp.zeros_like(y)
    return jnp.concatenate([jnp.where(lane_a, y, zero), jnp.where(lane_a, zero, y)], axis=0)


def _pair_matmul(x, y, lane_a):
    return _dot(x.astype(BF16), _pair_blockdiag(y.astype(BF16), lane_a))


def _scan_prepare(chains):
    c = SCAN_CHUNK
    t_idx = lax.broadcasted_iota(jnp.int32, (c, LANES), 0)
    s_idx = lax.broadcasted_iota(jnp.int32, (c, LANES), 1) % c
    lane_a = lax.broadcasted_iota(jnp.int32, (c, LANES), 1) < RWKV_HEAD_DIM
    eye = (s_idx == t_idx).astype(F32)
    tri_r = lax.broadcasted_iota(jnp.int32, (c, c), 0)
    tri_c = lax.broadcasted_iota(jnp.int32, (c, c), 1)
    bd = (lax.broadcasted_iota(jnp.int32, (LANES, LANES), 0) // RWKV_HEAD_DIM ==
          lax.broadcasted_iota(jnp.int32, (LANES, LANES), 1) // RWKV_HEAD_DIM)
    before = {rev: ((s_idx > t_idx) if rev else (s_idx < t_idx)) for rev in (False, True)}
    upto = {rev: before[rev] | (s_idx == t_idx) for rev in (False, True)}
    cum = {rev: ((tri_c >= tri_r) if rev else (tri_c <= tri_r)).astype(BF16) for rev in (False, True)}
    n = len(chains)

    lw_split = [jnp.concatenate(_split(ch[6]), axis=1) for ch in chains]
    lwc2 = [_dot(cum[chains[i][0]], lw_split[i]) for i in range(n)]
    lwc = [x[:, 0:LANES] + x[:, LANES:2 * LANES] for x in lwc2]
    vals = [ch[5] for ch in chains]

    at, rt, bh, kh, w_tot, lhs, rhs = [], [], [], [], [], [], []
    for (rev, ka, r, k, kk, v, lw, ia), lc in zip(chains, lwc):
        last = lc[0:1, :] if rev else lc[c - 1:c, :]
        w_in = jnp.exp(lc)
        w_ex = jnp.exp(lc - lw)
        w_inv = jnp.exp(-lc)
        w_end = jnp.exp(last)
        w_out = w_end * w_inv
        kf = k.astype(F32)
        kkf = kk.astype(F32)
        iaf = ia.astype(F32)
        kd = kf * (1.0 + (iaf - 1.0) * ka)
        bvec = kkf * iaf
        a_t = (-kkf) * w_ex
        r_t = r.astype(F32) * w_in
        at.append(a_t)
        rt.append(r_t)
        bh.append((bvec * w_out).astype(BF16))
        kh.append((kd * w_out).astype(BF16))
        w_tot.append(w_end)
        lhs.append(jnp.concatenate([a_t, r_t], axis=0).astype(BF16))
        rhs.append(jnp.concatenate([_pair_blockdiag((bvec * w_inv).astype(BF16), lane_a),
                                    _pair_blockdiag((kd * w_inv).astype(BF16), lane_a)], axis=0))

    gram = [_dot_nt(lhs[i], rhs[i]) for i in range(n)]
    a_ab = [jnp.where(before[chains[i][0]], gram[i][0:c, 0:LANES], 0.0) for i in range(n)]
    a_ak = [jnp.where(before[chains[i][0]], gram[i][0:c, LANES:2 * LANES], 0.0) for i in range(n)]
    a_rb = [jnp.where(upto[chains[i][0]], gram[i][c:2 * c, 0:LANES], 0.0) for i in range(n)]
    a_rk = [jnp.where(upto[chains[i][0]], gram[i][c:2 * c, LANES:2 * LANES], 0.0) for i in range(n)]

    tinv = [eye + a for a in a_ab]
    pw = [_pair_matmul(a, a, lane_a) for a in a_ab]
    n_sq = int(math.log2(c)) - 1
    for j in range(n_sq):
        final = j == n_sq - 1
        lhs_j = [(tinv[i] if final else jnp.concatenate([tinv[i], pw[i]], axis=0)).astype(BF16) for i in range(n)]
        res = [_dot(lhs_j[i], _pair_blockdiag(pw[i].astype(BF16), lane_a)) for i in range(n)]
        tinv = [tinv[i] + res[i][0:c] for i in range(n)]
        if not final:
            pw = [res[i][c:2 * c] for i in range(n)]

    av = [_pair_matmul(a_ak[i], vals[i], lane_a) for i in range(n)]
    y_loc = [_pair_matmul(a_rk[i], vals[i], lane_a) for i in range(n)]
    pq = [_dot(tinv[i].astype(BF16),
               jnp.concatenate([_pair_blockdiag(at[i].astype(BF16), lane_a),
                                _pair_blockdiag(av[i].astype(BF16), lane_a)], axis=1)) for i in range(n)]
    pqb = [x.astype(BF16) for x in pq]
    mn = [_dot_tn(pqb[i], bh[i]) for i in range(n)]
    kv = [_dot_tn(vals[i], kh[i]) for i in range(n)]
    arb = [_dot(a_rb[i].astype(BF16),
                jnp.concatenate([_pair_blockdiag(pqb[i][:, 0:LANES], lane_a),
                                 _pair_blockdiag(pqb[i][:, LANES:2 * LANES], lane_a)], axis=1)) for i in range(n)]
    out = []
    for i in range(n):
        ry = (rt[i] + arb[i][:, 0:LANES]).astype(BF16)
        yc = arb[i][:, LANES:2 * LANES] + y_loc[i]
        m_state = jnp.where(bd, mn[i][0:LANES], 0.0).astype(BF16)
        n_state = jnp.where(bd, mn[i][LANES:2 * LANES] + kv[i], 0.0)
        out.append((ry, yc, m_state, n_state, w_tot[i]))
    return out


def _scan_kernel(n_pairs, rf_ref, kf_ref, kkf_ref, vf_ref, lwf_ref, iaf_ref,
                 rb_ref, kb_ref, kkb_ref, vb_ref, lwb_ref, iab_ref, ka_ref, yf_ref, yb_ref, s_ref):
    @pl.when(pl.program_id(2) == 0)
    def _():
        s_ref[...] = jnp.zeros_like(s_ref)

    c = SCAN_CHUNK
    n_chunks = SCAN_BLOCK // c
    dirs = ((False, (rf_ref, kf_ref, kkf_ref, vf_ref, lwf_ref, iaf_ref), yf_ref),
            (True, (rb_ref, kb_ref, kkb_ref, vb_ref, lwb_ref, iab_ref), yb_ref))
    chains, where = [], []
    for step in range(n_chunks):
        for p in range(n_pairs):
            cols = slice(p * LANES, (p + 1) * LANES)
            for di, (rev, refs, _) in enumerate(dirs):
                ci = n_chunks - 1 - step if rev else step
                rows = slice(ci * c, (ci + 1) * c)
                chains.append((rev, ka_ref[:, cols]) + tuple(ref[rows, cols] for ref in refs))
                where.append((2 * p + di, di, rows, cols))
    pre = _scan_prepare(chains)
    state = [s_ref[j] for j in range(2 * n_pairs)]
    for (j, di, rows, cols), (ry, yc, m_state, n_state, w_total) in zip(where, pre):
        s = state[j]
        sb = s.astype(BF16)
        dirs[di][2][rows, cols] = _dot_nt(ry, sb) + yc
        state[j] = s * w_total + _dot(sb, m_state) + n_state
    for j in range(2 * n_pairs):
        s_ref[j] = state[j]


def _rwkv_scan(r, k, kk, v, lw0, lw1, ia0, ia1, ka, batch, seq_len, n_pairs=8):
    t, d = r.shape
    ns = seq_len // SCAN_BLOCK
    width = n_pairs * LANES
    fwd = pl.BlockSpec((SCAN_BLOCK, width), lambda b, h, s: (b * ns + s, h))
    bwd = pl.BlockSpec((SCAN_BLOCK, width), lambda b, h, s: (b * ns + ns - 1 - s, h))
    vec = pl.BlockSpec((1, width), lambda b, h, s: (0, h))
    return pl.pallas_call(
        functools.partial(_scan_kernel, n_pairs),
        grid=(batch, d // width, ns),
        in_specs=[fwd] * 6 + [bwd] * 6 + [vec],
        out_specs=[fwd, bwd],
        out_shape=[jax.ShapeDtypeStruct((t, d), F32)] * 2,
        scratch_shapes=[pltpu.VMEM((2 * n_pairs, LANES, LANES), F32)],
        compiler_params=_params("parallel", "parallel", "arbitrary"),
        name="rwkv_scan",
    )(r, k, kk, v, lw0, ia0, r, k, kk, v, lw1, ia1, ka)


def _rwkv_out_kernel(yf_ref, yb_ref, r_ref, k_ref, v_ref, g_ref, ia0_ref, ia1_ref, x_ref,
                     lg_ref, lb_ref, rk_ref, ka_ref, hs_ref, wo_ref, ng_ref, nb_ref, o_ref, z_ref):
    hsum = hs_ref[...]
    inv = 1.0 / RWKV_HEAD_DIM
    groups = [slice(gi * LANES, (gi + 1) * LANES) for gi in range(D_MODEL // LANES)]

    def head_sums(vals):
        parts = [_split(v) for v in vals]
        return [_dot(hi, hsum) + _dot(lo, hsum) for hi, lo in parts]

    y = [yf_ref[:, cols] + yb_ref[:, cols] for cols in groups]
    rkk = []
    for cols in groups:
        ia = ia0_ref[:, cols].astype(F32) + ia1_ref[:, cols].astype(F32)
        ksum = k_ref[:, cols].astype(F32) * (2.0 + (ia - 2.0) * ka_ref[:, cols])
        rkk.append(r_ref[:, cols].astype(F32) * ksum * rk_ref[:, cols])
    sums = head_sums(y + rkk)
    n_g = len(groups)
    yc = [y[gi] - sums[gi] * inv for gi in range(n_g)]
    var = head_sums([c * c for c in yc])
    for gi, cols in enumerate(groups):
        yn = yc[gi] * lax.rsqrt(var[gi] * inv + GN_EPS) * lg_ref[:, cols] + lb_ref[:, cols]
        bonus = sums[n_g + gi] * v_ref[:, cols].astype(F32)
        z_ref[:, cols] = ((yn + bonus) * g_ref[:, cols].astype(F32)).astype(BF16)
    h = _dot(z_ref[...], wo_ref[...])
    o_ref[...] = _layer_norm(ALPHA * x_ref[...] + h, ng_ref[...], nb_ref[...])


def _rwkv_out(yf, yb, r, k, v, g, ia0, ia1, x, lnx_g, lnx_b, r_k, k_a, wo, ng, nb, tm=256):
    t, d = x.shape
    row = pl.BlockSpec((tm, d), lambda i: (i, 0))
    vec = pl.BlockSpec((1, d), lambda i: (0, 0))
    return pl.pallas_call(
        _rwkv_out_kernel,
        grid=(t // tm,),
        in_specs=[row] * 9 + [vec] * 4 + [pl.BlockSpec((LANES, LANES), lambda i: (0, 0)),
                                          pl.BlockSpec((d, d), lambda i: (0, 0)), vec, vec],
        out_specs=row,
        out_shape=jax.ShapeDtypeStruct((t, d), F32),
        scratch_shapes=[pltpu.VMEM((tm, d), BF16)],
        compiler_params=_params("parallel"),
        name="rwkv_out",
    )(yf, yb, r, k, v, g, ia0, ia1, x, lnx_g, lnx_b, r_k, k_a, _head_sum_matrix(RWKV_HEAD_DIM), wo, ng, nb)


REC_E1, REC_E2, REC_R1, REC_R2, REC_G1, REC_G2 = range(6)


def _router_kernel(x_ref, w_ref, b_ref, tri_ref, rec_ref, cnt_ref, run_ref):
    @pl.when(pl.program_id(0) == 0)
    def _():
        run_ref[...] = jnp.zeros_like(run_ref)

    xh, xl = _split(x_ref[...])
    wh = w_ref[0]
    wl = w_ref[1]
    logits = _dot(xh, wh) + (_dot(xh, wl) + _dot(xl, wh)) + b_ref[...]
    lane = lax.broadcasted_iota(jnp.int32, logits.shape, 1)
    logits = jnp.where(lane < N_EXPERTS, logits, -jnp.inf)
    m1 = jnp.max(logits, axis=-1, keepdims=True)
    i1 = jnp.min(jnp.where(logits == m1, lane, LANES), axis=-1, keepdims=True)
    rest = jnp.where(lane == i1, -jnp.inf, logits)
    m2 = jnp.max(rest, axis=-1, keepdims=True)
    i2 = jnp.min(jnp.where(rest == m2, lane, LANES), axis=-1, keepdims=True)
    e2 = jnp.exp(m2 - m1)
    g1 = 1.0 / (1.0 + e2)
    g2 = e2 / (1.0 + e2)

    hot1 = lane == i1
    hot2 = lane == i2
    oh1 = hot1.astype(BF16)
    oh2 = hot2.astype(BF16)
    tri = tri_ref[...]
    run = run_ref[...]
    n1 = jnp.sum(oh1.astype(F32), axis=0, keepdims=True)
    n2 = jnp.sum(oh2.astype(F32), axis=0, keepdims=True)
    before1 = run + _dot(tri, oh1)
    before2 = run + n1 + _dot(tri, oh2)
    r1 = jnp.sum(jnp.where(hot1, before1, 0.0), axis=-1, keepdims=True)
    r2 = jnp.sum(jnp.where(hot2, before2, 0.0), axis=-1, keepdims=True)
    run = run + n1 + n2
    run_ref[...] = run
    cnt_ref[...] = run

    rec = jnp.zeros(logits.shape, F32)
    for idx, val in ((REC_E1, i1.astype(F32)), (REC_E2, i2.astype(F32)), (REC_R1, r1), (REC_R2, r2),
                     (REC_G1, g1), (REC_G2, g2)):
        rec = jnp.where(lane == idx, val, rec)
    rec_ref[...] = rec


def _router(x, w_hilo, bias, tm=512):
    t, d = x.shape
    tri = jnp.asarray(np.tril(np.ones((tm, tm), np.float32), -1), BF16)
    return pl.pallas_call(
        _router_kernel,
        grid=(t // tm,),
        in_specs=[pl.BlockSpec((tm, d), lambda i: (i, 0)),
                  pl.BlockSpec((2, d, LANES), lambda i: (0, 0, 0)),
                  pl.BlockSpec((1, LANES), lambda i: (0, 0)),
                  pl.BlockSpec((tm, tm), lambda i: (0, 0))],
        out_specs=[pl.BlockSpec((tm, LANES), lambda i: (i, 0)),
                   pl.BlockSpec((1, LANES), lambda i: (0, 0))],
        out_shape=[jax.ShapeDtypeStruct((t, LANES), F32), jax.ShapeDtypeStruct((1, LANES), F32)],
        scratch_shapes=[pltpu.VMEM((1, LANES), F32)],
        compiler_params=_params("arbitrary"),
        name="moe_router",
    )(x, w_hilo, bias, tri)


EXPERT_TILE = 1024
ROUTE_TILE = 1024
TOKEN_ROWS = D_MODEL // LANES
DMA_UNROLL = 8


def _token_copy(src, src_tok, dst, dst_tok, sem):
    def tile(tok):
        return pl.ds(pl.multiple_of(tok * TOKEN_ROWS, TOKEN_ROWS), TOKEN_ROWS)
    return pltpu.make_async_copy(src.at[tile(src_tok)], dst.at[tile(dst_tok)], sem)


def _to_token_tiles(ref, n_tok, value):
    for j in range(TOKEN_ROWS):
        ref[pl.ds(j, n_tok, stride=TOKEN_ROWS), :] = value[:, j * LANES:(j + 1) * LANES]


def _from_token_tiles(ref, n_tok, j):
    return ref[pl.ds(j, n_tok, stride=TOKEN_ROWS), :]


def _dispatch_kernel(pos_ref, x_ref, xs_in, xs_hbm, xt_ref, sem):
    del xs_in
    _to_token_tiles(xt_ref, ROUTE_TILE, x_ref[...])

    def copies(t):
        return (_token_copy(xt_ref, t, xs_hbm, pos_ref[0, 0, t], sem),
                _token_copy(xt_ref, t, xs_hbm, pos_ref[0, 0, ROUTE_TILE + t], sem))

    def start(t, carry):
        for cp in copies(t):
            cp.start()
        return carry

    def wait(t, carry):
        for cp in copies(t):
            cp.wait()
        return carry

    lax.fori_loop(0, ROUTE_TILE, start, 0, unroll=DMA_UNROLL)
    lax.fori_loop(0, ROUTE_TILE, wait, 0, unroll=DMA_UNROLL)


def _moe_dispatch(x, pos, n_rows):
    t, d = x.shape
    return pl.pallas_call(
        _dispatch_kernel,
        grid=(t // ROUTE_TILE,),
        in_specs=[pl.BlockSpec((1, 1, 2 * ROUTE_TILE), lambda i: (i, 0, 0), memory_space=pltpu.SMEM),
                  pl.BlockSpec((ROUTE_TILE, d), lambda i: (i, 0)),
                  pl.BlockSpec(memory_space=pl.ANY)],
        out_specs=pl.BlockSpec(memory_space=pl.ANY),
        out_shape=jax.ShapeDtypeStruct((n_rows * TOKEN_ROWS, LANES), F32),
        scratch_shapes=[pltpu.VMEM((ROUTE_TILE * TOKEN_ROWS, LANES), F32), pltpu.SemaphoreType.DMA(())],
        input_output_aliases={2: 0},
        compiler_params=_params("arbitrary"),
        name="moe_dispatch",
    )(pos, x, jnp.zeros((n_rows * TOKEN_ROWS, LANES), F32))


EXPERT_HEAD_ROWS = EXPERT_TILE // 4


def _experts_kernel(te_ref, na_ref, nv_ref, x_ref, wg_ref, wu_ref, wd_ref, o_ref, xb_ref, acc_ref):
    i = pl.program_id(0)
    f = pl.program_id(1)
    active = i < na_ref[0]
    short = nv_ref[i] <= EXPERT_HEAD_ROWS

    @pl.when(active & (f == 0))
    def _():
        for j in range(TOKEN_ROWS):
            xb_ref[:, j * LANES:(j + 1) * LANES] = _from_token_tiles(x_ref, EXPERT_TILE, j).astype(BF16)
        acc_ref[...] = jnp.zeros_like(acc_ref)

    def swiglu(rows):
        xb = xb_ref[rows, :]
        gate = _dot(xb, wg_ref[...])
        up = _dot(xb, wu_ref[...])
        h = (gate * _sigmoid(gate) * up).astype(BF16)
        acc_ref[rows, :] += _dot(h, wd_ref[...])

    @pl.when(active & jnp.logical_not(short))
    def _():
        swiglu(slice(0, EXPERT_TILE))

    @pl.when(active & short)
    def _():
        swiglu(slice(0, EXPERT_HEAD_ROWS))

    @pl.when(f == pl.num_programs(1) - 1)
    def _():
        _to_token_tiles(o_ref, EXPERT_TILE, jnp.where(active, acc_ref[...], 0.0))


def _moe_experts(xs, tile_expert, n_active, tile_rows, wg, wu, wd, tf=512):
    d = D_MODEL
    n_tiles = xs.shape[0] // (EXPERT_TILE * TOKEN_ROWS)
    fe = wg.shape[2]
    row = pl.BlockSpec((EXPERT_TILE * TOKEN_ROWS, LANES), lambda i, f, te, na, nv: (i, 0))
    return pl.pallas_call(
        _experts_kernel,
        grid_spec=pltpu.PrefetchScalarGridSpec(
            num_scalar_prefetch=3,
            grid=(n_tiles, fe // tf),
            in_specs=[row,
                      pl.BlockSpec((None, d, tf), lambda i, f, te, na, nv: (te[i], 0, f)),
                      pl.BlockSpec((None, d, tf), lambda i, f, te, na, nv: (te[i], 0, f)),
                      pl.BlockSpec((None, tf, d), lambda i, f, te, na, nv: (te[i], f, 0))],
            out_specs=row,
            scratch_shapes=[pltpu.VMEM((EXPERT_TILE, d), BF16), pltpu.VMEM((EXPERT_TILE, d), F32)]),
        out_shape=jax.ShapeDtypeStruct(xs.shape, F32),
        compiler_params=_params("parallel", "arbitrary"),
        name="moe_experts",
    )(tile_expert, n_active, tile_rows, xs, wg, wu, wd)


def _combine_kernel(pos_ref, nxt_ref, ys_hbm, x_ref, rec_ref, g_ref, b_ref, o_ref, y_ref, sem):
    i = pl.program_id(0)
    n = pl.num_programs(0)
    slot = i % 2

    def copies(p_ref, s, t):
        return (_token_copy(ys_hbm, p_ref[0, 0, t], y_ref.at[s, 0], t, sem.at[s]),
                _token_copy(ys_hbm, p_ref[0, 0, ROUTE_TILE + t], y_ref.at[s, 1], t, sem.at[s]))

    def start_all(p_ref, s):
        def body(t, carry):
            for cp in copies(p_ref, s, t):
                cp.start()
            return carry
        lax.fori_loop(0, ROUTE_TILE, body, 0, unroll=DMA_UNROLL)

    @pl.when(i == 0)
    def _():
        start_all(pos_ref, slot)

    @pl.when(i + 1 < n)
    def _():
        start_all(nxt_ref, 1 - slot)

    def wait_body(t, carry):
        for cp in copies(pos_ref, slot, t):
            cp.wait()
        return carry
    lax.fori_loop(0, ROUTE_TILE, wait_body, 0, unroll=DMA_UNROLL)

    rec = rec_ref[...]
    lane = lax.broadcasted_iota(jnp.int32, rec.shape, 1)
    g1 = jnp.sum(jnp.where(lane == REC_G1, rec, 0.0), axis=-1, keepdims=True)
    g2 = jnp.sum(jnp.where(lane == REC_G2, rec, 0.0), axis=-1, keepdims=True)
    y1_ref = y_ref.at[slot, 0]
    y2_ref = y_ref.at[slot, 1]
    f = jnp.concatenate([g1 * _from_token_tiles(y1_ref, ROUTE_TILE, j) + g2 * _from_token_tiles(y2_ref, ROUTE_TILE, j)
                         for j in range(TOKEN_ROWS)], axis=1)
    o_ref[...] = _layer_norm(ALPHA * x_ref[...] + f, g_ref[...], b_ref[...])


def _moe_combine_ln(ys, pos, x, rec, g, b):
    t, d = x.shape
    n = t // ROUTE_TILE
    row = pl.BlockSpec((ROUTE_TILE, d), lambda i: (i, 0))
    vec = pl.BlockSpec((1, d), lambda i: (0, 0))
    return pl.pallas_call(
        _combine_kernel,
        grid=(n,),
        in_specs=[pl.BlockSpec((1, 1, 2 * ROUTE_TILE), lambda i: (i, 0, 0), memory_space=pltpu.SMEM),
                  pl.BlockSpec((1, 1, 2 * ROUTE_TILE), lambda i: (jnp.minimum(i + 1, n - 1), 0, 0),
                               memory_space=pltpu.SMEM),
                  pl.BlockSpec(memory_space=pl.ANY),
                  row,
                  pl.BlockSpec((ROUTE_TILE, LANES), lambda i: (i, 0)),
                  vec, vec],
        out_specs=row,
        out_shape=jax.ShapeDtypeStruct((t, d), F32),
        scratch_shapes=[pltpu.VMEM((2, 2, ROUTE_TILE * TOKEN_ROWS, LANES), F32), pltpu.SemaphoreType.DMA((2,))],
        compiler_params=_params("arbitrary"),
        name="moe_combine_ln",
    )(pos, pos, ys, x, rec, g, b)


def _moe_res_ln(x, rec, counts, wg, wu, wd, g, b):
    t, d = x.shape
    n_tiles = 2 * t // EXPERT_TILE + N_EXPERTS
    cnt = counts[0, :N_EXPERTS].astype(jnp.int32)
    padded = (cnt + EXPERT_TILE - 1) // EXPERT_TILE * EXPERT_TILE
    ends = jnp.cumsum(padded)
    starts = ends - padded
    e1 = rec[:, REC_E1].astype(jnp.int32)
    e2 = rec[:, REC_E2].astype(jnp.int32)
    pos1 = starts[e1] + rec[:, REC_R1].astype(jnp.int32)
    pos2 = starts[e2] + rec[:, REC_R2].astype(jnp.int32)
    pos = jnp.concatenate([pos1.reshape(-1, 1, ROUTE_TILE), pos2.reshape(-1, 1, ROUTE_TILE)], axis=2)
    tile_start = jnp.arange(n_tiles, dtype=jnp.int32) * EXPERT_TILE
    tile_expert = jnp.minimum(jnp.sum(tile_start[:, None] >= ends[None, :], axis=1), N_EXPERTS - 1).astype(jnp.int32)
    n_active = (ends[-1:] // EXPERT_TILE).astype(jnp.int32)
    tile_rows = jnp.clip(starts[tile_expert] + cnt[tile_expert] - tile_start, 0, EXPERT_TILE).astype(jnp.int32)
    last_expert = tile_expert[jnp.maximum(n_active[0] - 1, 0)]
    tile_expert = jnp.where(jnp.arange(n_tiles) < n_active[0], tile_expert, last_expert)

    xs = _moe_dispatch(x, pos, n_tiles * EXPERT_TILE)
    ys = _moe_experts(xs, tile_expert, n_active, tile_rows, wg, wu, wd)
    return _moe_combine_ln(ys, pos, x, rec, g, b)


def _prepare_weights(na_w_qkv, na_rpb, na_w_o, ffn_w_gate, ffn_w_up, ffn_w_down,
                     rwkv_mu, rwkv_w_rkv, rwkv_w0, rwkv_w1, rwkv_w2, rwkv_a0, rwkv_a1, rwkv_a2,
                     rwkv_g1, rwkv_g2, rwkv_k_k, rwkv_k_a, rwkv_r_k, rwkv_lnx_g, rwkv_lnx_b, rwkv_w_o,
                     moe_w_router, moe_b_router, moe_w_gate, moe_w_up, moe_w_down,
                     ln_mix_g, ln_mix_b, ln_ffn_g, ln_ffn_b):
    d = D_MODEL
    vec = lambda a: a.reshape(1, d).astype(F32)
    gl_pad = 2 * LANES - GATE_LORA
    router_w = jnp.pad(moe_w_router[0], ((0, 0), (0, LANES - N_EXPERTS)))
    router_hi = router_w.astype(BF16)
    router_lo = (router_w - router_hi.astype(F32)).astype(BF16)
    return dict(
        qkv_w=na_w_qkv[0].astype(BF16),
        qkv_scale=jnp.concatenate([jnp.full((1, d), NA_HEAD_DIM ** -0.5 * LOG2E, F32), jnp.ones((1, 2 * d), F32)], axis=1),
        na_bias=_na_bias_table(na_rpb[0]),
        na_wo=na_w_o[0].astype(BF16),
        ffn_wg=ffn_w_gate[0].astype(BF16), ffn_wu=ffn_w_up[0].astype(BF16), ffn_wd=ffn_w_down[0].astype(BF16),
        mu=rwkv_mu[0].astype(F32),
        wrkv=rwkv_w_rkv[0].astype(BF16),
        w1=jnp.concatenate([rwkv_w1[0, 0], rwkv_w1[0, 1]], axis=1).astype(BF16),
        w2=jnp.concatenate([rwkv_w2[0, 0], rwkv_w2[0, 1]], axis=0).astype(BF16),
        w0=rwkv_w0[0].astype(F32),
        a1=jnp.concatenate([rwkv_a1[0, 0], rwkv_a1[0, 1]], axis=1).astype(BF16),
        a2=jnp.concatenate([rwkv_a2[0, 0], rwkv_a2[0, 1]], axis=0).astype(BF16),
        a0=rwkv_a0[0].astype(F32),
        g1=jnp.pad(rwkv_g1[0], ((0, 0), (0, gl_pad))).astype(BF16),
        g2=jnp.pad(rwkv_g2[0], ((0, gl_pad), (0, 0))).astype(BF16),
        k_k=vec(rwkv_k_k[0]), k_a=vec(rwkv_k_a[0]), r_k=vec(rwkv_r_k[0]),
        lnx_g=vec(rwkv_lnx_g[0]), lnx_b=vec(rwkv_lnx_b[0]),
        rwkv_wo=rwkv_w_o[0].astype(BF16),
        router_w=jnp.stack([router_hi, router_lo]),
        router_b=jnp.pad(moe_b_router[0], (0, LANES - N_EXPERTS)).reshape(1, LANES).astype(F32),
        moe_wg=moe_w_gate[0].astype(BF16), moe_wu=moe_w_up[0].astype(BF16), moe_wd=moe_w_down[0].astype(BF16),
        ln_mix_g=ln_mix_g.astype(F32), ln_mix_b=ln_mix_b.astype(F32),
        ln_ffn_g=ln_ffn_g.astype(F32), ln_ffn_b=ln_ffn_b.astype(F32),
    )


def _trunk(x3, w):
    batch, seq_len, d = x3.shape
    rows = seq_len // GRID_W
    x = x3.reshape(batch * seq_len, d)
    ln = lambda a, i: a[i].reshape(1, d)

    qkv = _qkv_proj(x, w["qkv_w"], w["qkv_scale"])
    att = _na_attention(qkv, w["na_bias"], batch, rows)
    x = _proj_res_ln(att, w["na_wo"], x, ln(w["ln_mix_g"], 0), ln(w["ln_mix_b"], 0))
    x = _ffn_res_ln(x, w["ffn_wg"], w["ffn_wu"], w["ffn_wd"], ln(w["ln_ffn_g"], 0), ln(w["ln_ffn_b"], 0))

    r, k, kk, v, g, lw0, lw1, ia0, ia1 = _rwkv_proj(x, seq_len, w["mu"], w["wrkv"], w["w1"], w["w2"], w["w0"],
                                                    w["a1"], w["a2"], w["a0"], w["g1"], w["g2"], w["k_k"])
    yf, yb = _rwkv_scan(r, k, kk, v, lw0, lw1, ia0, ia1, w["k_a"], batch, seq_len)
    x = _rwkv_out(yf, yb, r, k, v, g, ia0, ia1, x, w["lnx_g"], w["lnx_b"], w["r_k"], w["k_a"], w["rwkv_wo"],
                  ln(w["ln_mix_g"], 1), ln(w["ln_mix_b"], 1))
    rec, counts = _router(x, w["router_w"], w["router_b"])
    x = _moe_res_ln(x, rec, counts, w["moe_wg"], w["moe_wu"], w["moe_wd"], ln(w["ln_ffn_g"], 1), ln(w["ln_ffn_b"], 1))
    return x.reshape(batch, seq_len, d)


def kernel(x_prompt, x_sample, na_w_qkv, na_rpb, na_w_o, ffn_w_gate, ffn_w_up, ffn_w_down, rwkv_mu, rwkv_w_rkv, rwkv_w0, rwkv_w1, rwkv_w2, rwkv_a0, rwkv_a1, rwkv_a2, rwkv_g1, rwkv_g2, rwkv_k_k, rwkv_k_a, rwkv_r_k, rwkv_lnx_g, rwkv_lnx_b, rwkv_w_o, moe_w_router, moe_b_router, moe_w_gate, moe_w_up, moe_w_down, ln_mix_g, ln_mix_b, ln_ffn_g, ln_ffn_b):
    w = _prepare_weights(na_w_qkv, na_rpb, na_w_o, ffn_w_gate, ffn_w_up, ffn_w_down,
                         rwkv_mu, rwkv_w_rkv, rwkv_w0, rwkv_w1, rwkv_w2, rwkv_a0, rwkv_a1, rwkv_a2,
                         rwkv_g1, rwkv_g2, rwkv_k_k, rwkv_k_a, rwkv_r_k, rwkv_lnx_g, rwkv_lnx_b, rwkv_w_o,
                         moe_w_router, moe_b_router, moe_w_gate, moe_w_up, moe_w_down,
                         ln_mix_g, ln_mix_b, ln_ffn_g, ln_ffn_b)
    return (_trunk(x_prompt, w), _trunk(x_sample, w))
```

```python
import functools
import math

import numpy as np
import jax
import jax.numpy as jnp
from jax import lax
from jax.experimental import pallas as pl
from jax.experimental.pallas import tpu as pltpu

D_MODEL = 1024
DEPTH = 2
GRID_W = 64
NA_HEAD_DIM = 32
NA_HEADS = D_MODEL // NA_HEAD_DIM
NA_ROWS = 8
NA_COLS = 16
RWKV_HEAD_DIM = 64
DECAY_LORA = 64
GATE_LORA = 160
GN_EPS = 64e-5
N_EXPERTS = 8
LN_EPS = 1e-5
ALPHA = (2 * DEPTH) ** 0.25
LOG2E = math.log2(math.e)

LANES = 128
SCAN_CHUNK = 64
SCAN_BLOCK = 256
VMEM_LIMIT = 56 * 1024 * 1024

F32 = jnp.float32
BF16 = jnp.bfloat16


def _params(*sem):
    return pltpu.CompilerParams(dimension_semantics=sem, vmem_limit_bytes=VMEM_LIMIT)


def _dot(a, b):
    return jnp.dot(a, b, preferred_element_type=F32)


def _dot_nt(a, b):
    return lax.dot_general(a, b, (((1,), (1,)), ((), ())), preferred_element_type=F32)


def _dot_tn(a, b):
    return lax.dot_general(a, b, (((0,), (0,)), ((), ())), preferred_element_type=F32)


def _split(a):
    hi = a.astype(BF16)
    lo = (a - hi.astype(F32)).astype(BF16)
    return hi, lo


def _layer_norm(z, g, b):
    mu = jnp.mean(z, axis=-1, keepdims=True)
    zc = z - mu
    var = jnp.mean(zc * zc, axis=-1, keepdims=True)
    return zc * lax.rsqrt(var + LN_EPS) * g + b


def _sigmoid(z):
    return 1.0 / (1.0 + jnp.exp(-z))


def _head_sum_matrix(head_dim):
    lane = np.arange(LANES)
    return jnp.asarray((lane[:, None] // head_dim) == (lane[None, :] // head_dim), BF16)


def _qkv_kernel(x_ref, w_ref, s_ref, o_ref):
    acc = _dot(x_ref[...].astype(BF16), w_ref[...])
    o_ref[...] = (acc * s_ref[...]).astype(o_ref.dtype)


def _qkv_proj(x, w, colscale, tm=512):
    t, d = x.shape
    n = w.shape[1]
    return pl.pallas_call(
        _qkv_kernel,
        grid=(t // tm,),
        in_specs=[pl.BlockSpec((tm, d), lambda i: (i, 0)),
                  pl.BlockSpec((d, n), lambda i: (0, 0)),
                  pl.BlockSpec((1, n), lambda i: (0, 0))],
        out_specs=pl.BlockSpec((tm, n), lambda i: (i, 0)),
        out_shape=jax.ShapeDtypeStruct((t, n), BF16),
        compiler_params=_params("parallel"),
        name="qkv_proj",
    )(x, w, colscale)


NA_HEADS_PER_GROUP = LANES // NA_HEAD_DIM
NA_GROUPS = D_MODEL // LANES
NA_WIN = NA_ROWS * GRID_W


NA_GROUP_ROWS = NA_HEADS_PER_GROUP * GRID_W
NA_ROW_PAIRS = NA_ROWS // 2
NA_BIAS_PAIRS = 2 * NA_ROWS - 2


NA_STEP_ROWS = 4
NA_FETCH_ROWS = NA_ROWS + NA_STEP_ROWS - 1


def _na_win_start(r, rows):
    return jnp.clip(r - NA_ROWS // 2, 0, rows - NA_ROWS)


def _na_fetch_start(r0, rows):
    return jnp.minimum(_na_win_start(r0, rows), rows - NA_FETCH_ROWS)


def _na_kernel(rows, q_ref, k_ref, v_ref, b_ref, o_ref):
    r0 = pl.program_id(1) * NA_STEP_ROWS
    fetch = _na_fetch_start(r0, rows)
    lane_head = lax.broadcasted_iota(jnp.int32, (GRID_W, LANES), 1) // NA_HEAD_DIM
    groups = [slice(g * LANES, (g + 1) * LANES) for g in range(NA_GROUPS)]
    units = []
    for ri in range(NA_STEP_ROWS):
        win = _na_win_start(r0 + ri, rows)
        keys = pl.ds(pl.multiple_of((win - fetch) * GRID_W, GRID_W), NA_WIN)
        for g in range(NA_GROUPS):
            units.append((slice(ri * GRID_W, (ri + 1) * GRID_W), keys, r0 + ri - win, g))

    scores = []
    for qrows, keys, _, g in units:
        qg = q_ref[qrows, groups[g]]
        zero = jnp.zeros_like(qg)
        qm = jnp.concatenate([jnp.where(lane_head == h, qg, zero) for h in range(NA_HEADS_PER_GROUP)], axis=0)
        scores.append(_dot_nt(qm, k_ref[keys, groups[g]]))

    probs, denoms = [], []
    for (_, _, off, g), s in zip(units, scores):
        hq = pl.ds(g * NA_GROUP_ROWS, NA_GROUP_ROWS)
        s = jnp.concatenate([s[:, m * LANES:(m + 1) * LANES] + b_ref[2 * m - off + NA_ROWS - 1, hq, :]
                             for m in range(NA_ROW_PAIRS)], axis=1)
        p = jnp.exp2(s - jnp.max(s, axis=-1, keepdims=True))
        denoms.append(jnp.sum(p, axis=-1, keepdims=True))
        probs.append(p.astype(BF16))

    for (qrows, keys, _, g), p, l in zip(units, probs, denoms):
        o4 = _dot(p, v_ref[keys, groups[g]]) / l
        og = jnp.zeros((GRID_W, LANES), F32)
        for h in range(NA_HEADS_PER_GROUP):
            og = og + jnp.where(lane_head == h, o4[h * GRID_W:(h + 1) * GRID_W], 0.0)
        o_ref[qrows, groups[g]] = og.astype(o_ref.dtype)


def _na_attention(qkv, bias, batch, rows):
    t = qkv.shape[0]
    steps = rows // NA_STEP_ROWS
    q_rows = NA_STEP_ROWS * GRID_W

    def q_map(b, s):
        return (b * steps + s, 0)

    def k_map(b, s):
        return ((b * rows + _na_fetch_start(s * NA_STEP_ROWS, rows)) * GRID_W, D_MODEL)

    def v_map(b, s):
        return ((b * rows + _na_fetch_start(s * NA_STEP_ROWS, rows)) * GRID_W, 2 * D_MODEL)

    window = (pl.Element(NA_FETCH_ROWS * GRID_W), pl.Element(D_MODEL))
    return pl.pallas_call(
        functools.partial(_na_kernel, rows),
        grid=(batch, steps),
        in_specs=[pl.BlockSpec((q_rows, D_MODEL), q_map),
                  pl.BlockSpec(window, k_map),
                  pl.BlockSpec(window, v_map),
                  pl.BlockSpec(bias.shape, lambda b, s: (0, 0, 0), pipeline_mode=pl.Buffered(1))],
        out_specs=pl.BlockSpec((q_rows, D_MODEL), q_map),
        out_shape=jax.ShapeDtypeStruct((t, D_MODEL), BF16),
        compiler_params=_params("parallel", "arbitrary"),
        name="na_attention",
    )(qkv, qkv, qkv, bias)


def _na_bias_kernel(rpb_ref, oh_ref, mask_ref, o_ref):
    x = rpb_ref[...]
    hi = x.astype(BF16)
    r1 = x - hi.astype(F32)
    mid = r1.astype(BF16)
    lo = (r1 - mid.astype(F32)).astype(BF16)
    oh = oh_ref[...]
    o_ref[...] = (_dot(hi, oh) + _dot(mid, oh) + _dot(lo, oh)) * LOG2E + mask_ref[...]


def _na_bias_table(rpb, tn=1024):
    n_dr, n_dc = 2 * NA_ROWS - 1, 2 * NA_COLS - 1
    c = np.arange(GRID_W)
    q_start = np.clip(c - NA_COLS // 2, 0, GRID_W - NA_COLS)
    kc = np.arange(GRID_W)
    valid = (kc[None, :] >= q_start[:, None]) & (kc[None, :] < q_start[:, None] + NA_COLS)
    dc = kc[None, :] - c[:, None] + NA_COLS - 1
    onehot = (np.arange(LANES)[:, None, None] == dc[None]) & valid[None]
    onehot = jnp.asarray(onehot.reshape(LANES, GRID_W * GRID_W), BF16)
    mask = jnp.asarray(np.where(valid, 0.0, -np.inf).reshape(1, GRID_W * GRID_W), F32)
    rpb2 = jnp.pad(rpb.reshape(NA_HEADS * n_dr, n_dc).astype(F32), ((0, 0), (0, LANES - n_dc)))
    nrow, ncol = rpb2.shape[0], GRID_W * GRID_W
    flat = pl.pallas_call(
        _na_bias_kernel,
        grid=(ncol // tn,),
        in_specs=[pl.BlockSpec((nrow, LANES), lambda j: (0, 0)),
                  pl.BlockSpec((LANES, tn), lambda j: (0, j)),
                  pl.BlockSpec((1, tn), lambda j: (0, j))],
        out_specs=pl.BlockSpec((nrow, tn), lambda j: (0, j)),
        out_shape=jax.ShapeDtypeStruct((nrow, ncol), F32),
        compiler_params=_params("parallel"),
        name="na_bias_table",
    )(rpb2, onehot, mask)
    toe = flat.reshape(NA_HEADS, n_dr, GRID_W, GRID_W)
    pairs = jnp.stack([toe[:, 0:NA_BIAS_PAIRS], toe[:, 1:NA_BIAS_PAIRS + 1]], axis=3)
    return jnp.transpose(pairs, (1, 0, 2, 3, 4)).reshape(NA_BIAS_PAIRS, NA_HEADS * GRID_W, LANES)


def _proj_ln_kernel(a_ref, w_ref, x_ref, g_ref, b_ref, o_ref):
    h = _dot(a_ref[...], w_ref[...])
    o_ref[...] = _layer_norm(ALPHA * x_ref[...] + h, g_ref[...], b_ref[...])


def _proj_res_ln(a, w, x, g, b, tm=512):
    t, d = x.shape
    row = pl.BlockSpec((tm, d), lambda i: (i, 0))
    vec = pl.BlockSpec((1, d), lambda i: (0, 0))
    return pl.pallas_call(
        _proj_ln_kernel,
        grid=(t // tm,),
        in_specs=[row, pl.BlockSpec((d, d), lambda i: (0, 0)), row, vec, vec],
        out_specs=row,
        out_shape=jax.ShapeDtypeStruct((t, d), F32),
        compiler_params=_params("parallel"),
        name="proj_res_ln",
    )(a, w, x, g, b)


def _ffn_kernel(x_ref, wg_ref, wu_ref, wd_ref, g_ref, b_ref, o_ref, xb_ref, acc_ref):
    f = pl.program_id(1)

    @pl.when(f == 0)
    def _():
        xb_ref[...] = x_ref[...].astype(BF16)
        acc_ref[...] = jnp.zeros_like(acc_ref)

    xb = xb_ref[...]
    gate = _dot(xb, wg_ref[...])
    up = _dot(xb, wu_ref[...])
    h = (gate * _sigmoid(gate) * up).astype(BF16)
    acc_ref[...] += _dot(h, wd_ref[...])

    @pl.when(f == pl.num_programs(1) - 1)
    def _():
        o_ref[...] = _layer_norm(ALPHA * x_ref[...] + acc_ref[...], g_ref[...], b_ref[...])


def _ffn_res_ln(x, wg, wu, wd, g, b, tm=512, tf=1408):
    t, d = x.shape
    ff = wg.shape[1]
    row = pl.BlockSpec((tm, d), lambda i, f: (i, 0))
    vec = pl.BlockSpec((1, d), lambda i, f: (0, 0))
    return pl.pallas_call(
        _ffn_kernel,
        grid=(t // tm, ff // tf),
        in_specs=[row,
                  pl.BlockSpec((d, tf), lambda i, f: (0, f)),
                  pl.BlockSpec((d, tf), lambda i, f: (0, f)),
                  pl.BlockSpec((tf, d), lambda i, f: (f, 0)),
                  vec, vec],
        out_specs=row,
        out_shape=jax.ShapeDtypeStruct((t, d), F32),
        scratch_shapes=[pltpu.VMEM((tm, d), BF16), pltpu.VMEM((tm, d), F32)],
        compiler_params=_params("parallel", "arbitrary"),
        name="ffn_res_ln",
    )(x, wg, wu, wd, g, b)


HALO = 8


def _rwkv_proj_kernel(seq_len, x_ref, xp_ref, xn_ref, mu_ref, wrkv_ref, w1_ref, w2_ref, w0_ref,
                      a1_ref, a2_ref, a0_ref, g1_ref, g2_ref, kscale_ref, hs_ref,
                      r_ref, k_ref, kk_ref, v_ref, g_ref, lw0_ref, lw1_ref, ia0_ref, ia1_ref):
    tm = x_ref.shape[0]
    i = pl.program_id(0)
    x = x_ref[...]
    row = lax.broadcasted_iota(jnp.int32, x.shape, 0)
    first_in_seq = (i * tm) % seq_len == 0
    last_in_seq = ((i + 1) * tm) % seq_len == 0
    prev_row = jnp.where(first_in_seq, 0.0, xp_ref[HALO - 1:HALO, :])
    next_row = jnp.where(last_in_seq, 0.0, xn_ref[0:1, :])
    x_prev = jnp.where(row == 0, prev_row, pltpu.roll(x, 1, axis=0))
    x_next = jnp.where(row == tm - 1, next_row, pltpu.roll(x, tm - 1, axis=0))
    xx = 0.5 * (x_prev + x_next) - x

    mixes = [(x + xx * mu_ref[j:j + 1, :]).astype(BF16) for j in range(6)]
    rf = _dot(mixes[0], wrkv_ref[0])
    kf = _dot(mixes[2], wrkv_ref[1])
    vf = _dot(mixes[3], wrkv_ref[2])
    lo_pre = _dot(mixes[1], w1_ref[...])
    al = _dot(mixes[4], a1_ref[...])
    gl_pre = _dot(mixes[5], g1_ref[...])

    r_ref[...] = rf.astype(r_ref.dtype)
    k_ref[...] = kf.astype(k_ref.dtype)
    v_ref[...] = vf.astype(v_ref.dtype)
    lane = lax.broadcasted_iota(jnp.int32, (tm, LANES), 1)
    dir0 = lane < DECAY_LORA
    lo = jnp.tanh(lo_pre)
    lz = [jnp.where(dir0 if z == 0 else ~dir0, lo, 0.0).astype(BF16) for z in range(2)]
    az = [jnp.where(dir0 if z == 0 else ~dir0, al, 0.0).astype(BF16) for z in range(2)]
    gl = _sigmoid(gl_pre).astype(BF16)
    groups = [slice(gi * LANES, (gi + 1) * LANES) for gi in range(D_MODEL // LANES)]
    kkr = [kf[:, cols] * kscale_ref[:, cols] for cols in groups]
    sq = [_split(q * q) for q in kkr]

    wl = [_dot(lz[z], w2_ref[...]) for z in range(2)]
    ai = [_dot(az[z], a2_ref[...]) for z in range(2)]
    gf = _dot(gl, g2_ref[...])
    hsum = hs_ref[...]
    norm2 = [_dot(hi, hsum) + _dot(lo_, hsum) for hi, lo_ in sq]

    for z, out in enumerate((lw0_ref, lw1_ref)):
        out[...] = (-math.exp(-0.5)) * _sigmoid(w0_ref[z:z + 1, :] + wl[z])
    for z, out in enumerate((ia0_ref, ia1_ref)):
        out[...] = _sigmoid(a0_ref[z:z + 1, :] + ai[z]).astype(out.dtype)
    g_ref[...] = gf.astype(g_ref.dtype)
    for cols, q, n2 in zip(groups, kkr, norm2):
        kk_ref[:, cols] = (q / jnp.maximum(jnp.sqrt(n2), 1e-12)).astype(kk_ref.dtype)


def _rwkv_proj(x, seq_len, mu, wrkv, w1, w2, w0, a1, a2, a0, g1, g2, kscale, tm=256):
    t, d = x.shape
    nb = t // HALO
    step = tm // HALO
    row = pl.BlockSpec((tm, d), lambda i: (i, 0))
    full = lambda shape: pl.BlockSpec(shape, lambda i: (0,) * len(shape))
    hsum = _head_sum_matrix(RWKV_HEAD_DIM)
    outs = [jax.ShapeDtypeStruct((t, d), BF16)] * 5 + [jax.ShapeDtypeStruct((t, d), F32)] * 2 + \
           [jax.ShapeDtypeStruct((t, d), BF16)] * 2
    return pl.pallas_call(
        functools.partial(_rwkv_proj_kernel, seq_len),
        grid=(t // tm,),
        in_specs=[row,
                  pl.BlockSpec((HALO, d), lambda i: (jnp.maximum(i * step - 1, 0), 0)),
                  pl.BlockSpec((HALO, d), lambda i: (jnp.minimum((i + 1) * step, nb - 1), 0)),
                  full(mu.shape), full(wrkv.shape), full(w1.shape), full(w2.shape), full(w0.shape),
                  full(a1.shape), full(a2.shape), full(a0.shape), full(g1.shape), full(g2.shape),
                  full(kscale.shape), full(hsum.shape)],
        out_specs=[row] * 9,
        out_shape=outs,
        compiler_params=_params("parallel"),
        name="rwkv_proj",
    )(x, x, x, mu, wrkv, w1, w2, w0, a1, a2, a0, g1, g2, kscale, hsum)


def _pair_blockdiag(y, lane_a):
    zero = jnp.zeros_like(y)
    return jnp.concatenate([jnp.where(lane_a, y, zero), jnp.where(lane_a, zero, y)], axis=0)


def _pair_matmul(x, y, lane_a):
    return _dot(x.astype(BF16), _pair_blockdiag(y.astype(BF16), lane_a))


def _scan_prepare(chains):
    c = SCAN_CHUNK
    t_idx = lax.broadcasted_iota(jnp.int32, (c, LANES), 0)
    s_idx = lax.broadcasted_iota(jnp.int32, (c, LANES), 1) % c
    lane_a = lax.broadcasted_iota(jnp.int32, (c, LANES), 1) < RWKV_HEAD_DIM
    eye = (s_idx == t_idx).astype(F32)
    tri_r = lax.broadcasted_iota(jnp.int32, (c, c), 0)
    tri_c = lax.broadcasted_iota(jnp.int32, (c, c), 1)
    bd = (lax.broadcasted_iota(jnp.int32, (LANES, LANES), 0) // RWKV_HEAD_DIM ==
          lax.broadcasted_iota(jnp.int32, (LANES, LANES), 1) // RWKV_HEAD_DIM)
    before = {rev: ((s_idx > t_idx) if rev else (s_idx < t_idx)) for rev in (False, True)}
    upto = {rev: before[rev] | (s_idx == t_idx) for rev in (False, True)}
    cum = {rev: ((tri_c >= tri_r) if rev else (tri_c <= tri_r)).astype(BF16) for rev in (False, True)}
    n = len(chains)

    lw_split = [jnp.concatenate(_split(ch[6]), axis=1) for ch in chains]
    lwc2 = [_dot(cum[chains[i][0]], lw_split[i]) for i in range(n)]
    lwc = [x[:, 0:LANES] + x[:, LANES:2 * LANES] for x in lwc2]
    vals = [ch[5] for ch in chains]

    at, rt, bh, kh, w_tot, lhs, rhs = [], [], [], [], [], [], []
    for (rev, ka, r, k, kk, v, lw, ia), lc in zip(chains, lwc):
        last = lc[0:1, :] if rev else lc[c - 1:c, :]
        w_in = jnp.exp(lc)
        w_ex = jnp.exp(lc - lw)
        w_inv = jnp.exp(-lc)
        w_end = jnp.exp(last)
        w_out = w_end * w_inv
        kf = k.astype(F32)
        kkf = kk.astype(F32)
        iaf = ia.astype(F32)
        kd = kf * (1.0 + (iaf - 1.0) * ka)
        bvec = kkf * iaf
        a_t = (-kkf) * w_ex
        r_t = r.astype(F32) * w_in
        at.append(a_t)
        rt.append(r_t)
        bh.append((bvec * w_out).astype(BF16))
        kh.append((kd * w_out).astype(BF16))
        w_tot.append(w_end)
        lhs.append(jnp.concatenate([a_t, r_t], axis=0).astype(BF16))
        rhs.append(jnp.concatenate([_pair_blockdiag((bvec * w_inv).astype(BF16), lane_a),
                                    _pair_blockdiag((kd * w_inv).astype(BF16), lane_a)], axis=0))

    gram = [_dot_nt(lhs[i], rhs[i]) for i in range(n)]
    a_ab = [jnp.where(before[chains[i][0]], gram[i][0:c, 0:LANES], 0.0) for i in range(n)]
    a_ak = [jnp.where(before[chains[i][0]], gram[i][0:c, LANES:2 * LANES], 0.0) for i in range(n)]
    a_rb = [jnp.where(upto[chains[i][0]], gram[i][c:2 * c, 0:LANES], 0.0) for i in range(n)]
    a_rk = [jnp.where(upto[chains[i][0]], gram[i][c:2 * c, LANES:2 * LANES], 0.0) for i in range(n)]

    tinv = [eye + a for a in a_ab]
    pw = [_pair_matmul(a, a, lane_a) for a in a_ab]
    n_sq = int(math.log2(c)) - 1
    for j in range(n_sq):
        final = j == n_sq - 1
        lhs_j = [(tinv[i] if final else jnp.concatenate([tinv[i], pw[i]], axis=0)).astype(BF16) for i in range(n)]
        res = [_dot(lhs_j[i], _pair_blockdiag(pw[i].astype(BF16), lane_a)) for i in range(n)]
        tinv = [tinv[i] + res[i][0:c] for i in range(n)]
        if not final:
            pw = [res[i][c:2 * c] for i in range(n)]

    av = [_pair_matmul(a_ak[i], vals[i], lane_a) for i in range(n)]
    y_loc = [_pair_matmul(a_rk[i], vals[i], lane_a) for i in range(n)]
    pq = [_dot(tinv[i].astype(BF16),
               jnp.concatenate([_pair_blockdiag(at[i].astype(BF16), lane_a),
                                _pair_blockdiag(av[i].astype(BF16), lane_a)], axis=1)) for i in range(n)]
    pqb = [x.astype(BF16) for x in pq]
    mn = [_dot_tn(pqb[i], bh[i]) for i in range(n)]
    kv = [_dot_tn(vals[i], kh[i]) for i in range(n)]
    arb = [_dot(a_rb[i].astype(BF16),
                jnp.concatenate([_pair_blockdiag(pqb[i][:, 0:LANES], lane_a),
                                 _pair_blockdiag(pqb[i][:, LANES:2 * LANES], lane_a)], axis=1)) for i in range(n)]
    out = []
    for i in range(n):
        ry = (rt[i] + arb[i][:, 0:LANES]).astype(BF16)
        yc = arb[i][:, LANES:2 * LANES] + y_loc[i]
        m_state = jnp.where(bd, mn[i][0:LANES], 0.0).astype(BF16)
        n_state = jnp.where(bd, mn[i][LANES:2 * LANES] + kv[i], 0.0)
        out.append((ry, yc, m_state, n_state, w_tot[i]))
    return out


def _scan_kernel(n_pairs, rf_ref, kf_ref, kkf_ref, vf_ref, lwf_ref, iaf_ref,
                 rb_ref, kb_ref, kkb_ref, vb_ref, lwb_ref, iab_ref, ka_ref, yf_ref, yb_ref, s_ref):
    @pl.when(pl.program_id(2) == 0)
    def _():
        s_ref[...] = jnp.zeros_like(s_ref)

    c = SCAN_CHUNK
    n_chunks = SCAN_BLOCK // c
    dirs = ((False, (rf_ref, kf_ref, kkf_ref, vf_ref, lwf_ref, iaf_ref), yf_ref),
            (True, (rb_ref, kb_ref, kkb_ref, vb_ref, lwb_ref, iab_ref), yb_ref))
    chains, where = [], []
    for step in range(n_chunks):
        for p in range(n_pairs):
            cols = slice(p * LANES, (p + 1) * LANES)
            for di, (rev, refs, _) in enumerate(dirs):
                ci = n_chunks - 1 - step if rev else step
                rows = slice(ci * c, (ci + 1) * c)
                chains.append((rev, ka_ref[:, cols]) + tuple(ref[rows, cols] for ref in refs))
                where.append((2 * p + di, di, rows, cols))
    pre = _scan_prepare(chains)
    state = [s_ref[j] for j in range(2 * n_pairs)]
    for (j, di, rows, cols), (ry, yc, m_state, n_state, w_total) in zip(where, pre):
        s = state[j]
        sb = s.astype(BF16)
        dirs[di][2][rows, cols] = _dot_nt(ry, sb) + yc
        state[j] = s * w_total + _dot(sb, m_state) + n_state
    for j in range(2 * n_pairs):
        s_ref[j] = state[j]


def _rwkv_scan(r, k, kk, v, lw0, lw1, ia0, ia1, ka, batch, seq_len, n_pairs=8):
    t, d = r.shape
    ns = seq_len // SCAN_BLOCK
    width = n_pairs * LANES
    fwd = pl.BlockSpec((SCAN_BLOCK, width), lambda b, h, s: (b * ns + s, h))
    bwd = pl.BlockSpec((SCAN_BLOCK, width), lambda b, h, s: (b * ns + ns - 1 - s, h))
    vec = pl.BlockSpec((1, width), lambda b, h, s: (0, h))
    return pl.pallas_call(
        functools.partial(_scan_kernel, n_pairs),
        grid=(batch, d // width, ns),
        in_specs=[fwd] * 6 + [bwd] * 6 + [vec],
        out_specs=[fwd, bwd],
        out_shape=[jax.ShapeDtypeStruct((t, d), F32)] * 2,
        scratch_shapes=[pltpu.VMEM((2 * n_pairs, LANES, LANES), F32)],
        compiler_params=_params("parallel", "parallel", "arbitrary"),
        name="rwkv_scan",
    )(r, k, kk, v, lw0, ia0, r, k, kk, v, lw1, ia1, ka)


def _rwkv_out_kernel(yf_ref, yb_ref, r_ref, k_ref, v_ref, g_ref, ia0_ref, ia1_ref, x_ref,
                     lg_ref, lb_ref, rk_ref, ka_ref, hs_ref, wo_ref, ng_ref, nb_ref, o_ref, z_ref):
    hsum = hs_ref[...]
    inv = 1.0 / RWKV_HEAD_DIM
    groups = [slice(gi * LANES, (gi + 1) * LANES) for gi in range(D_MODEL // LANES)]

    def head_sums(vals):
        parts = [_split(v) for v in vals]
        return [_dot(hi, hsum) + _dot(lo, hsum) for hi, lo in parts]

    y = [yf_ref[:, cols] + yb_ref[:, cols] for cols in groups]
    rkk = []
    for cols in groups:
        ia = ia0_ref[:, cols].astype(F32) + ia1_ref[:, cols].astype(F32)
        ksum = k_ref[:, cols].astype(F32) * (2.0 + (ia - 2.0) * ka_ref[:, cols])
        rkk.append(r_ref[:, cols].astype(F32) * ksum * rk_ref[:, cols])
    sums = head_sums(y + rkk)
    n_g = len(groups)
    yc = [y[gi] - sums[gi] * inv for gi in range(n_g)]
    var = head_sums([c * c for c in yc])
    for gi, cols in enumerate(groups):
        yn = yc[gi] * lax.rsqrt(var[gi] * inv + GN_EPS) * lg_ref[:, cols] + lb_ref[:, cols]
        bonus = sums[n_g + gi] * v_ref[:, cols].astype(F32)
        z_ref[:, cols] = ((yn + bonus) * g_ref[:, cols].astype(F32)).astype(BF16)
    h = _dot(z_ref[...], wo_ref[...])
    o_ref[...] = _layer_norm(ALPHA * x_ref[...] + h, ng_ref[...], nb_ref[...])


def _rwkv_out(yf, yb, r, k, v, g, ia0, ia1, x, lnx_g, lnx_b, r_k, k_a, wo, ng, nb, tm=256):
    t, d = x.shape
    row = pl.BlockSpec((tm, d), lambda i: (i, 0))
    vec = pl.BlockSpec((1, d), lambda i: (0, 0))
    return pl.pallas_call(
        _rwkv_out_kernel,
        grid=(t // tm,),
        in_specs=[row] * 9 + [vec] * 4 + [pl.BlockSpec((LANES, LANES), lambda i: (0, 0)),
                                          pl.BlockSpec((d, d), lambda i: (0, 0)), vec, vec],
        out_specs=row,
        out_shape=jax.ShapeDtypeStruct((t, d), F32),
        scratch_shapes=[pltpu.VMEM((tm, d), BF16)],
        compiler_params=_params("parallel"),
        name="rwkv_out",
    )(yf, yb, r, k, v, g, ia0, ia1, x, lnx_g, lnx_b, r_k, k_a, _head_sum_matrix(RWKV_HEAD_DIM), wo, ng, nb)


REC_E1, REC_E2, REC_R1, REC_R2, REC_G1, REC_G2 = range(6)


def _router_kernel(x_ref, w_ref, b_ref, tri_ref, rec_ref, cnt_ref, run_ref):
    @pl.when(pl.program_id(0) == 0)
    def _():
        run_ref[...] = jnp.zeros_like(run_ref)

    xh, xl = _split(x_ref[...])
    wh = w_ref[0]
    wl = w_ref[1]
    logits = _dot(xh, wh) + (_dot(xh, wl) + _dot(xl, wh)) + b_ref[...]
    lane = lax.broadcasted_iota(jnp.int32, logits.shape, 1)
    logits = jnp.where(lane < N_EXPERTS, logits, -jnp.inf)
    m1 = jnp.max(logits, axis=-1, keepdims=True)
    i1 = jnp.min(jnp.where(logits == m1, lane, LANES), axis=-1, keepdims=True)
    rest = jnp.where(lane == i1, -jnp.inf, logits)
    m2 = jnp.max(rest, axis=-1, keepdims=True)
    i2 = jnp.min(jnp.where(rest == m2, lane, LANES), axis=-1, keepdims=True)
    e2 = jnp.exp(m2 - m1)
    g1 = 1.0 / (1.0 + e2)
    g2 = e2 / (1.0 + e2)

    hot1 = lane == i1
    hot2 = lane == i2
    oh1 = hot1.astype(BF16)
    oh2 = hot2.astype(BF16)
    tri = tri_ref[...]
    run = run_ref[...]
    n1 = jnp.sum(oh1.astype(F32), axis=0, keepdims=True)
    n2 = jnp.sum(oh2.astype(F32), axis=0, keepdims=True)
    before1 = run + _dot(tri, oh1)
    before2 = run + n1 + _dot(tri, oh2)
    r1 = jnp.sum(jnp.where(hot1, before1, 0.0), axis=-1, keepdims=True)
    r2 = jnp.sum(jnp.where(hot2, before2, 0.0), axis=-1, keepdims=True)
    run = run + n1 + n2
    run_ref[...] = run
    cnt_ref[...] = run

    rec = jnp.zeros(logits.shape, F32)
    for idx, val in ((REC_E1, i1.astype(F32)), (REC_E2, i2.astype(F32)), (REC_R1, r1), (REC_R2, r2),
                     (REC_G1, g1), (REC_G2, g2)):
        rec = jnp.where(lane == idx, val, rec)
    rec_ref[...] = rec


def _router(x, w_hilo, bias, tm=512):
    t, d = x.shape
    tri = jnp.asarray(np.tril(np.ones((tm, tm), np.float32), -1), BF16)
    return pl.pallas_call(
        _router_kernel,
        grid=(t // tm,),
        in_specs=[pl.BlockSpec((tm, d), lambda i: (i, 0)),
                  pl.BlockSpec((2, d, LANES), lambda i: (0, 0, 0)),
                  pl.BlockSpec((1, LANES), lambda i: (0, 0)),
                  pl.BlockSpec((tm, tm), lambda i: (0, 0))],
        out_specs=[pl.BlockSpec((tm, LANES), lambda i: (i, 0)),
                   pl.BlockSpec((1, LANES), lambda i: (0, 0))],
        out_shape=[jax.ShapeDtypeStruct((t, LANES), F32), jax.ShapeDtypeStruct((1, LANES), F32)],
        scratch_shapes=[pltpu.VMEM((1, LANES), F32)],
        compiler_params=_params("arbitrary"),
        name="moe_router",
    )(x, w_hilo, bias, tri)


EXPERT_TILE = 1024
ROUTE_TILE = 512
TOKEN_ROWS = D_MODEL // LANES
DMA_UNROLL = 8


def _token_copy(src, src_tok, dst, dst_tok, sem):
    def tile(tok):
        return pl.ds(pl.multiple_of(tok * TOKEN_ROWS, TOKEN_ROWS), TOKEN_ROWS)
    return pltpu.make_async_copy(src.at[tile(src_tok)], dst.at[tile(dst_tok)], sem)


def _to_token_tiles(ref, n_tok, value):
    for j in range(TOKEN_ROWS):
        ref[pl.ds(j, n_tok, stride=TOKEN_ROWS), :] = value[:, j * LANES:(j + 1) * LANES]


def _from_token_tiles(ref, n_tok, j):
    return ref[pl.ds(j, n_tok, stride=TOKEN_ROWS), :]


def _dispatch_kernel(pos_ref, prv_ref, x_ref, xs_in, xs_hbm, xt_ref, sem):
    del xs_in
    i = pl.program_id(0)
    n = pl.num_programs(0)
    slot = i % 2

    def copies(p_ref, s, t):
        return (_token_copy(xt_ref.at[s], t, xs_hbm, p_ref[0, 0, t], sem.at[s]),
                _token_copy(xt_ref.at[s], t, xs_hbm, p_ref[0, 0, ROUTE_TILE + t], sem.at[s]))

    def wait_all(p_ref, s):
        def body(t, carry):
            for cp in copies(p_ref, s, t):
                cp.wait()
            return carry
        lax.fori_loop(0, ROUTE_TILE, body, 0, unroll=DMA_UNROLL)

    _to_token_tiles(xt_ref.at[slot], ROUTE_TILE, x_ref[...])

    def start(t, carry):
        for cp in copies(pos_ref, slot, t):
            cp.start()
        return carry
    lax.fori_loop(0, ROUTE_TILE, start, 0, unroll=DMA_UNROLL)

    @pl.when(i > 0)
    def _():
        wait_all(prv_ref, 1 - slot)

    @pl.when(i == n - 1)
    def _():
        wait_all(pos_ref, slot)


def _moe_dispatch(x, pos, n_rows):
    t, d = x.shape
    return pl.pallas_call(
        _dispatch_kernel,
        grid=(t // ROUTE_TILE,),
        in_specs=[pl.BlockSpec((1, 1, 2 * ROUTE_TILE), lambda i: (i, 0, 0), memory_space=pltpu.SMEM),
                  pl.BlockSpec((1, 1, 2 * ROUTE_TILE), lambda i: (jnp.maximum(i - 1, 0), 0, 0),
                               memory_space=pltpu.SMEM),
                  pl.BlockSpec((ROUTE_TILE, d), lambda i: (i, 0)),
                  pl.BlockSpec(memory_space=pl.ANY)],
        out_specs=pl.BlockSpec(memory_space=pl.ANY),
        out_shape=jax.ShapeDtypeStruct((n_rows * TOKEN_ROWS, LANES), F32),
        scratch_shapes=[pltpu.VMEM((2, ROUTE_TILE * TOKEN_ROWS, LANES), F32), pltpu.SemaphoreType.DMA((2,))],
        input_output_aliases={3: 0},
        compiler_params=_params("arbitrary"),
        name="moe_dispatch",
    )(pos, pos, x, jnp.zeros((n_rows * TOKEN_ROWS, LANES), F32))


EXPERT_HEAD_ROWS = EXPERT_TILE // 4


def _experts_kernel(te_ref, na_ref, nv_ref, x_ref, wg_ref, wu_ref, wd_ref, o_ref, xb_ref, acc_ref):
    i = pl.program_id(0)
    f = pl.program_id(1)
    active = i < na_ref[0]
    short = nv_ref[i] <= EXPERT_HEAD_ROWS

    @pl.when(active & (f == 0))
    def _():
        for j in range(TOKEN_ROWS):
            xb_ref[:, j * LANES:(j + 1) * LANES] = _from_token_tiles(x_ref, EXPERT_TILE, j).astype(BF16)
        acc_ref[...] = jnp.zeros_like(acc_ref)

    def swiglu(rows):
        xb = xb_ref[rows, :]
        gate = _dot(xb, wg_ref[...])
        up = _dot(xb, wu_ref[...])
        h = (gate * _sigmoid(gate) * up).astype(BF16)
        acc_ref[rows, :] += _dot(h, wd_ref[...])

    @pl.when(active & jnp.logical_not(short))
    def _():
        swiglu(slice(0, EXPERT_TILE))

    @pl.when(active & short)
    def _():
        swiglu(slice(0, EXPERT_HEAD_ROWS))

    @pl.when(f == pl.num_programs(1) - 1)
    def _():
        _to_token_tiles(o_ref, EXPERT_TILE, jnp.where(active, acc_ref[...], 0.0))


def _moe_experts(xs, tile_expert, n_active, tile_rows, wg, wu, wd, tf=512):
    d = D_MODEL
    n_tiles = xs.shape[0] // (EXPERT_TILE * TOKEN_ROWS)
    fe = wg.shape[2]
    row = pl.BlockSpec((EXPERT_TILE * TOKEN_ROWS, LANES), lambda i, f, te, na, nv: (i, 0))
    return pl.pallas_call(
        _experts_kernel,
        grid_spec=pltpu.PrefetchScalarGridSpec(
            num_scalar_prefetch=3,
            grid=(n_tiles, fe // tf),
            in_specs=[row,
                      pl.BlockSpec((None, d, tf), lambda i, f, te, na, nv: (te[i], 0, f)),
                      pl.BlockSpec((None, d, tf), lambda i, f, te, na, nv: (te[i], 0, f)),
                      pl.BlockSpec((None, tf, d), lambda i, f, te, na, nv: (te[i], f, 0))],
            out_specs=row,
            scratch_shapes=[pltpu.VMEM((EXPERT_TILE, d), BF16), pltpu.VMEM((EXPERT_TILE, d), F32)]),
        out_shape=jax.ShapeDtypeStruct(xs.shape, F32),
        compiler_params=_params("parallel", "arbitrary"),
        name="moe_experts",
    )(tile_expert, n_active, tile_rows, xs, wg, wu, wd)


def _combine_kernel(pos_ref, nxt_ref, ys_hbm, x_ref, rec_ref, g_ref, b_ref, o_ref, y_ref, sem):
    i = pl.program_id(0)
    n = pl.num_programs(0)
    slot = i % 2

    def copies(p_ref, s, t):
        return (_token_copy(ys_hbm, p_ref[0, 0, t], y_ref.at[s, 0], t, sem.at[s]),
                _token_copy(ys_hbm, p_ref[0, 0, ROUTE_TILE + t], y_ref.at[s, 1], t, sem.at[s]))

    def start_all(p_ref, s):
        def body(t, carry):
            for cp in copies(p_ref, s, t):
                cp.start()
            return carry
        lax.fori_loop(0, ROUTE_TILE, body, 0, unroll=DMA_UNROLL)

    @pl.when(i == 0)
    def _():
        start_all(pos_ref, slot)

    @pl.when(i + 1 < n)
    def _():
        start_all(nxt_ref, 1 - slot)

    def wait_body(t, carry):
        for cp in copies(pos_ref, slot, t):
            cp.wait()
        return carry
    lax.fori_loop(0, ROUTE_TILE, wait_body, 0, unroll=DMA_UNROLL)

    rec = rec_ref[...]
    lane = lax.broadcasted_iota(jnp.int32, rec.shape, 1)
    g1 = jnp.sum(jnp.where(lane == REC_G1, rec, 0.0), axis=-1, keepdims=True)
    g2 = jnp.sum(jnp.where(lane == REC_G2, rec, 0.0), axis=-1, keepdims=True)
    y1_ref = y_ref.at[slot, 0]
    y2_ref = y_ref.at[slot, 1]
    f = jnp.concatenate([g1 * _from_token_tiles(y1_ref, ROUTE_TILE, j) + g2 * _from_token_tiles(y2_ref, ROUTE_TILE, j)
                         for j in range(TOKEN_ROWS)], axis=1)
    o_ref[...] = _layer_norm(ALPHA * x_ref[...] + f, g_ref[...], b_ref[...])


def _moe_combine_ln(ys, pos, x, rec, g, b):
    t, d = x.shape
    n = t // ROUTE_TILE
    row = pl.BlockSpec((ROUTE_TILE, d), lambda i: (i, 0))
    vec = pl.BlockSpec((1, d), lambda i: (0, 0))
    return pl.pallas_call(
        _combine_kernel,
        grid=(n,),
        in_specs=[pl.BlockSpec((1, 1, 2 * ROUTE_TILE), lambda i: (i, 0, 0), memory_space=pltpu.SMEM),
                  pl.BlockSpec((1, 1, 2 * ROUTE_TILE), lambda i: (jnp.minimum(i + 1, n - 1), 0, 0),
                               memory_space=pltpu.SMEM),
                  pl.BlockSpec(memory_space=pl.ANY),
                  row,
                  pl.BlockSpec((ROUTE_TILE, LANES), lambda i: (i, 0)),
                  vec, vec],
        out_specs=row,
        out_shape=jax.ShapeDtypeStruct((t, d), F32),
        scratch_shapes=[pltpu.VMEM((2, 2, ROUTE_TILE * TOKEN_ROWS, LANES), F32), pltpu.SemaphoreType.DMA((2,))],
        compiler_params=_params("arbitrary"),
        name="moe_combine_ln",
    )(pos, pos, ys, x, rec, g, b)


def _moe_res_ln(x, rec, counts, wg, wu, wd, g, b):
    t, d = x.shape
    n_tiles = 2 * t // EXPERT_TILE + N_EXPERTS
    cnt = counts[0, :N_EXPERTS].astype(jnp.int32)
    padded = (cnt + EXPERT_TILE - 1) // EXPERT_TILE * EXPERT_TILE
    ends = jnp.cumsum(padded)
    starts = ends - padded
    e1 = rec[:, REC_E1].astype(jnp.int32)
    e2 = rec[:, REC_E2].astype(jnp.int32)
    pos1 = starts[e1] + rec[:, REC_R1].astype(jnp.int32)
    pos2 = starts[e2] + rec[:, REC_R2].astype(jnp.int32)
    pos = jnp.concatenate([pos1.reshape(-1, 1, ROUTE_TILE), pos2.reshape(-1, 1, ROUTE_TILE)], axis=2)
    tile_start = jnp.arange(n_tiles, dtype=jnp.int32) * EXPERT_TILE
    tile_expert = jnp.minimum(jnp.sum(tile_start[:, None] >= ends[None, :], axis=1), N_EXPERTS - 1).astype(jnp.int32)
    n_active = (ends[-1:] // EXPERT_TILE).astype(jnp.int32)
    tile_rows = jnp.clip(starts[tile_expert] + cnt[tile_expert] - tile_start, 0, EXPERT_TILE).astype(jnp.int32)
    last_expert = tile_expert[jnp.maximum(n_active[0] - 1, 0)]
    tile_expert = jnp.where(jnp.arange(n_tiles) < n_active[0], tile_expert, last_expert)

    xs = _moe_dispatch(x, pos, n_tiles * EXPERT_TILE)
    ys = _moe_experts(xs, tile_expert, n_active, tile_rows, wg, wu, wd)
    return _moe_combine_ln(ys, pos, x, rec, g, b)


def _prepare_weights(na_w_qkv, na_rpb, na_w_o, ffn_w_gate, ffn_w_up, ffn_w_down,
                     rwkv_mu, rwkv_w_rkv, rwkv_w0, rwkv_w1, rwkv_w2, rwkv_a0, rwkv_a1, rwkv_a2,
                     rwkv_g1, rwkv_g2, rwkv_k_k, rwkv_k_a, rwkv_r_k, rwkv_lnx_g, rwkv_lnx_b, rwkv_w_o,
                     moe_w_router, moe_b_router, moe_w_gate, moe_w_up, moe_w_down,
                     ln_mix_g, ln_mix_b, ln_ffn_g, ln_ffn_b):
    d = D_MODEL
    vec = lambda a: a.reshape(1, d).astype(F32)
    gl_pad = 2 * LANES - GATE_LORA
    router_w = jnp.pad(moe_w_router[0], ((0, 0), (0, LANES - N_EXPERTS)))
    router_hi = router_w.astype(BF16)
    router_lo = (router_w - router_hi.astype(F32)).astype(BF16)
    return dict(
        qkv_w=na_w_qkv[0].astype(BF16),
        qkv_scale=jnp.concatenate([jnp.full((1, d), NA_HEAD_DIM ** -0.5 * LOG2E, F32), jnp.ones((1, 2 * d), F32)], axis=1),
        na_bias=_na_bias_table(na_rpb[0]),
        na_wo=na_w_o[0].astype(BF16),
        ffn_wg=ffn_w_gate[0].astype(BF16), ffn_wu=ffn_w_up[0].astype(BF16), ffn_wd=ffn_w_down[0].astype(BF16),
        mu=rwkv_mu[0].astype(F32),
        wrkv=rwkv_w_rkv[0].astype(BF16),
        w1=jnp.concatenate([rwkv_w1[0, 0], rwkv_w1[0, 1]], axis=1).astype(BF16),
        w2=jnp.concatenate([rwkv_w2[0, 0], rwkv_w2[0, 1]], axis=0).astype(BF16),
        w0=rwkv_w0[0].astype(F32),
        a1=jnp.concatenate([rwkv_a1[0, 0], rwkv_a1[0, 1]], axis=1).astype(BF16),
        a2=jnp.concatenate([rwkv_a2[0, 0], rwkv_a2[0, 1]], axis=0).astype(BF16),
        a0=rwkv_a0[0].astype(F32),
        g1=jnp.pad(rwkv_g1[0], ((0, 0), (0, gl_pad))).astype(BF16),
        g2=jnp.pad(rwkv_g2[0], ((0, gl_pad), (0, 0))).astype(BF16),
        k_k=vec(rwkv_k_k[0]), k_a=vec(rwkv_k_a[0]), r_k=vec(rwkv_r_k[0]),
        lnx_g=vec(rwkv_lnx_g[0]), lnx_b=vec(rwkv_lnx_b[0]),
        rwkv_wo=rwkv_w_o[0].astype(BF16),
        router_w=jnp.stack([router_hi, router_lo]),
        router_b=jnp.pad(moe_b_router[0], (0, LANES - N_EXPERTS)).reshape(1, LANES).astype(F32),
        moe_wg=moe_w_gate[0].astype(BF16), moe_wu=moe_w_up[0].astype(BF16), moe_wd=moe_w_down[0].astype(BF16),
        ln_mix_g=ln_mix_g.astype(F32), ln_mix_b=ln_mix_b.astype(F32),
        ln_ffn_g=ln_ffn_g.astype(F32), ln_ffn_b=ln_ffn_b.astype(F32),
    )


def _trunk(x3, w):
    batch, seq_len, d = x3.shape
    rows = seq_len // GRID_W
    x = x3.reshape(batch * seq_len, d)
    ln = lambda a, i: a[i].reshape(1, d)

    qkv = _qkv_proj(x, w["qkv_w"], w["qkv_scale"])
    att = _na_attention(qkv, w["na_bias"], batch, rows)
    x = _proj_res_ln(att, w["na_wo"], x, ln(w["ln_mix_g"], 0), ln(w["ln_mix_b"], 0))
    x = _ffn_res_ln(x, w["ffn_wg"], w["ffn_wu"], w["ffn_wd"], ln(w["ln_ffn_g"], 0), ln(w["ln_ffn_b"], 0))

    r, k, kk, v, g, lw0, lw1, ia0, ia1 = _rwkv_proj(x, seq_len, w["mu"], w["wrkv"], w["w1"], w["w2"], w["w0"],
                                                    w["a1"], w["a2"], w["a0"], w["g1"], w["g2"], w["k_k"])
    yf, yb = _rwkv_scan(r, k, kk, v, lw0, lw1, ia0, ia1, w["k_a"], batch, seq_len)
    x = _rwkv_out(yf, yb, r, k, v, g, ia0, ia1, x, w["lnx_g"], w["lnx_b"], w["r_k"], w["k_a"], w["rwkv_wo"],
                  ln(w["ln_mix_g"], 1), ln(w["ln_mix_b"], 1))
    rec, counts = _router(x, w["router_w"], w["router_b"])
    x = _moe_res_ln(x, rec, counts, w["moe_wg"], w["moe_wu"], w["moe_wd"], ln(w["ln_ffn_g"], 1), ln(w["ln_ffn_b"], 1))
    return x.reshape(batch, seq_len, d)


def kernel(x_prompt, x_sample, na_w_qkv, na_rpb, na_w_o, ffn_w_gate, ffn_w_up, ffn_w_down, rwkv_mu, rwkv_w_rkv, rwkv_w0, rwkv_w1, rwkv_w2, rwkv_a0, rwkv_a1, rwkv_a2, rwkv_g1, rwkv_g2, rwkv_k_k, rwkv_k_a, rwkv_r_k, rwkv_lnx_g, rwkv_lnx_b, rwkv_w_o, moe_w_router, moe_b_router, moe_w_gate, moe_w_up, moe_w_down, ln_mix_g, ln_mix_b, ln_ffn_g, ln_ffn_b):
    w = _prepare_weights(na_w_qkv, na_rpb, na_w_o, ffn_w_gate, ffn_w_up, ffn_w_down,
                         rwkv_mu, rwkv_w_rkv, rwkv_w0, rwkv_w1, rwkv_w2, rwkv_a0, rwkv_a1, rwkv_a2,
                         rwkv_g1, rwkv_g2, rwkv_k_k, rwkv_k_a, rwkv_r_k, rwkv_lnx_g, rwkv_lnx_b, rwkv_w_o,
                         moe_w_router, moe_b_router, moe_w_gate, moe_w_up, moe_w_down,
                         ln_mix_g, ln_mix_b, ln_ffn_g, ln_ffn_b)
    return (_trunk(x_prompt, w), _trunk(x_sample, w))
```
